```python
import math
import jax, jax.numpy as jnp
from jax import lax
import numpy as np

D_MODEL = 2048
BATCH = 2
SEQ = 8192
DEPTH = 2
DEC_BATCH = 8
DEC_SEQ = 4096
PAST_LEN = 128

GRID_W = 64
MIX_W = D_MODEL
GROUP_W = MIX_W // 4
D_FF = 5632
NORM_EPS = 1e-6

MLSTM_HEADS = 4
MLSTM_DH = GROUP_W // MLSTM_HEADS
MLSTM_CHUNK = 64
MLSTM_CONV = 5
MLSTM_GATES = 4 * MLSTM_HEADS

DIFF_HEADS = 4
DIFF_DH = GROUP_W // (2 * DIFF_HEADS)
Q_BLOCK = 128
ROPE_THETA = 10000.0

NA_HEADS = 8
NA_DH = GROUP_W // NA_HEADS
NA_WIN_ROWS = 8
NA_WIN_W = 16

S5_GROUP_CH = 16
S5_GROUPS = GROUP_W // S5_GROUP_CH
S5_STATE = 64

SPLIT_SIZES = (GROUP_W, GROUP_W, GROUP_W, GROUP_W, MLSTM_GATES, GROUP_W, GROUP_W, GROUP_W, GROUP_W, GROUP_W, GROUP_W, GROUP_W)
IN_W = 11 * GROUP_W + MLSTM_GATES

kernel_name = 'hybrid_parallel_head_encoder'


def rmsnorm(x, g):
    xf = x.astype(jnp.float32)
    y = xf * lax.rsqrt(jnp.mean(xf * xf, axis=-1, keepdims=True) + NORM_EPS)
    return (y * g.astype(jnp.float32)).astype(x.dtype)


def swiglu(x, wg, wu, wd):
    return (jax.nn.silu(x @ wg) * (x @ wu)) @ wd


def rope_tables(T):
    inv = 1.0 / (ROPE_THETA ** (jnp.arange(0, DIFF_DH, 2, dtype=jnp.float32) / DIFF_DH))
    ang = jnp.arange(T, dtype=jnp.float32)[:, None] * inv[None, :]
    return jnp.cos(ang), jnp.sin(ang)


def apply_rope(x, cos, sin):
    half = DIFF_DH // 2
    xf = x.astype(jnp.float32)
    x1, x2 = xf[..., :half], xf[..., half:]
    c = cos[None, :, None, None, :]
    s = sin[None, :, None, None, :]
    return jnp.concatenate([x1 * c - x2 * s, x2 * c + x1 * s], axis=-1).astype(x.dtype)


def centred_dwconv(x, w, b):
    K = w.shape[0]
    pad = K // 2
    T = x.shape[1]
    xp = jnp.pad(x, ((0, 0), (pad, pad), (0, 0)))
    acc = xp[:, 0:T] * w[0]
    for j in range(1, K):
        acc = acc + xp[:, j:j + T] * w[j]
    return acc + b


def mlstm_direction(q, k, v, ig, fg):
    Bsz, H, T, d = q.shape
    L = MLSTM_CHUNK
    nc = T // L
    qc = q.reshape(Bsz, H, nc, L, d)
    kc = k.reshape(Bsz, H, nc, L, d)
    vc = v.reshape(Bsz, H, nc, L, d)
    ic = ig.reshape(Bsz, H, nc, L)
    b = jnp.cumsum(jax.nn.log_sigmoid(fg).reshape(Bsz, H, nc, L), axis=-1)
    g = b[..., -1]
    a = g[..., None] - b + ic

    def step(carry, xs):
        C, n, m = carry
        k_j, v_j, a_j, g_j = xs
        m_new = jnp.maximum(g_j + m, jnp.max(a_j, axis=-1))
        decay = jnp.exp(g_j + m - m_new)
        w = jnp.exp(a_j - m_new[..., None])
        C_new = decay[..., None, None] * C + jnp.einsum('bhl,bhld,bhle->bhde', w, k_j, v_j)
        n_new = decay[..., None] * n + jnp.einsum('bhl,bhld->bhd', w, k_j)
        return (C_new, n_new, m_new), (C, n, m)

    init = (jnp.zeros((Bsz, H, d, d), jnp.float32), jnp.zeros((Bsz, H, d), jnp.float32), jnp.zeros((Bsz, H), jnp.float32))
    xs = (jnp.moveaxis(kc, 2, 0), jnp.moveaxis(vc, 2, 0), jnp.moveaxis(a, 2, 0), jnp.moveaxis(g, 2, 0))
    _, (C_prev, n_prev, m_prev) = lax.scan(step, init, xs)
    C_prev = jnp.moveaxis(C_prev, 0, 2)
    n_prev = jnp.moveaxis(n_prev, 0, 2)
    m_prev = jnp.moveaxis(m_prev, 0, 2)

    inter = b + m_prev[..., None]
    mask = jnp.tril(jnp.ones((L, L), dtype=bool))
    dlog = jnp.where(mask, b[..., :, None] - b[..., None, :] + ic[..., None, :], -jnp.inf)
    m_t = jnp.maximum(inter, jnp.max(dlog, axis=-1))
    S = jnp.einsum('bhnld,bhnsd->bhnls', qc, kc) * jnp.exp(dlog - m_t[..., None])
    s_inter = jnp.exp(inter - m_t)
    num = s_inter[..., None] * jnp.einsum('bhnld,bhnde->bhnle', qc, C_prev) + jnp.einsum('bhnls,bhnse->bhnle', S, vc)
    den = jnp.abs(s_inter * jnp.einsum('bhnld,bhnd->bhnl', qc, n_prev) + jnp.sum(S, axis=-1))
    den = jnp.maximum(den, jnp.exp(-m_t))
    return (num / den[..., None]).reshape(Bsz, H, T, d)


def mlstm_mixer(q, k, v, o, gates, conv_w, conv_b, gate_bias, norm_g):
    Bsz, T, _ = q.shape
    qk = jax.nn.silu(centred_dwconv(jnp.concatenate([q, k], axis=-1), conv_w, conv_b))
    q, k = qk[..., :GROUP_W], qk[..., GROUP_W:]

    def heads(t):
        return t.astype(jnp.float32).reshape(Bsz, T, MLSTM_HEADS, MLSTM_DH).transpose(0, 2, 1, 3)

    qh, kh, vh = heads(q), heads(k) * (MLSTM_DH ** -0.5), heads(v)
    gt = (gates.astype(jnp.float32) + gate_bias.astype(jnp.float32)).transpose(0, 2, 1)
    ig_f, fg_f, ig_b, fg_b = jnp.split(gt, 4, axis=1)
    h_f = mlstm_direction(qh, kh, vh, ig_f, fg_f)
    fl = lambda t: jnp.flip(t, axis=2)
    h_b = fl(mlstm_direction(fl(qh), fl(kh), fl(vh), fl(ig_b), fl(fg_b)))
    h = (h_f + h_b).transpose(0, 2, 1, 3)
    h = h * lax.rsqrt(jnp.mean(h * h, axis=-1, keepdims=True) + NORM_EPS)
    h = h * norm_g.astype(jnp.float32).reshape(MLSTM_HEADS, MLSTM_DH)
    h = h.reshape(Bsz, T, GROUP_W)
    return (h * jax.nn.sigmoid(o.astype(jnp.float32))).astype(o.dtype)


def diff_attention(q, k, v, lam_params, norm_g, cos, sin, layer_idx):
    Bsz, T, _ = q.shape
    qh = apply_rope(q.reshape(Bsz, T, DIFF_HEADS, 2, DIFF_DH), cos, sin)
    kh = apply_rope(k.reshape(Bsz, T, DIFF_HEADS, 2, DIFF_DH), cos, sin)
    vh = v.reshape(Bsz, T, DIFF_HEADS, 2 * DIFF_DH)
    lam_init = 0.8 - 0.6 * math.exp(-0.3 * layer_idx)
    lp = lam_params.astype(jnp.float32)
    lam = jnp.exp(jnp.sum(lp[0] * lp[1])) - jnp.exp(jnp.sum(lp[2] * lp[3])) + lam_init
    scale = DIFF_DH ** -0.5
    nb = T // Q_BLOCK
    qb = jnp.moveaxis(qh.reshape(Bsz, nb, Q_BLOCK, DIFF_HEADS, 2, DIFF_DH), 1, 0)

    def block(q_blk):
        s = jnp.einsum('bqhid,bkhid->bhiqk', q_blk, kh).astype(jnp.float32) * scale
        p = jax.nn.softmax(s, axis=-1)
        w = p[:, :, 0] - lam * p[:, :, 1]
        return jnp.einsum('bhqk,bkhe->bqhe', w.astype(vh.dtype), vh)

    out = lax.map(block, qb)
    out = jnp.moveaxis(out, 0, 1).reshape(Bsz, T, DIFF_HEADS, 2 * DIFF_DH)
    out = rmsnorm(out, norm_g) * (1.0 - lam_init)
    return out.reshape(Bsz, T, GROUP_W)


def neighbourhood_attention(q, k, v, rpb):
    Bsz, T, _ = q.shape
    rows = T // GRID_W
    kr = min(NA_WIN_ROWS, rows)
    qg = q.reshape(Bsz, rows, GRID_W, NA_HEADS, NA_DH)
    kg = k.reshape(Bsz, rows, GRID_W, NA_HEADS, NA_DH)
    vg = v.reshape(Bsz, rows, GRID_W, NA_HEADS, NA_DH)
    r_idx = jnp.arange(rows)
    row_start = jnp.clip(r_idx - kr // 2, 0, rows - kr)
    c_idx = jnp.arange(GRID_W)
    col_start = jnp.clip(c_idx - NA_WIN_W // 2, 0, GRID_W - NA_WIN_W)
    col_idx = col_start[:, None] + jnp.arange(NA_WIN_W)[None, :]
    col_off = col_idx - c_idx[:, None] + (NA_WIN_W - 1)
    col_bias = rpb.astype(jnp.float32)[:, :, col_off]
    scale = NA_DH ** -0.5

    def row_block(args):
        r, rs, q_r = args
        k_rows = lax.dynamic_slice_in_dim(kg, rs, kr, axis=1)
        v_rows = lax.dynamic_slice_in_dim(vg, rs, kr, axis=1)
        k_sel = k_rows[:, :, col_idx]
        v_sel = v_rows[:, :, col_idx]
        row_off = rs + jnp.arange(kr) - r + (NA_WIN_ROWS - 1)
        bias = col_bias[:, row_off].transpose(0, 2, 1, 3)
        s = jnp.einsum('bchd,bkcwhd->bhckw', q_r, k_sel).astype(jnp.float32) * scale + bias[None]
        p = jax.nn.softmax(s.reshape(Bsz, NA_HEADS, GRID_W, kr * NA_WIN_W), axis=-1)
        p = p.reshape(Bsz, NA_HEADS, GRID_W, kr, NA_WIN_W)
        return jnp.einsum('bhckw,bkcwhd->bchd', p.astype(v_sel.dtype), v_sel)

    out = lax.map(row_block, (r_idx, row_start, jnp.moveaxis(qg, 1, 0)))
    return jnp.moveaxis(out, 0, 1).reshape(Bsz, T, GROUP_W)


def ssm_combine(e1, e2):
    a1, b1 = e1
    a2, b2 = e2
    return a2 * a1, a2 * b1 + b2


def s5_mixer(u, lam_re, lam_im, log_step, b_re, b_im, c_re, c_im, d_skip, glu_w, glu_b):
    Bsz, T, _ = u.shape
    f32 = jnp.float32
    u_t = jnp.moveaxis(u.astype(f32).reshape(Bsz, T, S5_GROUPS, S5_GROUP_CH), 1, 0)
    u_c = u_t.astype(jnp.complex64)
    b_mat = lax.complex(b_re.astype(f32), b_im.astype(f32))
    c_mat = lax.complex(c_re.astype(f32), c_im.astype(f32))

    def run(direction, seq):
        lam = lax.complex(lam_re[direction].astype(f32), lam_im[direction].astype(f32))
        step = jnp.exp(log_step[direction].astype(f32))[:, None]
        lam_bar = jnp.exp(lam * step)
        b_bar = ((lam_bar - 1.0) / lam)[..., None] * b_mat
        bu = jnp.einsum('tbgc,gpc->tbgp', seq, b_bar)
        a = jnp.broadcast_to(lam_bar[None, None], (T, 1, S5_GROUPS, S5_STATE))
        _, states = lax.associative_scan(ssm_combine, (a, bu), axis=0)
        return states

    s = run(0, u_c) + jnp.flip(run(1, jnp.flip(u_c, axis=0)), axis=0)
    y = jnp.einsum('tbgp,gcp->tbgc', s, c_mat).real + d_skip.astype(f32) * u_t
    y = jnp.moveaxis(y, 0, 1).reshape(Bsz, T, GROUP_W)
    z = jax.nn.gelu(y)
    out = z * jax.nn.sigmoid(z @ glu_w.astype(f32) + glu_b.astype(f32))
    return out.astype(u.dtype)


def encoder_trunk(x, w):
    T = x.shape[1]
    cos, sin = rope_tables(T)
    split_points = [int(p) for p in np.cumsum(SPLIT_SIZES)[:-1]]
    for l in range(DEPTH):
        x = x + 0.5 * swiglu(rmsnorm(x, w['ffn1_norm'][l]), w['ffn1_w_gate'][l], w['ffn1_w_up'][l], w['ffn1_w_down'][l])
        h = rmsnorm(x, w['mix_norm'][l])
        proj = h @ w['w_in'][l]
        mq, mk, mv, mo, mg, dq, dk, dv, nq, nk, nv, su = jnp.split(proj, split_points, axis=-1)
        y_a = mlstm_mixer(mq, mk, mv, mo, mg, w['mlstm_conv_w'][l], w['mlstm_conv_b'][l], w['mlstm_gate_bias'][l], w['mlstm_norm'][l])
        y_b = diff_attention(dq, dk, dv, w['diff_lambda'][l], w['diff_norm'][l], cos, sin, l)
        y_c = neighbourhood_attention(nq, nk, nv, w['na_rpb'][l])
        y_d = s5_mixer(su, w['s5_lambda_re'][l], w['s5_lambda_im'][l], w['s5_log_step'][l], w['s5_b_re'][l], w['s5_b_im'][l], w['s5_c_re'][l], w['s5_c_im'][l], w['s5_d'][l], w['s5_glu_w'][l], w['s5_glu_b'][l])
        mixed = jnp.concatenate([y_a, y_b.astype(x.dtype), y_c.astype(x.dtype), y_d], axis=-1)
        x = x + mixed @ w['w_out'][l]
        x = x + 0.5 * swiglu(rmsnorm(x, w['ffn2_norm'][l]), w['ffn2_w_gate'][l], w['ffn2_w_up'][l], w['ffn2_w_down'][l])
    return rmsnorm(x, w['final_norm'])


def setup_inputs(seed: int = 0) -> dict:
    key = jax.random.key(seed)
    ks = jax.random.split(key, 40)
    f32 = jnp.float32
    nrm = lambda k, shape, scale: jax.random.normal(k, shape, f32) * scale
    gain = lambda k, shape: 1.0 + 0.02 * jax.random.normal(k, shape, f32)
    ib = nrm(ks[10], (DEPTH, 2, MLSTM_HEADS), 0.1)
    fb = jnp.linspace(3.0, 6.0, MLSTM_HEADS, dtype=f32)[None, None, :] + nrm(ks[11], (DEPTH, 2, MLSTM_HEADS), 0.1)
    mlstm_gate_bias = jnp.concatenate([ib[:, 0], fb[:, 0], ib[:, 1], fb[:, 1]], axis=-1)
    s5_lambda_im = math.pi * jnp.arange(S5_STATE, dtype=f32) + nrm(ks[16], (DEPTH, 2, S5_GROUPS, S5_STATE), 0.01)
    return {
        'x_prompt': nrm(ks[0], (BATCH, SEQ, D_MODEL), 1.0),
        'x_sample': nrm(ks[1], (DEC_BATCH, DEC_SEQ, D_MODEL), 1.0),
        'ffn1_norm': gain(ks[2], (DEPTH, D_MODEL)),
        'ffn1_w_gate': nrm(ks[3], (DEPTH, D_MODEL, D_FF), D_MODEL ** -0.5),
        'ffn1_w_up': nrm(ks[4], (DEPTH, D_MODEL, D_FF), D_MODEL ** -0.5),
        'ffn1_w_down': nrm(ks[5], (DEPTH, D_FF, D_MODEL), D_FF ** -0.5),
        'mix_norm': gain(ks[6], (DEPTH, D_MODEL)),
        'w_in': nrm(ks[7], (DEPTH, D_MODEL, IN_W), D_MODEL ** -0.5),
        'mlstm_conv_w': nrm(ks[8], (DEPTH, MLSTM_CONV, 2 * GROUP_W), MLSTM_CONV ** -0.5),
        'mlstm_conv_b': nrm(ks[9], (DEPTH, 2 * GROUP_W), 0.02),
        'mlstm_gate_bias': mlstm_gate_bias,
        'mlstm_norm': gain(ks[12], (DEPTH, GROUP_W)),
        'diff_lambda': nrm(ks[13], (DEPTH, 4, DIFF_DH), 0.1),
        'diff_norm': gain(ks[14], (DEPTH, 2 * DIFF_DH)),
        'na_rpb': nrm(ks[15], (DEPTH, NA_HEADS, 2 * NA_WIN_ROWS - 1, 2 * NA_WIN_W - 1), 0.02),
        's5_lambda_re': -0.5 + nrm(ks[17], (DEPTH, 2, S5_GROUPS, S5_STATE), 0.01),
        's5_lambda_im': s5_lambda_im,
        's5_log_step': jax.random.uniform(ks[18], (DEPTH, 2, S5_GROUPS), f32, math.log(1e-3), math.log(1e-1)),
        's5_b_re': nrm(ks[19], (DEPTH, S5_GROUPS, S5_STATE, S5_GROUP_CH), (2 * S5_GROUP_CH) ** -0.5),
        's5_b_im': nrm(ks[20], (DEPTH, S5_GROUPS, S5_STATE, S5_GROUP_CH), (2 * S5_GROUP_CH) ** -0.5),
        's5_c_re': nrm(ks[21], (DEPTH, S5_GROUPS, S5_GROUP_CH, S5_STATE), (2 * S5_STATE) ** -0.5),
        's5_c_im': nrm(ks[22], (DEPTH, S5_GROUPS, S5_GROUP_CH, S5_STATE), (2 * S5_STATE) ** -0.5),
        's5_d': nrm(ks[23], (DEPTH, S5_GROUPS, S5_GROUP_CH), 1.0),
        's5_glu_w': nrm(ks[24], (DEPTH, GROUP_W, GROUP_W), GROUP_W ** -0.5),
        's5_glu_b': nrm(ks[25], (DEPTH, GROUP_W), 0.02),
        'w_out': nrm(ks[26], (DEPTH, MIX_W, D_MODEL), MIX_W ** -0.5),
        'ffn2_norm': gain(ks[27], (DEPTH, D_MODEL)),
        'ffn2_w_gate': nrm(ks[28], (DEPTH, D_MODEL, D_FF), D_MODEL ** -0.5),
        'ffn2_w_up': nrm(ks[29], (DEPTH, D_MODEL, D_FF), D_MODEL ** -0.5),
        'ffn2_w_down': nrm(ks[30], (DEPTH, D_FF, D_MODEL), D_FF ** -0.5),
        'final_norm': gain(ks[31], (D_MODEL,)),
    }


def reference(x_prompt, x_sample, ffn1_norm, ffn1_w_gate, ffn1_w_up, ffn1_w_down, mix_norm, w_in, mlstm_conv_w, mlstm_conv_b, mlstm_gate_bias, mlstm_norm, diff_lambda, diff_norm, na_rpb, s5_lambda_re, s5_lambda_im, s5_log_step, s5_b_re, s5_b_im, s5_c_re, s5_c_im, s5_d, s5_glu_w, s5_glu_b, w_out, ffn2_norm, ffn2_w_gate, ffn2_w_up, ffn2_w_down, final_norm):
    weights = dict(ffn1_norm=ffn1_norm, ffn1_w_gate=ffn1_w_gate, ffn1_w_up=ffn1_w_up, ffn1_w_down=ffn1_w_down,
                   mix_norm=mix_norm, w_in=w_in, mlstm_conv_w=mlstm_conv_w, mlstm_conv_b=mlstm_conv_b,
                   mlstm_gate_bias=mlstm_gate_bias, mlstm_norm=mlstm_norm, diff_lambda=diff_lambda, diff_norm=diff_norm,
                   na_rpb=na_rpb, s5_lambda_re=s5_lambda_re, s5_lambda_im=s5_lambda_im, s5_log_step=s5_log_step,
                   s5_b_re=s5_b_re, s5_b_im=s5_b_im, s5_c_re=s5_c_re, s5_c_im=s5_c_im, s5_d=s5_d,
                   s5_glu_w=s5_glu_w, s5_glu_b=s5_glu_b, w_out=w_out, ffn2_norm=ffn2_norm, ffn2_w_gate=ffn2_w_gate,
                   ffn2_w_up=ffn2_w_up, ffn2_w_down=ffn2_w_down, final_norm=final_norm)
    y_prompt = encoder_trunk(x_prompt, weights)
    y_sample = encoder_trunk(x_sample, weights)
    return (y_prompt, y_sample)
```

```python
import functools
import math

import jax
import jax.numpy as jnp
from jax import lax
from jax.experimental import pallas as pl
from jax.experimental.pallas import tpu as pltpu

F32 = jnp.float32
BF16 = jnp.bfloat16

NORM_EPS = 1e-6
GROUP_W = 512
GRID_W = 64
MLSTM_HEADS = 4
MLSTM_DH = 128
MLSTM_CONV = 5
MLSTM_CHUNK = 128
DIFF_HEADS = 4
DIFF_DH = 64
ROPE_THETA = 10000.0
NA_HEADS = 8
NA_DH = 64
NA_WIN_ROWS = 8
NA_WIN_W = 16
NA_ROWS_PER_STEP = 8
S5_GROUP_CH = 16
S5_GROUPS = 32
S5_STATE = 64
S5_CHUNK = 32
LANES = 128
BF16_SUBLANES = 16
MASK_VALUE = -1e30
VMEM_LIMIT = 56 * 1024 * 1024


def _cparams(*sem):
    return pltpu.CompilerParams(dimension_semantics=sem, vmem_limit_bytes=VMEM_LIMIT)


def _rms(x, g):
    return x * lax.rsqrt(jnp.mean(x * x, axis=-1, keepdims=True) + NORM_EPS) * g


def _sigmoid(x):
    return 1.0 / (1.0 + jnp.exp(-x))


def _dot(a, b):
    return jnp.dot(a, b, preferred_element_type=F32)


def _dot_nt(a, b):
    return lax.dot_general(a, b, (((1,), (1,)), ((), ())), preferred_element_type=F32)


def _dot_tn(a, b):
    return lax.dot_general(a, b, (((0,), (0,)), ((), ())), preferred_element_type=F32)


def _split3(x):
    hi = x.astype(BF16)
    r1 = x - hi.astype(F32)
    mid = r1.astype(BF16)
    lo = (r1 - mid.astype(F32)).astype(BF16)
    return hi, mid, lo


def _ffn_kernel(x_ref, g_ref, wg_ref, wu_ref, wd_ref, *rest, final):
    if final:
        fg_ref, o_ref, h_ref = rest
    else:
        o_ref, h_ref = rest
    f = pl.program_id(1)
    nf = pl.num_programs(1)

    @pl.when(f == 0)
    def _():
        h_ref[...] = _rms(x_ref[...], g_ref[...]).astype(BF16)

    h = h_ref[...]
    gate = _dot(h, wg_ref[...])
    up = _dot(h, wu_ref[...])
    act = (gate * _sigmoid(gate) * up).astype(BF16)
    part = _dot(act, wd_ref[...])

    @pl.when(f == 0)
    def _():
        o_ref[...] = part

    @pl.when(f > 0)
    def _():
        o_ref[...] += part

    @pl.when(f == nf - 1)
    def _():
        y = x_ref[...] + 0.5 * o_ref[...]
        if final:
            y = _rms(y, fg_ref[...])
        o_ref[...] = y


def _ffn(x, g, wg, wu, wd, final_g=None, tm=512, tf=512):
    M, D = x.shape
    F = wg.shape[1]
    final = final_g is not None
    in_specs = [
        pl.BlockSpec((tm, D), lambda i, f: (i, 0)),
        pl.BlockSpec((1, D), lambda i, f: (0, 0)),
        pl.BlockSpec((D, tf), lambda i, f: (0, f)),
        pl.BlockSpec((D, tf), lambda i, f: (0, f)),
        pl.BlockSpec((tf, D), lambda i, f: (f, 0)),
    ]
    args = [x, g.reshape(1, D), wg, wu, wd]
    if final:
        in_specs.append(pl.BlockSpec((1, D), lambda i, f: (0, 0)))
        args.append(final_g.reshape(1, D))
    return pl.pallas_call(
        functools.partial(_ffn_kernel, final=final),
        grid=(M // tm, F // tf),
        in_specs=in_specs,
        out_specs=pl.BlockSpec((tm, D), lambda i, f: (i, 0)),
        out_shape=jax.ShapeDtypeStruct((M, D), F32),
        scratch_shapes=[pltpu.VMEM((tm, D), BF16)],
        compiler_params=_cparams("parallel", "arbitrary"),
        name="ffn",
    )(*args)


def _inproj_kernel(x_ref, g_ref, w_ref, wgate_ref, o_ref, og_ref, h_ref):
    n = pl.program_id(1)

    @pl.when(n == 0)
    def _():
        h = _rms(x_ref[...], g_ref[...]).astype(BF16)
        h_ref[...] = h
        og_ref[...] = _dot(h, wgate_ref[...])

    o_ref[...] = _dot(h_ref[...], w_ref[...]).astype(BF16)


def _inproj(x, g, w_main, w_gate, tm=512, tn=1408):
    M, D = x.shape
    N = w_main.shape[1]
    return pl.pallas_call(
        _inproj_kernel,
        grid=(M // tm, N // tn),
        in_specs=[
            pl.BlockSpec((tm, D), lambda i, n: (i, 0)),
            pl.BlockSpec((1, D), lambda i, n: (0, 0)),
            pl.BlockSpec((D, tn), lambda i, n: (0, n)),
            pl.BlockSpec((D, LANES), lambda i, n: (0, 0)),
        ],
        out_specs=[
            pl.BlockSpec((tm, tn), lambda i, n: (i, n)),
            pl.BlockSpec((tm, LANES), lambda i, n: (i, 0)),
        ],
        out_shape=[
            jax.ShapeDtypeStruct((M, N), BF16),
            jax.ShapeDtypeStruct((M, LANES), F32),
        ],
        scratch_shapes=[pltpu.VMEM((tm, D), BF16)],
        compiler_params=_cparams("parallel", "arbitrary"),
        name="inproj",
    )(x, g.reshape(1, D), w_main, w_gate)


def _conv_kernel(x_ref, prev_ref, next_ref, w_ref, b_ref, q_ref, k_ref, *, blocks_per_seq):
    tb = x_ref.shape[0]
    halo = prev_ref.shape[0]
    pos = pl.program_id(0) % blocks_per_seq
    prev = jnp.where(pos != 0, prev_ref[...].astype(F32), 0.0)
    nxt = jnp.where(pos != blocks_per_seq - 1, next_ref[...].astype(F32), 0.0)
    ext = jnp.concatenate([prev, x_ref[...].astype(F32), nxt], axis=0)
    rows = tb + 2 * halo
    pad = MLSTM_CONV // 2
    acc = None
    for j in range(MLSTM_CONV):
        shift = (pad - j) % rows
        sh = ext if shift == 0 else pltpu.roll(ext, shift, axis=0)
        term = sh[halo:halo + tb] * w_ref[j:j + 1, :]
        acc = term if acc is None else acc + term
    acc = acc + b_ref[...]
    y = acc * _sigmoid(acc)
    q_ref[...] = y[:, :GROUP_W].astype(BF16)
    k_ref[...] = (y[:, GROUP_W:] * (MLSTM_DH ** -0.5)).astype(BF16)


def _mlstm_conv(proj, conv_w, conv_b, T, tb=256):
    M = proj.shape[0]
    C = 2 * GROUP_W
    halo = BF16_SUBLANES
    r = tb // halo
    nhalo = M // halo
    w = jnp.zeros((8, C), F32).at[:MLSTM_CONV].set(conv_w)
    return pl.pallas_call(
        functools.partial(_conv_kernel, blocks_per_seq=T // tb),
        grid=(M // tb,),
        in_specs=[
            pl.BlockSpec((tb, C), lambda i: (i, 0)),
            pl.BlockSpec((halo, C), lambda i: (jnp.maximum(i * r - 1, 0), 0)),
            pl.BlockSpec((halo, C), lambda i: (jnp.minimum((i + 1) * r, nhalo - 1), 0)),
            pl.BlockSpec((8, C), lambda i: (0, 0)),
            pl.BlockSpec((1, C), lambda i: (0, 0)),
        ],
        out_specs=[
            pl.BlockSpec((tb, GROUP_W), lambda i: (i, 0)),
            pl.BlockSpec((tb, GROUP_W), lambda i: (i, 0)),
        ],
        out_shape=[jax.ShapeDtypeStruct((M, GROUP_W), BF16)] * 2,
        compiler_params=_cparams("parallel"),
        name="mlstm_conv",
    )(proj, proj, proj, w, conv_b.reshape(1, C))


def _mlstm_kernel(qf_ref, kf_ref, vf_ref, gf_ref, qb_ref, kb_ref, vb_ref, gb_ref, bias_ref,
                  of_ref, ob_ref, c_ref, m_ref):
    L = qf_ref.shape[0]
    H, dh = MLSTM_HEADS, MLSTM_DH

    @pl.when(pl.program_id(1) == 0)
    def _():
        c_ref[...] = jnp.zeros_like(c_ref)
        m_ref[...] = jnp.zeros_like(m_ref)

    row = lax.broadcasted_iota(jnp.int32, (L, L), 0)
    col = lax.broadcasted_iota(jnp.int32, (L, L), 1)
    lane = lax.broadcasted_iota(jnp.int32, (L, dh), 1)
    ones_col = jnp.where(lane == 0, 1.0, 0.0).astype(BF16)

    for d, (q_ref, k_ref, v_ref, g_ref, o_ref) in enumerate(
            ((qf_ref, kf_ref, vf_ref, gf_ref, of_ref), (qb_ref, kb_ref, vb_ref, gb_ref, ob_ref))):
        seen = (col <= row) if d == 0 else (col >= row)
        tri = jnp.where(seen, 1.0, 0.0).astype(BF16)
        gt = g_ref[...] + bias_ref[...]
        lf = jnp.minimum(gt, 0.0) - jnp.log(1.0 + jnp.exp(-jnp.abs(gt)))
        hi, mid, lo = _split3(lf)
        bcum = _dot(tri, hi) + _dot(tri, mid) + _dot(tri, lo)
        bcum_t = bcum.T
        gt_t = gt.T
        end = L - 1 if d == 0 else 0
        for h in range(H):
            ic = 8 * d + h
            fc = 8 * d + 4 + h
            idx = d * H + h
            b_col = bcum[:, fc:fc + 1]
            b_row = bcum_t[fc:fc + 1, :]
            i_col = gt[:, ic:ic + 1]
            i_row = gt_t[ic:ic + 1, :]
            g_tot = bcum[end:end + 1, fc:fc + 1]
            m_prev = m_ref[idx][:, :1]
            c_prev = c_ref[idx]
            q = q_ref[:, h * dh:(h + 1) * dh]
            k = k_ref[:, h * dh:(h + 1) * dh]
            v_aug = jnp.concatenate([v_ref[:, h * dh:(h + 1) * dh], ones_col], axis=1)

            dlog = jnp.where(seen, b_col - b_row + i_row, -jnp.inf)
            inter = b_col + m_prev
            m_t = jnp.maximum(inter, jnp.max(dlog, axis=1, keepdims=True))
            s = (_dot_nt(q, k) * jnp.exp(dlog - m_t)).astype(BF16)
            s_inter = jnp.exp(inter - m_t)
            num_aug = s_inter * _dot(q, c_prev.astype(BF16)) + _dot(s, v_aug)
            den = jnp.maximum(jnp.abs(num_aug[:, dh:dh + 1]), jnp.exp(-m_t))
            o_ref[:, h * dh:(h + 1) * dh] = num_aug[:, :dh] / den

            a_col = g_tot - b_col + i_col
            m_new = jnp.maximum(g_tot + m_prev, jnp.max(a_col, axis=0, keepdims=True))
            decay = jnp.exp(g_tot + m_prev - m_new)
            kw = (k.astype(F32) * jnp.exp(a_col - m_new)).astype(BF16)
            c_ref[idx] = decay * c_prev + _dot_tn(kw, v_aug)
            m_ref[idx] = jnp.broadcast_to(m_new, (1, LANES))


def _mlstm(qc, kc, proj, gates, gate_bias, B, T):
    M = qc.shape[0]
    L = MLSTM_CHUNK
    nc = T // L
    W = GROUP_W
    vblk = 2
    fwd = lambda b, i: (b * nc + i, 0)
    bwd = lambda b, i: (b * nc + nc - 1 - i, 0)
    fwd_v = lambda b, i: (b * nc + i, vblk)
    bwd_v = lambda b, i: (b * nc + nc - 1 - i, vblk)
    bias = jnp.zeros((1, LANES), F32).at[0, :4 * MLSTM_HEADS].set(gate_bias)
    return pl.pallas_call(
        _mlstm_kernel,
        grid=(B, nc),
        in_specs=[
            pl.BlockSpec((L, W), fwd), pl.BlockSpec((L, W), fwd), pl.BlockSpec((L, W), fwd_v),
            pl.BlockSpec((L, LANES), fwd),
            pl.BlockSpec((L, W), bwd), pl.BlockSpec((L, W), bwd), pl.BlockSpec((L, W), bwd_v),
            pl.BlockSpec((L, LANES), bwd),
            pl.BlockSpec((1, LANES), lambda b, i: (0, 0)),
        ],
        out_specs=[pl.BlockSpec((L, W), fwd), pl.BlockSpec((L, W), bwd)],
        out_shape=[jax.ShapeDtypeStruct((M, W), F32)] * 2,
        scratch_shapes=[
            pltpu.VMEM((2 * MLSTM_HEADS, MLSTM_DH, 2 * MLSTM_DH), F32),
            pltpu.VMEM((2 * MLSTM_HEADS, 1, LANES), F32),
        ],
        compiler_params=_cparams("parallel", "arbitrary"),
        name="mlstm",
    )(qc, kc, proj, gates, qc, kc, proj, gates, bias)


def _rope_kernel(x_ref, cos_ref, sin_ref, q_ref, k_ref):
    x = x_ref[...].astype(F32)
    reps = x.shape[1] // LANES
    cos = jnp.concatenate([cos_ref[...]] * reps, axis=1)
    sin = jnp.concatenate([sin_ref[...]] * reps, axis=1)
    lane = lax.broadcasted_iota(jnp.int32, x.shape, 1)
    width = x.shape[1]
    half = DIFF_DH // 2
    rot = jnp.where(lane % DIFF_DH < half, pltpu.roll(x, width - half, axis=1), pltpu.roll(x, half, axis=1))
    y = x * cos + rot * sin
    q_ref[...] = (y[:, :GROUP_W] * (DIFF_DH ** -0.5)).astype(BF16)
    k_ref[...] = y[:, GROUP_W:].astype(BF16)


def _rope_tables(T):
    half = DIFF_DH // 2
    inv = 1.0 / (ROPE_THETA ** (jnp.arange(0, DIFF_DH, 2, dtype=F32) / DIFF_DH))
    ang = jnp.arange(T, dtype=F32)[:, None] * inv[None, :]
    cos, sin = jnp.cos(ang), jnp.sin(ang)
    cos_t = jnp.concatenate([cos, cos] * (LANES // DIFF_DH), axis=1)
    sin_t = jnp.concatenate([-sin, sin] * (LANES // DIFF_DH), axis=1)
    return cos_t, sin_t


def _rope(proj, cos_t, sin_t, T, tb=256):
    M = proj.shape[0]
    nb = T // tb
    return pl.pallas_call(
        _rope_kernel,
        grid=(M // tb,),
        in_specs=[
            pl.BlockSpec((tb, 2 * GROUP_W), lambda i: (i, 2)),
            pl.BlockSpec((tb, LANES), lambda i: (i % nb, 0)),
            pl.BlockSpec((tb, LANES), lambda i: (i % nb, 0)),
        ],
        out_specs=[pl.BlockSpec((tb, GROUP_W), lambda i: (i, 0))] * 2,
        out_shape=[jax.ShapeDtypeStruct((M, GROUP_W), BF16)] * 2,
        compiler_params=_cparams("parallel"),
        name="rope",
    )(proj, cos_t, sin_t)


def _diff_kernel(q_ref, k_ref, v_ref, lp_ref, g_ref, o_ref, m_ref, l_ref, acc_ref, *, lam_init):
    ki = pl.program_id(3)

    @pl.when(ki == 0)
    def _():
        m_ref[...] = jnp.full_like(m_ref, -jnp.inf)
        l_ref[...] = jnp.zeros_like(l_ref)
        acc_ref[...] = jnp.zeros_like(acc_ref)

    q = q_ref[...]
    k = k_ref[...]
    v = v_ref[...]
    lane = lax.broadcasted_iota(jnp.int32, q.shape, 1)
    for i in range(2):
        qi = jnp.where((lane < DIFF_DH) == (i == 0), q, jnp.zeros_like(q))
        s = _dot_nt(qi, k)
        m_old = m_ref[i]
        m_new = jnp.maximum(m_old, jnp.max(s, axis=1, keepdims=True))
        alpha = jnp.exp(m_old - m_new)
        p = jnp.exp(s - m_new)
        l_ref[i] = alpha * l_ref[i] + jnp.sum(p, axis=1, keepdims=True)
        acc_ref[i] = alpha * acc_ref[i] + _dot(p.astype(BF16), v)
        m_ref[i] = m_new

    @pl.when(ki == pl.num_programs(3) - 1)
    def _():
        lp = lp_ref[...]
        lam = (jnp.exp(jnp.sum(lp[0:1] * lp[1:2], axis=1, keepdims=True))
               - jnp.exp(jnp.sum(lp[2:3] * lp[3:4], axis=1, keepdims=True)) + lam_init)
        out = acc_ref[0] / l_ref[0] - lam * (acc_ref[1] / l_ref[1])
        o_ref[...] = (_rms(out, g_ref[...]) * (1.0 - lam_init)).astype(BF16)


def _diff_attn(qr, kr, proj, lam_params, norm_g, B, T, layer_idx, tq=512, tk=512):
    M = qr.shape[0]
    dv = 2 * DIFF_DH
    nq, nk = T // tq, T // tk
    vblk0 = (6 * GROUP_W) // dv
    lam_init = 0.8 - 0.6 * math.exp(-0.3 * layer_idx)
    return pl.pallas_call(
        functools.partial(_diff_kernel, lam_init=lam_init),
        grid=(B, DIFF_HEADS, nq, nk),
        in_specs=[
            pl.BlockSpec((tq, dv), lambda b, h, qi, ki: (b * nq + qi, h)),
            pl.BlockSpec((tk, dv), lambda b, h, qi, ki: (b * nk + ki, h)),
            pl.BlockSpec((tk, dv), lambda b, h, qi, ki: (b * nk + ki, vblk0 + h)),
            pl.BlockSpec((4, DIFF_DH), lambda b, h, qi, ki: (0, 0)),
            pl.BlockSpec((1, dv), lambda b, h, qi, ki: (0, 0)),
        ],
        out_specs=pl.BlockSpec((tq, dv), lambda b, h, qi, ki: (b * nq + qi, h)),
        out_shape=jax.ShapeDtypeStruct((M, GROUP_W), BF16),
        scratch_shapes=[
            pltpu.VMEM((2, tq, 1), F32),
            pltpu.VMEM((2, tq, 1), F32),
            pltpu.VMEM((2, tq, dv), F32),
        ],
        compiler_params=_cparams("parallel", "parallel", "parallel", "arbitrary"),
        name="diff_attn",
    )(qr, kr, proj, lam_params, norm_g.reshape(1, dv))


def _na_kernel(q_ref, k_ref, v_ref, bias_ref, o_ref, *, rows):
    g = pl.program_id(2)
    win = NA_WIN_ROWS * GRID_W
    lane = lax.broadcasted_iota(jnp.int32, (GRID_W, 2 * NA_DH), 1)
    first = lane < NA_DH
    for jr in range(NA_ROWS_PER_STEP):
        r = g * NA_ROWS_PER_STEP + jr
        rs = jnp.clip(r - NA_WIN_ROWS // 2, 0, rows - NA_WIN_ROWS)
        start = pl.multiple_of(rs * GRID_W, GRID_W)
        q = q_ref[jr * GRID_W:(jr + 1) * GRID_W, :]
        zero = jnp.zeros_like(q)
        q2 = jnp.concatenate([jnp.where(first, q, zero), jnp.where(first, zero, q)], axis=0)
        kw = k_ref[pl.ds(start, win), :]
        vw = v_ref[pl.ds(start, win), :]
        s = _dot_nt(q2, kw) * (NA_DH ** -0.5) + bias_ref[r - rs]
        e = jnp.exp(s - jnp.max(s, axis=1, keepdims=True))
        pv = _dot(e.astype(BF16), vw) / jnp.sum(e, axis=1, keepdims=True)
        o_ref[jr * GRID_W:(jr + 1) * GRID_W, :] = jnp.where(first, pv[:GRID_W], pv[GRID_W:]).astype(BF16)


def _na_bias_table(rpb):
    c = jnp.arange(GRID_W)
    col_start = jnp.clip(c - NA_WIN_W // 2, 0, GRID_W - NA_WIN_W)
    kc = jnp.arange(GRID_W)
    valid = (kc[None, :] >= col_start[:, None]) & (kc[None, :] < col_start[:, None] + NA_WIN_W)
    col_off = jnp.clip(kc[None, :] - c[:, None] + NA_WIN_W - 1, 0, 2 * NA_WIN_W - 2)
    p = jnp.arange(NA_WIN_ROWS)
    k = jnp.arange(NA_WIN_ROWS)
    row_off = k[None, :] - p[:, None] + NA_WIN_ROWS - 1
    t = rpb.astype(F32)[:, row_off][:, :, :, col_off]
    t = jnp.where(valid[None, None, None], t, MASK_VALUE)
    t = t.transpose(0, 1, 3, 2, 4).reshape(NA_HEADS, NA_WIN_ROWS, GRID_W, NA_WIN_ROWS * GRID_W)
    t = t.reshape(NA_HEADS // 2, 2, NA_WIN_ROWS, GRID_W, NA_WIN_ROWS * GRID_W).transpose(0, 2, 1, 3, 4)
    return t.reshape(NA_HEADS // 2, NA_WIN_ROWS, 2 * GRID_W, NA_WIN_ROWS * GRID_W)


def _na_attn(proj, bias_table, B, T):
    M = proj.shape[0]
    rows = T // GRID_W
    tq = NA_ROWS_PER_STEP * GRID_W
    ng = rows // NA_ROWS_PER_STEP
    hp = NA_HEADS // 2
    w = 2 * NA_DH
    qblk0, kblk0, vblk0 = (7 * GROUP_W) // w, (8 * GROUP_W) // w, (9 * GROUP_W) // w
    return pl.pallas_call(
        functools.partial(_na_kernel, rows=rows),
        grid=(B, hp, ng),
        in_specs=[
            pl.BlockSpec((tq, w), lambda b, h, g: (b * ng + g, qblk0 + h)),
            pl.BlockSpec((T, w), lambda b, h, g: (b, kblk0 + h)),
            pl.BlockSpec((T, w), lambda b, h, g: (b, vblk0 + h)),
            pl.BlockSpec((None, NA_WIN_ROWS, 2 * GRID_W, NA_WIN_ROWS * GRID_W), lambda b, h, g: (h, 0, 0, 0)),
        ],
        out_specs=pl.BlockSpec((tq, w), lambda b, h, g: (b * ng + g, h)),
        out_shape=jax.ShapeDtypeStruct((M, GROUP_W), BF16),
        compiler_params=_cparams("parallel", "parallel", "arbitrary"),
        name="na_attn",
    )(proj, proj, proj, bias_table)


def _s5_params(lam_re, lam_im, log_step, b_re, b_im, c_re, c_im, d_skip):
    L, c, P, G = S5_CHUNK, S5_GROUP_CH, S5_STATE, S5_GROUPS
    lam = lax.complex(lam_re.astype(F32), lam_im.astype(F32))
    step = jnp.exp(log_step.astype(F32))[..., None]
    lam_bar = jnp.exp(lam * step)
    b_mat = lax.complex(b_re.astype(F32), b_im.astype(F32))
    c_mat = lax.complex(c_re.astype(F32), c_im.astype(F32))
    b_bar = ((lam_bar - 1.0) / lam)[..., None] * b_mat[None]
    pw = lam_bar[:, None] ** jnp.arange(L + 1, dtype=F32)[None, :, None, None]
    kern = jnp.einsum('gop,dlgp,dgpi->dlgoi', c_mat, pw[:, :L], b_bar).real
    s_i = jnp.arange(L)[:, None]
    t_i = jnp.arange(L)[None, :]
    lag_f = jnp.clip(t_i - s_i, 0, L - 1)
    lag_b = jnp.clip(s_i - t_i, 0, L - 1)
    mf = jnp.where((t_i >= s_i)[:, :, None, None, None], kern[0][lag_f], 0.0)
    mb = jnp.where((s_i >= t_i)[:, :, None, None, None], kern[1][lag_b], 0.0)
    m = (mf + mb).transpose(2, 0, 4, 1, 3).reshape(G, L * c, L * c)
    ef = pw[0][L - 1 - jnp.arange(L)][:, :, :, None] * b_bar[0][None]
    eb = pw[1][jnp.arange(L)][:, :, :, None] * b_bar[1][None]
    to_rows = lambda e: e.transpose(1, 0, 3, 2).reshape(G, L * c, P)
    ef, eb = to_rows(ef), to_rows(eb)
    bs = jnp.concatenate([ef.real, ef.imag, eb.real, eb.imag, ef.imag, ef.real, eb.imag, eb.real], axis=-1)
    of = c_mat[:, None] * pw[0][1:L + 1].transpose(1, 0, 2)[:, :, None, :]
    ob = c_mat[:, None] * pw[1][L - jnp.arange(L)].transpose(1, 0, 2)[:, :, None, :]
    to_cols = lambda o: o.transpose(0, 3, 1, 2).reshape(G, P, L * c)
    of, ob = to_cols(of), to_cols(ob)
    cs = jnp.concatenate([of.real, -of.imag, ob.real, -ob.imag], axis=1)
    al = pw[:, L]
    a1 = jnp.concatenate([al[0].real, al[0].real, al[1].real, al[1].real], axis=-1)
    a2 = jnp.concatenate([-al[0].imag, al[0].imag, -al[1].imag, al[1].imag], axis=-1)
    a = jnp.stack([a1, a2, a1, -a2], axis=1)
    a = jnp.concatenate([a, jnp.zeros((G, 4, 4 * P), F32)], axis=1)
    dvec = jnp.tile(d_skip.astype(F32)[:, None, :], (1, L, 1)).reshape(G, 1, L * c)
    return m.astype(BF16), bs.astype(BF16), cs.astype(BF16), a, dvec


def _s5_kernel(x_ref, m_ref, bs_ref, cs_ref, a_ref, d_ref, o_ref, s_ref, h_ref):
    nc = x_ref.shape[0]
    x = x_ref[...]
    s_ref[...] = _dot(x, bs_ref[...])
    a1f, a1b = a_ref[0:1, 0:128], a_ref[0:1, 128:256]
    a2f, a2b = a_ref[1:2, 0:128], a_ref[1:2, 128:256]
    a2fs, a2bs = a_ref[3:4, 0:128], a_ref[3:4, 128:256]

    sub = 8
    nt = nc // sub

    def step(i, carry):
        hf, hfs, hb, hbs = carry
        rf = pl.multiple_of(i * sub, sub)
        rb = pl.multiple_of((nt - 1 - i) * sub, sub)
        sf, sfs = s_ref[pl.ds(rf, sub), 0:128], s_ref[pl.ds(rf, sub), 256:384]
        sb, sbs = s_ref[pl.ds(rb, sub), 128:256], s_ref[pl.ds(rb, sub), 384:512]
        hf_rows, hb_rows = [], [None] * sub
        for j in range(sub):
            jb = sub - 1 - j
            hf_rows.append(hf)
            hb_rows[jb] = hb
            hf, hfs = (a1f * hf + a2f * hfs + sf[j:j + 1], a1f * hfs + a2fs * hf + sfs[j:j + 1])
            hb, hbs = (a1b * hb + a2b * hbs + sb[jb:jb + 1], a1b * hbs + a2bs * hb + sbs[jb:jb + 1])
        h_ref[pl.ds(rf, sub), 0:128] = jnp.concatenate(hf_rows, axis=0)
        h_ref[pl.ds(rb, sub), 128:256] = jnp.concatenate(hb_rows, axis=0)
        return hf, hfs, hb, hbs

    z = jnp.zeros((1, 128), F32)
    lax.fori_loop(0, nt, step, (z, z, z, z))
    y = _dot(x, m_ref[...]) + _dot(h_ref[...].astype(BF16), cs_ref[...]) + x.astype(F32) * d_ref[...]
    o_ref[...] = y


def _s5(proj, params, B, T):
    m, bs, cs, a, dvec = params
    L, c, G = S5_CHUNK, S5_GROUP_CH, S5_GROUPS
    nc = T // L
    w = L * c
    u = proj[:, 10 * GROUP_W:]
    x = u.reshape(B, nc, L, G, c).transpose(0, 3, 1, 2, 4).reshape(B, G, nc, w)
    y = pl.pallas_call(
        _s5_kernel,
        grid=(B, G),
        in_specs=[
            pl.BlockSpec((None, None, nc, w), lambda b, g: (b, g, 0, 0)),
            pl.BlockSpec((None, w, w), lambda b, g: (g, 0, 0)),
            pl.BlockSpec((None, w, 512), lambda b, g: (g, 0, 0)),
            pl.BlockSpec((None, 256, w), lambda b, g: (g, 0, 0)),
            pl.BlockSpec((None, 8, 256), lambda b, g: (g, 0, 0)),
            pl.BlockSpec((None, 1, w), lambda b, g: (g, 0, 0)),
        ],
        out_specs=pl.BlockSpec((None, None, nc, w), lambda b, g: (b, g, 0, 0)),
        out_shape=jax.ShapeDtypeStruct((B, G, nc, w), F32),
        scratch_shapes=[pltpu.VMEM((nc, 512), F32), pltpu.VMEM((nc, 256), F32)],
        compiler_params=_cparams("parallel", "arbitrary"),
        name="s5",
    )(x, m, bs, cs, a, dvec)
    return y.reshape(B, G, nc, L, c).transpose(0, 2, 3, 1, 4).reshape(B * T, GROUP_W)


def _outproj_kernel(x_ref, hf_ref, hb_ref, og_ref, mg_ref, yb_ref, yc_ref, ys_ref, gw_ref, gb_ref, w_ref, o_ref):
    W = GROUP_W
    h = hf_ref[...] + hb_ref[...]
    parts = []
    for j in range(MLSTM_HEADS):
        hj = h[:, j * MLSTM_DH:(j + 1) * MLSTM_DH]
        parts.append(hj * lax.rsqrt(jnp.mean(hj * hj, axis=-1, keepdims=True) + NORM_EPS))
    hn = jnp.concatenate(parts, axis=1) * mg_ref[...]
    ya = (hn * _sigmoid(og_ref[...].astype(F32))).astype(BF16)
    y = ys_ref[...]
    z = 0.5 * y * (1.0 + jnp.tanh(math.sqrt(2.0 / math.pi) * (y + 0.044715 * (y * y * y))))
    yd = (z * _sigmoid(_dot(z.astype(BF16), gw_ref[...]) + gb_ref[...])).astype(BF16)
    acc = x_ref[...] + _dot(ya, w_ref[0:W, :])
    acc = acc + _dot(yb_ref[...], w_ref[W:2 * W, :])
    acc = acc + _dot(yc_ref[...], w_ref[2 * W:3 * W, :])
    acc = acc + _dot(yd, w_ref[3 * W:4 * W, :])
    o_ref[...] = acc


def _outproj(x, hf, hb, proj, mlstm_norm, yb, yc, ys, glu_w, glu_b, w_out, tm=512):
    M, D = x.shape
    W = GROUP_W
    tok = lambda i: (i, 0)
    const = lambda i: (0, 0)
    return pl.pallas_call(
        _outproj_kernel,
        grid=(M // tm,),
        in_specs=[
            pl.BlockSpec((tm, D), tok),
            pl.BlockSpec((tm, W), tok), pl.BlockSpec((tm, W), tok),
            pl.BlockSpec((tm, W), lambda i: (i, 3)),
            pl.BlockSpec((1, W), const),
            pl.BlockSpec((tm, W), tok), pl.BlockSpec((tm, W), tok), pl.BlockSpec((tm, W), tok),
            pl.BlockSpec((W, W), const), pl.BlockSpec((1, W), const),
            pl.BlockSpec((4 * W, D), const),
        ],
        out_specs=pl.BlockSpec((tm, D), tok),
        out_shape=jax.ShapeDtypeStruct((M, D), F32),
        compiler_params=_cparams("parallel"),
        name="outproj",
    )(x, hf, hb, proj, mlstm_norm.reshape(1, W), yb, yc, ys, glu_w, glu_b.reshape(1, W), w_out)


def _prepare_layer(w, l):
    W = GROUP_W
    ng = 4 * MLSTM_HEADS
    w_in = w['w_in'][l]
    w_main = jnp.concatenate([w_in[:, :4 * W], w_in[:, 4 * W + ng:]], axis=1).astype(BF16)
    w_gate = jnp.zeros((w_in.shape[0], LANES), BF16).at[:, :ng].set(w_in[:, 4 * W:4 * W + ng].astype(BF16))
    p = {k: w[k][l] for k in ('ffn1_norm', 'mix_norm', 'ffn2_norm', 'mlstm_conv_w', 'mlstm_conv_b', 'mlstm_gate_bias',
                              'mlstm_norm', 'diff_lambda', 'diff_norm', 's5_glu_b')}
    for k in ('ffn1_w_gate', 'ffn1_w_up', 'ffn1_w_down', 'ffn2_w_gate', 'ffn2_w_up', 'ffn2_w_down', 's5_glu_w', 'w_out'):
        p[k] = w[k][l].astype(BF16)
    p['w_main'], p['w_gate'] = w_main, w_gate
    p['na_bias'] = _na_bias_table(w['na_rpb'][l])
    p['s5'] = _s5_params(w['s5_lambda_re'][l], w['s5_lambda_im'][l], w['s5_log_step'][l], w['s5_b_re'][l],
                         w['s5_b_im'][l], w['s5_c_re'][l], w['s5_c_im'][l], w['s5_d'][l])
    return p


def _mixers(x, p, B, T, layer_idx, rope_tabs):
    proj, gates = _inproj(x, p['mix_norm'], p['w_main'], p['w_gate'])
    qc, kc = _mlstm_conv(proj, p['mlstm_conv_w'], p['mlstm_conv_b'], T)
    hf, hb = _mlstm(qc, kc, proj, gates, p['mlstm_gate_bias'], B, T)
    qr, kr = _rope(proj, rope_tabs[0], rope_tabs[1], T)
    yb = _diff_attn(qr, kr, proj, p['diff_lambda'], p['diff_norm'], B, T, layer_idx)
    yc = _na_attn(proj, p['na_bias'], B, T)
    ys = _s5(proj, p['s5'], B, T)
    return _outproj(x, hf, hb, proj, p['mlstm_norm'], yb, yc, ys, p['s5_glu_w'], p['s5_glu_b'], p['w_out'])


def _trunk(x, layers, final_norm):
    B, T, D = x.shape
    x = x.reshape(B * T, D)
    rope_tabs = _rope_tables(T)
    for l, p in enumerate(layers):
        x = _ffn(x, p['ffn1_norm'], p['ffn1_w_gate'], p['ffn1_w_up'], p['ffn1_w_down'])
        x = _mixers(x, p, B, T, l, rope_tabs)
        last = l == len(layers) - 1
        x = _ffn(x, p['ffn2_norm'], p['ffn2_w_gate'], p['ffn2_w_up'], p['ffn2_w_down'],
                 final_g=final_norm if last else None)
    return x.reshape(B, T, D)


def kernel(x_prompt, x_sample, ffn1_norm, ffn1_w_gate, ffn1_w_up, ffn1_w_down, mix_norm, w_in, mlstm_conv_w, mlstm_conv_b, mlstm_gate_bias, mlstm_norm, diff_lambda, diff_norm, na_rpb, s5_lambda_re, s5_lambda_im, s5_log_step, s5_b_re, s5_b_im, s5_c_re, s5_c_im, s5_d, s5_glu_w, s5_glu_b, w_out, ffn2_norm, ffn2_w_gate, ffn2_w_up, ffn2_w_down, final_norm):
    w = dict(ffn1_norm=ffn1_norm, ffn1_w_gate=ffn1_w_gate, ffn1_w_up=ffn1_w_up, ffn1_w_down=ffn1_w_down,
             mix_norm=mix_norm, w_in=w_in, mlstm_conv_w=mlstm_conv_w, mlstm_conv_b=mlstm_conv_b,
             mlstm_gate_bias=mlstm_gate_bias, mlstm_norm=mlstm_norm, diff_lambda=diff_lambda, diff_norm=diff_norm,
             na_rpb=na_rpb, s5_lambda_re=s5_lambda_re, s5_lambda_im=s5_lambda_im, s5_log_step=s5_log_step,
             s5_b_re=s5_b_re, s5_b_im=s5_b_im, s5_c_re=s5_c_re, s5_c_im=s5_c_im, s5_d=s5_d,
             s5_glu_w=s5_glu_w, s5_glu_b=s5_glu_b, w_out=w_out, ffn2_norm=ffn2_norm, ffn2_w_gate=ffn2_w_gate,
             ffn2_w_up=ffn2_w_up, ffn2_w_down=ffn2_w_down)
    layers = [_prepare_layer(w, l) for l in range(ffn1_norm.shape[0])]
    return (_trunk(x_prompt, layers, final_norm), _trunk(x_sample, layers, final_norm))
```

```python
import functools
import math

import jax
import jax.numpy as jnp
from jax import lax
from jax.experimental import pallas as pl
from jax.experimental.pallas import tpu as pltpu

F32 = jnp.float32
BF16 = jnp.bfloat16

NORM_EPS = 1e-6
GROUP_W = 512
GRID_W = 64
MLSTM_HEADS = 4
MLSTM_DH = 128
MLSTM_CONV = 5
MLSTM_CHUNK = 128
DIFF_HEADS = 4
DIFF_DH = 64
ROPE_THETA = 10000.0
NA_HEADS = 8
NA_DH = 64
NA_WIN_ROWS = 8
NA_WIN_W = 16
NA_ROWS_PER_STEP = 8
S5_GROUP_CH = 16
S5_GROUPS = 32
S5_STATE = 64
S5_CHUNK = 32
LANES = 128
BF16_SUBLANES = 16
MASK_VALUE = -1e30
VMEM_LIMIT = 56 * 1024 * 1024


def _cparams(*sem):
    return pltpu.CompilerParams(dimension_semantics=sem, vmem_limit_bytes=VMEM_LIMIT)


def _rms(x, g):
    return x * lax.rsqrt(jnp.mean(x * x, axis=-1, keepdims=True) + NORM_EPS) * g


def _sigmoid(x):
    return 1.0 / (1.0 + jnp.exp(-x))


def _dot(a, b):
    return jnp.dot(a, b, preferred_element_type=F32)


def _dot_nt(a, b):
    return lax.dot_general(a, b, (((1,), (1,)), ((), ())), preferred_element_type=F32)


def _dot_tn(a, b):
    return lax.dot_general(a, b, (((0,), (0,)), ((), ())), preferred_element_type=F32)


def _split3(x):
    hi = x.astype(BF16)
    r1 = x - hi.astype(F32)
    mid = r1.astype(BF16)
    lo = (r1 - mid.astype(F32)).astype(BF16)
    return hi, mid, lo


def _ffn_kernel(x_ref, g_ref, wg_ref, wu_ref, wd_ref, *rest, final):
    if final:
        fg_ref, o_ref, h_ref = rest
    else:
        o_ref, h_ref = rest
    f = pl.program_id(1)
    nf = pl.num_programs(1)

    @pl.when(f == 0)
    def _():
        h_ref[...] = _rms(x_ref[...], g_ref[...]).astype(BF16)
        o_ref[...] = x_ref[...]

    h = h_ref[...]
    gate = _dot(h, wg_ref[...])
    up = _dot(h, wu_ref[...])
    act = (0.5 * gate * _sigmoid(gate) * up).astype(BF16)
    nchunk = 4
    cw = o_ref.shape[1] // nchunk
    for c in range(nchunk):
        o_ref[:, c * cw:(c + 1) * cw] += _dot(act, wd_ref[:, c * cw:(c + 1) * cw])

    if final:
        @pl.when(f == nf - 1)
        def _():
            o_ref[...] = _rms(o_ref[...], fg_ref[...])


def _ffn(x, g, wg, wu, wd, final_g=None, tm=1024, tf=512):
    M, D = x.shape
    F = wg.shape[1]
    final = final_g is not None
    in_specs = [
        pl.BlockSpec((tm, D), lambda i, f: (i, 0), pipeline_mode=pl.Buffered(1)),
        pl.BlockSpec((1, D), lambda i, f: (0, 0)),
        pl.BlockSpec((D, tf), lambda i, f: (0, f)),
        pl.BlockSpec((D, tf), lambda i, f: (0, f)),
        pl.BlockSpec((tf, D), lambda i, f: (f, 0)),
    ]
    args = [x, g.reshape(1, D), wg, wu, wd]
    if final:
        in_specs.append(pl.BlockSpec((1, D), lambda i, f: (0, 0)))
        args.append(final_g.reshape(1, D))
    return pl.pallas_call(
        functools.partial(_ffn_kernel, final=final),
        grid=(M // tm, F // tf),
        in_specs=in_specs,
        out_specs=pl.BlockSpec((tm, D), lambda i, f: (i, 0)),
        out_shape=jax.ShapeDtypeStruct((M, D), F32),
        scratch_shapes=[pltpu.VMEM((tm, D), BF16)],
        compiler_params=_cparams("parallel", "arbitrary"),
        name="ffn",
    )(*args)


def _inproj_kernel(x_ref, g_ref, w_ref, wgate_ref, o_ref, og_ref, h_ref):
    n = pl.program_id(1)

    @pl.when(n == 0)
    def _():
        h = _rms(x_ref[...], g_ref[...]).astype(BF16)
        h_ref[...] = h
        og_ref[...] = _dot(h, wgate_ref[...])

    o_ref[...] = _dot(h_ref[...], w_ref[...]).astype(BF16)


def _inproj(x, g, w_main, w_gate, tm=1024, tn=1408):
    M, D = x.shape
    N = w_main.shape[1]
    return pl.pallas_call(
        _inproj_kernel,
        grid=(M // tm, N // tn),
        in_specs=[
            pl.BlockSpec((tm, D), lambda i, n: (i, 0)),
            pl.BlockSpec((1, D), lambda i, n: (0, 0)),
            pl.BlockSpec((D, tn), lambda i, n: (0, n)),
            pl.BlockSpec((D, LANES), lambda i, n: (0, 0)),
        ],
        out_specs=[
            pl.BlockSpec((tm, tn), lambda i, n: (i, n)),
            pl.BlockSpec((tm, LANES), lambda i, n: (i, 0)),
        ],
        out_shape=[
            jax.ShapeDtypeStruct((M, N), BF16),
            jax.ShapeDtypeStruct((M, LANES), F32),
        ],
        scratch_shapes=[pltpu.VMEM((tm, D), BF16)],
        compiler_params=_cparams("parallel", "arbitrary"),
        name="inproj",
    )(x, g.reshape(1, D), w_main, w_gate)


def _conv_kernel(x_ref, prev_ref, next_ref, w_ref, b_ref, q_ref, k_ref, *, blocks_per_seq):
    tb = x_ref.shape[0]
    halo = prev_ref.shape[0]
    pos = pl.program_id(0) % blocks_per_seq
    prev = jnp.where(pos != 0, prev_ref[...].astype(F32), 0.0)
    nxt = jnp.where(pos != blocks_per_seq - 1, next_ref[...].astype(F32), 0.0)
    ext = jnp.concatenate([prev, x_ref[...].astype(F32), nxt], axis=0)
    rows = tb + 2 * halo
    pad = MLSTM_CONV // 2
    acc = None
    for j in range(MLSTM_CONV):
        shift = (pad - j) % rows
        sh = ext if shift == 0 else pltpu.roll(ext, shift, axis=0)
        term = sh[halo:halo + tb] * w_ref[j:j + 1, :]
        acc = term if acc is None else acc + term
    acc = acc + b_ref[...]
    y = acc * _sigmoid(acc)
    q_ref[...] = y[:, :GROUP_W].astype(BF16)
    k_ref[...] = (y[:, GROUP_W:] * (MLSTM_DH ** -0.5)).astype(BF16)


def _mlstm_conv(proj, conv_w, conv_b, T, tb=256):
    M = proj.shape[0]
    C = 2 * GROUP_W
    halo = BF16_SUBLANES
    r = tb // halo
    nhalo = M // halo
    w = jnp.zeros((8, C), F32).at[:MLSTM_CONV].set(conv_w)
    return pl.pallas_call(
        functools.partial(_conv_kernel, blocks_per_seq=T // tb),
        grid=(M // tb,),
        in_specs=[
            pl.BlockSpec((tb, C), lambda i: (i, 0)),
            pl.BlockSpec((halo, C), lambda i: (jnp.maximum(i * r - 1, 0), 0)),
            pl.BlockSpec((halo, C), lambda i: (jnp.minimum((i + 1) * r, nhalo - 1), 0)),
            pl.BlockSpec((8, C), lambda i: (0, 0)),
            pl.BlockSpec((1, C), lambda i: (0, 0)),
        ],
        out_specs=[
            pl.BlockSpec((tb, GROUP_W), lambda i: (i, 0)),
            pl.BlockSpec((tb, GROUP_W), lambda i: (i, 0)),
        ],
        out_shape=[jax.ShapeDtypeStruct((M, GROUP_W), BF16)] * 2,
        compiler_params=_cparams("parallel"),
        name="mlstm_conv",
    )(proj, proj, proj, w, conv_b.reshape(1, C))


def _mlstm_kernel(qf_ref, kf_ref, vf_ref, gf_ref, qb_ref, kb_ref, vb_ref, gb_ref, bias_ref,
                  of_ref, ob_ref, c_ref, m_ref):
    L = qf_ref.shape[0]
    H, dh = MLSTM_HEADS, MLSTM_DH

    @pl.when(pl.program_id(1) == 0)
    def _():
        c_ref[...] = jnp.zeros_like(c_ref)
        m_ref[...] = jnp.zeros_like(m_ref)

    row = lax.broadcasted_iota(jnp.int32, (L, L), 0)
    col = lax.broadcasted_iota(jnp.int32, (L, L), 1)
    lane = lax.broadcasted_iota(jnp.int32, (L, dh), 1)
    ones_col = jnp.where(lane == 0, 1.0, 0.0).astype(BF16)

    for d, (q_ref, k_ref, v_ref, g_ref, o_ref) in enumerate(
            ((qf_ref, kf_ref, vf_ref, gf_ref, of_ref), (qb_ref, kb_ref, vb_ref, gb_ref, ob_ref))):
        seen = (col <= row) if d == 0 else (col >= row)
        tri = jnp.where(seen, 1.0, 0.0).astype(BF16)
        gt = g_ref[...] + bias_ref[...]
        lf = jnp.minimum(gt, 0.0) - jnp.log(1.0 + jnp.exp(-jnp.abs(gt)))
        hi, mid, lo = _split3(lf)
        bcum = _dot(tri, hi) + _dot(tri, mid) + _dot(tri, lo)
        bcum_t = bcum.T
        gt_t = gt.T
        end = L - 1 if d == 0 else 0
        for h in range(H):
            ic = 8 * d + h
            fc = 8 * d + 4 + h
            idx = d * H + h
            b_col = bcum[:, fc:fc + 1]
            b_row = bcum_t[fc:fc + 1, :]
            i_col = gt[:, ic:ic + 1]
            i_row = gt_t[ic:ic + 1, :]
            g_tot = bcum[end:end + 1, fc:fc + 1]
            m_prev = m_ref[idx][:, :1]
            c_prev = c_ref[idx]
            q = q_ref[:, h * dh:(h + 1) * dh]
            k = k_ref[:, h * dh:(h + 1) * dh]
            v_aug = jnp.concatenate([v_ref[:, h * dh:(h + 1) * dh], ones_col], axis=1)

            dlog = jnp.where(seen, b_col - b_row + i_row, -jnp.inf)
            inter = b_col + m_prev
            m_t = jnp.maximum(inter, jnp.max(dlog, axis=1, keepdims=True))
            s = (_dot_nt(q, k) * jnp.exp(dlog - m_t)).astype(BF16)
            s_inter = jnp.exp(inter - m_t)
            num_aug = s_inter * _dot(q, c_prev.astype(BF16)) + _dot(s, v_aug)
            den = jnp.maximum(jnp.abs(num_aug[:, dh:dh + 1]), jnp.exp(-m_t))
            o_ref[:, h * dh:(h + 1) * dh] = num_aug[:, :dh] / den

            a_col = g_tot - b_col + i_col
            m_new = jnp.maximum(g_tot + m_prev, jnp.max(a_col, axis=0, keepdims=True))
            decay = jnp.exp(g_tot + m_prev - m_new)
            kw = (k.astype(F32) * jnp.exp(a_col - m_new)).astype(BF16)
            c_ref[idx] = decay * c_prev + _dot_tn(kw, v_aug)
            m_ref[idx] = jnp.broadcast_to(m_new, (1, LANES))


def _mlstm(qc, kc, proj, gates, gate_bias, B, T):
    M = qc.shape[0]
    L = MLSTM_CHUNK
    nc = T // L
    W = GROUP_W
    vblk = 2
    fwd = lambda b, i: (b * nc + i, 0)
    bwd = lambda b, i: (b * nc + nc - 1 - i, 0)
    fwd_v = lambda b, i: (b * nc + i, vblk)
    bwd_v = lambda b, i: (b * nc + nc - 1 - i, vblk)
    bias = jnp.zeros((1, LANES), F32).at[0, :4 * MLSTM_HEADS].set(gate_bias)
    return pl.pallas_call(
        _mlstm_kernel,
        grid=(B, nc),
        in_specs=[
            pl.BlockSpec((L, W), fwd), pl.BlockSpec((L, W), fwd), pl.BlockSpec((L, W), fwd_v),
            pl.BlockSpec((L, LANES), fwd),
            pl.BlockSpec((L, W), bwd), pl.BlockSpec((L, W), bwd), pl.BlockSpec((L, W), bwd_v),
            pl.BlockSpec((L, LANES), bwd),
            pl.BlockSpec((1, LANES), lambda b, i: (0, 0)),
        ],
        out_specs=[pl.BlockSpec((L, W), fwd), pl.BlockSpec((L, W), bwd)],
        out_shape=[jax.ShapeDtypeStruct((M, W), F32)] * 2,
        scratch_shapes=[
            pltpu.VMEM((2 * MLSTM_HEADS, MLSTM_DH, 2 * MLSTM_DH), F32),
            pltpu.VMEM((2 * MLSTM_HEADS, 1, LANES), F32),
        ],
        compiler_params=_cparams("parallel", "arbitrary"),
        name="mlstm",
    )(qc, kc, proj, gates, qc, kc, proj, gates, bias)


def _rope_kernel(x_ref, cos_ref, sin_ref, q_ref, k_ref):
    x = x_ref[...].astype(F32)
    reps = x.shape[1] // LANES
    cos = jnp.concatenate([cos_ref[...]] * reps, axis=1)
    sin = jnp.concatenate([sin_ref[...]] * reps, axis=1)
    lane = lax.broadcasted_iota(jnp.int32, x.shape, 1)
    width = x.shape[1]
    half = DIFF_DH // 2
    rot = jnp.where(lane % DIFF_DH < half, pltpu.roll(x, width - half, axis=1), pltpu.roll(x, half, axis=1))
    y = x * cos + rot * sin
    q_ref[...] = (y[:, :GROUP_W] * (DIFF_DH ** -0.5 * math.log2(math.e))).astype(BF16)
    k_ref[...] = y[:, GROUP_W:].astype(BF16)


def _rope_tables(T):
    half = DIFF_DH // 2
    inv = 1.0 / (ROPE_THETA ** (jnp.arange(0, DIFF_DH, 2, dtype=F32) / DIFF_DH))
    ang = jnp.arange(T, dtype=F32)[:, None] * inv[None, :]
    cos, sin = jnp.cos(ang), jnp.sin(ang)
    cos_t = jnp.concatenate([cos, cos] * (LANES // DIFF_DH), axis=1)
    sin_t = jnp.concatenate([-sin, sin] * (LANES // DIFF_DH), axis=1)
    return cos_t, sin_t


def _rope(proj, cos_t, sin_t, T, tb=256):
    M = proj.shape[0]
    nb = T // tb
    return pl.pallas_call(
        _rope_kernel,
        grid=(M // tb,),
        in_specs=[
            pl.BlockSpec((tb, 2 * GROUP_W), lambda i: (i, 2)),
            pl.BlockSpec((tb, LANES), lambda i: (i % nb, 0)),
            pl.BlockSpec((tb, LANES), lambda i: (i % nb, 0)),
        ],
        out_specs=[pl.BlockSpec((tb, GROUP_W), lambda i: (i, 0))] * 2,
        out_shape=[jax.ShapeDtypeStruct((M, GROUP_W), BF16)] * 2,
        compiler_params=_cparams("parallel"),
        name="rope",
    )(proj, cos_t, sin_t)


def _diff_kernel(q_ref, k_ref, v_ref, lp_ref, g_ref, o_ref, q2_ref, m_ref, l_ref, acc_ref, *, lam_init, tks):
    ki = pl.program_id(3)
    tq = q_ref.shape[0]

    @pl.when(ki == 0)
    def _():
        q = q_ref[...]
        lane = lax.broadcasted_iota(jnp.int32, q.shape, 1)
        zero = jnp.zeros_like(q)
        q2_ref[0:tq, :] = jnp.where(lane < DIFF_DH, q, zero)
        q2_ref[tq:2 * tq, :] = jnp.where(lane < DIFF_DH, zero, q)
        m_ref[...] = jnp.full_like(m_ref, -jnp.inf)
        l_ref[...] = jnp.zeros_like(l_ref)
        acc_ref[...] = jnp.zeros_like(acc_ref)

    q2 = q2_ref[...]
    for j in range(k_ref.shape[0] // tks):
        s = _dot_nt(q2, k_ref[j * tks:(j + 1) * tks, :])
        tiles = [s[:, c * LANES:(c + 1) * LANES] for c in range(tks // LANES)]
        m_prev = m_ref[...]
        m_new = jnp.maximum(m_prev, jnp.max(functools.reduce(jnp.maximum, tiles), axis=1, keepdims=True))
        alpha = jnp.exp2(m_prev - m_new)
        ps = [jnp.exp2(t - m_new) for t in tiles]
        l_ref[...] = alpha * l_ref[...] + functools.reduce(jnp.add, ps)
        p = jnp.concatenate([t.astype(BF16) for t in ps], axis=1)
        acc_ref[...] = alpha * acc_ref[...] + _dot(p, v_ref[j * tks:(j + 1) * tks, :])
        m_ref[...] = m_new

    @pl.when(ki == pl.num_programs(3) - 1)
    def _():
        lp = lp_ref[...]
        lam = (jnp.exp(jnp.sum(lp[0:1] * lp[1:2], axis=1, keepdims=True))
               - jnp.exp(jnp.sum(lp[2:3] * lp[3:4], axis=1, keepdims=True)) + lam_init)
        o = acc_ref[...] / jnp.sum(l_ref[...], axis=1, keepdims=True)
        out = o[0:tq] - lam * o[tq:2 * tq]
        o_ref[...] = (_rms(out, g_ref[...]) * (1.0 - lam_init)).astype(BF16)


def _diff_attn(qr, kr, proj, lam_params, norm_g, B, T, layer_idx, tq=512, tk=2048, tks=512):
    M = qr.shape[0]
    dv = 2 * DIFF_DH
    tk = min(tk, T)
    nq, nk = T // tq, T // tk
    vblk0 = (6 * GROUP_W) // dv
    lam_init = 0.8 - 0.6 * math.exp(-0.3 * layer_idx)
    return pl.pallas_call(
        functools.partial(_diff_kernel, lam_init=lam_init, tks=tks),
        grid=(B, DIFF_HEADS, nq, nk),
        in_specs=[
            pl.BlockSpec((tq, dv), lambda b, h, qi, ki: (b * nq + qi, h)),
            pl.BlockSpec((tk, dv), lambda b, h, qi, ki: (b * nk + ki, h)),
            pl.BlockSpec((tk, dv), lambda b, h, qi, ki: (b * nk + ki, vblk0 + h)),
            pl.BlockSpec((4, DIFF_DH), lambda b, h, qi, ki: (0, 0)),
            pl.BlockSpec((1, dv), lambda b, h, qi, ki: (0, 0)),
        ],
        out_specs=pl.BlockSpec((tq, dv), lambda b, h, qi, ki: (b * nq + qi, h)),
        out_shape=jax.ShapeDtypeStruct((M, GROUP_W), BF16),
        scratch_shapes=[
            pltpu.VMEM((2 * tq, dv), BF16),
            pltpu.VMEM((2 * tq, LANES), F32),
            pltpu.VMEM((2 * tq, LANES), F32),
            pltpu.VMEM((2 * tq, dv), F32),
        ],
        compiler_params=_cparams("parallel", "parallel", "parallel", "arbitrary"),
        name="diff_attn",
    )(qr, kr, proj, lam_params, norm_g.reshape(1, dv))


def _na_kernel(q_ref, k_ref, v_ref, bias_ref, o_ref, *, rows):
    g = pl.program_id(2)
    win = NA_WIN_ROWS * GRID_W
    lane = lax.broadcasted_iota(jnp.int32, (GRID_W, 2 * NA_DH), 1)
    first = lane < NA_DH
    for jr in range(NA_ROWS_PER_STEP):
        r = g * NA_ROWS_PER_STEP + jr
        rs = jnp.clip(r - NA_WIN_ROWS // 2, 0, rows - NA_WIN_ROWS)
        start = pl.multiple_of(rs * GRID_W, GRID_W)
        q = q_ref[jr * GRID_W:(jr + 1) * GRID_W, :]
        zero = jnp.zeros_like(q)
        q2 = jnp.concatenate([jnp.where(first, q, zero), jnp.where(first, zero, q)], axis=0)
        kw = k_ref[pl.ds(start, win), :]
        vw = v_ref[pl.ds(start, win), :]
        s = _dot_nt(q2, kw) * (NA_DH ** -0.5) + bias_ref[r - rs]
        e = jnp.exp(s - jnp.max(s, axis=1, keepdims=True))
        pv = _dot(e.astype(BF16), vw) / jnp.sum(e, axis=1, keepdims=True)
        o_ref[jr * GRID_W:(jr + 1) * GRID_W, :] = jnp.where(first, pv[:GRID_W], pv[GRID_W:]).astype(BF16)


def _na_bias_table(rpb):
    c = jnp.arange(GRID_W)
    col_start = jnp.clip(c - NA_WIN_W // 2, 0, GRID_W - NA_WIN_W)
    kc = jnp.arange(GRID_W)
    valid = (kc[None, :] >= col_start[:, None]) & (kc[None, :] < col_start[:, None] + NA_WIN_W)
    col_off = jnp.clip(kc[None, :] - c[:, None] + NA_WIN_W - 1, 0, 2 * NA_WIN_W - 2)
    p = jnp.arange(NA_WIN_ROWS)
    k = jnp.arange(NA_WIN_ROWS)
    row_off = k[None, :] - p[:, None] + NA_WIN_ROWS - 1
    t = rpb.astype(F32)[:, row_off][:, :, :, col_off]
    t = jnp.where(valid[None, None, None], t, MASK_VALUE)
    t = t.transpose(0, 1, 3, 2, 4).reshape(NA_HEADS, NA_WIN_ROWS, GRID_W, NA_WIN_ROWS * GRID_W)
    t = t.reshape(NA_HEADS // 2, 2, NA_WIN_ROWS, GRID_W, NA_WIN_ROWS * GRID_W).transpose(0, 2, 1, 3, 4)
    return t.reshape(NA_HEADS // 2, NA_WIN_ROWS, 2 * GRID_W, NA_WIN_ROWS * GRID_W)


def _na_attn(proj, bias_table, B, T):
    M = proj.shape[0]
    rows = T // GRID_W
    tq = NA_ROWS_PER_STEP * GRID_W
    ng = rows // NA_ROWS_PER_STEP
    hp = NA_HEADS // 2
    w = 2 * NA_DH
    qblk0, kblk0, vblk0 = (7 * GROUP_W) // w, (8 * GROUP_W) // w, (9 * GROUP_W) // w
    return pl.pallas_call(
        functools.partial(_na_kernel, rows=rows),
        grid=(B, hp, ng),
        in_specs=[
            pl.BlockSpec((tq, w), lambda b, h, g: (b * ng + g, qblk0 + h)),
            pl.BlockSpec((T, w), lambda b, h, g: (b, kblk0 + h)),
            pl.BlockSpec((T, w), lambda b, h, g: (b, vblk0 + h)),
            pl.BlockSpec((None, NA_WIN_ROWS, 2 * GRID_W, NA_WIN_ROWS * GRID_W), lambda b, h, g: (h, 0, 0, 0)),
        ],
        out_specs=pl.BlockSpec((tq, w), lambda b, h, g: (b * ng + g, h)),
        out_shape=jax.ShapeDtypeStruct((M, GROUP_W), BF16),
        compiler_params=_cparams("parallel", "parallel", "arbitrary"),
        name="na_attn",
    )(proj, proj, proj, bias_table)


def _s5_params(lam_re, lam_im, log_step, b_re, b_im, c_re, c_im, d_skip):
    L, c, P, G = S5_CHUNK, S5_GROUP_CH, S5_STATE, S5_GROUPS
    lam = lax.complex(lam_re.astype(F32), lam_im.astype(F32))
    step = jnp.exp(log_step.astype(F32))[..., None]
    lam_bar = jnp.exp(lam * step)
    b_mat = lax.complex(b_re.astype(F32), b_im.astype(F32))
    c_mat = lax.complex(c_re.astype(F32), c_im.astype(F32))
    b_bar = ((lam_bar - 1.0) / lam)[..., None] * b_mat[None]
    pw = lam_bar[:, None] ** jnp.arange(L + 1, dtype=F32)[None, :, None, None]
    kern = jnp.einsum('gop,dlgp,dgpi->dlgoi', c_mat, pw[:, :L], b_bar).real
    s_i = jnp.arange(L)[:, None]
    t_i = jnp.arange(L)[None, :]
    lag_f = jnp.clip(t_i - s_i, 0, L - 1)
    lag_b = jnp.clip(s_i - t_i, 0, L - 1)
    mf = jnp.where((t_i >= s_i)[:, :, None, None, None], kern[0][lag_f], 0.0)
    mb = jnp.where((s_i >= t_i)[:, :, None, None, None], kern[1][lag_b], 0.0)
    m = (mf + mb).transpose(2, 0, 4, 1, 3).reshape(G, L * c, L * c)
    ef = pw[0][L - 1 - jnp.arange(L)][:, :, :, None] * b_bar[0][None]
    eb = pw[1][jnp.arange(L)][:, :, :, None] * b_bar[1][None]
    to_rows = lambda e: e.transpose(1, 0, 3, 2).reshape(G, L * c, P)
    ef, eb = to_rows(ef), to_rows(eb)
    bs = jnp.concatenate([ef.real, ef.imag, eb.real, eb.imag, ef.imag, ef.real, eb.imag, eb.real], axis=-1)
    of = c_mat[:, None] * pw[0][1:L + 1].transpose(1, 0, 2)[:, :, None, :]
    ob = c_mat[:, None] * pw[1][L - jnp.arange(L)].transpose(1, 0, 2)[:, :, None, :]
    to_cols = lambda o: o.transpose(0, 3, 1, 2).reshape(G, P, L * c)
    of, ob = to_cols(of), to_cols(ob)
    cs = jnp.concatenate([of.real, -of.imag, ob.real, -ob.imag], axis=1)
    al = pw[:, L]
    a1 = jnp.concatenate([al[0].real, al[0].real, al[1].real, al[1].real], axis=-1)
    a2 = jnp.concatenate([-al[0].imag, al[0].imag, -al[1].imag, al[1].imag], axis=-1)
    a = jnp.stack([a1, a2, a1, -a2], axis=1)
    a = jnp.concatenate([a, jnp.zeros((G, 4, 4 * P), F32)], axis=1)
    dvec = jnp.tile(d_skip.astype(F32)[:, None, :], (1, L, 1)).reshape(G, 1, L * c)
    return m.astype(BF16), bs.astype(BF16), cs.astype(BF16), a, dvec


def _s5_kernel(x_ref, m_ref, bs_ref, cs_ref, a_ref, d_ref, o_ref, s_ref, h_ref):
    nc = x_ref.shape[0]
    x = x_ref[...]
    s_ref[...] = _dot(x, bs_ref[...])
    a1f, a1b = a_ref[0:1, 0:128], a_ref[0:1, 128:256]
    a2f, a2b = a_ref[1:2, 0:128], a_ref[1:2, 128:256]
    a2fs, a2bs = a_ref[3:4, 0:128], a_ref[3:4, 128:256]

    sub = 8
    nt = nc // sub

    def step(i, carry):
        hf, hfs, hb, hbs = carry
        rf = pl.multiple_of(i * sub, sub)
        rb = pl.multiple_of((nt - 1 - i) * sub, sub)
        sf, sfs = s_ref[pl.ds(rf, sub), 0:128], s_ref[pl.ds(rf, sub), 256:384]
        sb, sbs = s_ref[pl.ds(rb, sub), 128:256], s_ref[pl.ds(rb, sub), 384:512]
        hf_rows, hb_rows = [], [None] * sub
        for j in range(sub):
            jb = sub - 1 - j
            hf_rows.append(hf)
            hb_rows[jb] = hb
            hf, hfs = (a1f * hf + a2f * hfs + sf[j:j + 1], a1f * hfs + a2fs * hf + sfs[j:j + 1])
            hb, hbs = (a1b * hb + a2b * hbs + sb[jb:jb + 1], a1b * hbs + a2bs * hb + sbs[jb:jb + 1])
        h_ref[pl.ds(rf, sub), 0:128] = jnp.concatenate(hf_rows, axis=0)
        h_ref[pl.ds(rb, sub), 128:256] = jnp.concatenate(hb_rows, axis=0)
        return hf, hfs, hb, hbs

    z = jnp.zeros((1, 128), F32)
    lax.fori_loop(0, nt, step, (z, z, z, z))
    y = _dot(x, m_ref[...]) + _dot(h_ref[...].astype(BF16), cs_ref[...]) + x.astype(F32) * d_ref[...]
    o_ref[...] = y


def _s5(proj, params, B, T):
    m, bs, cs, a, dvec = params
    L, c, G = S5_CHUNK, S5_GROUP_CH, S5_GROUPS
    nc = T // L
    w = L * c
    u = proj[:, 10 * GROUP_W:]
    x = u.reshape(B, nc, L, G, c).transpose(0, 3, 1, 2, 4).reshape(B, G, nc, w)
    y = pl.pallas_call(
        _s5_kernel,
        grid=(B, G),
        in_specs=[
            pl.BlockSpec((None, None, nc, w), lambda b, g: (b, g, 0, 0)),
            pl.BlockSpec((None, w, w), lambda b, g: (g, 0, 0)),
            pl.BlockSpec((None, w, 512), lambda b, g: (g, 0, 0)),
            pl.BlockSpec((None, 256, w), lambda b, g: (g, 0, 0)),
            pl.BlockSpec((None, 8, 256), lambda b, g: (g, 0, 0)),
            pl.BlockSpec((None, 1, w), lambda b, g: (g, 0, 0)),
        ],
        out_specs=pl.BlockSpec((None, None, nc, w), lambda b, g: (b, g, 0, 0)),
        out_shape=jax.ShapeDtypeStruct((B, G, nc, w), F32),
        scratch_shapes=[pltpu.VMEM((nc, 512), F32), pltpu.VMEM((nc, 256), F32)],
        compiler_params=_cparams("parallel", "arbitrary"),
        name="s5",
    )(x, m, bs, cs, a, dvec)
    return y.reshape(B, G, nc, L, c).transpose(0, 2, 3, 1, 4).reshape(B * T, GROUP_W)


def _outproj_kernel(x_ref, hf_ref, hb_ref, og_ref, mg_ref, yb_ref, yc_ref, ys_ref, gw_ref, gb_ref, w_ref, o_ref):
    W = GROUP_W
    h = hf_ref[...] + hb_ref[...]
    parts = []
    for j in range(MLSTM_HEADS):
        hj = h[:, j * MLSTM_DH:(j + 1) * MLSTM_DH]
        parts.append(hj * lax.rsqrt(jnp.mean(hj * hj, axis=-1, keepdims=True) + NORM_EPS))
    hn = jnp.concatenate(parts, axis=1) * mg_ref[...]
    ya = (hn * _sigmoid(og_ref[...].astype(F32))).astype(BF16)
    y = ys_ref[...]
    z = 0.5 * y * (1.0 + jnp.tanh(math.sqrt(2.0 / math.pi) * (y + 0.044715 * (y * y * y))))
    yd = (z * _sigmoid(_dot(z.astype(BF16), gw_ref[...]) + gb_ref[...])).astype(BF16)
    acc = x_ref[...] + _dot(ya, w_ref[0:W, :])
    acc = acc + _dot(yb_ref[...], w_ref[W:2 * W, :])
    acc = acc + _dot(yc_ref[...], w_ref[2 * W:3 * W, :])
    acc = acc + _dot(yd, w_ref[3 * W:4 * W, :])
    o_ref[...] = acc


def _outproj(x, hf, hb, proj, mlstm_norm, yb, yc, ys, glu_w, glu_b, w_out, tm=512):
    M, D = x.shape
    W = GROUP_W
    tok = lambda i: (i, 0)
    const = lambda i: (0, 0)
    return pl.pallas_call(
        _outproj_kernel,
        grid=(M // tm,),
        in_specs=[
            pl.BlockSpec((tm, D), tok),
            pl.BlockSpec((tm, W), tok), pl.BlockSpec((tm, W), tok),
            pl.BlockSpec((tm, W), lambda i: (i, 3)),
            pl.BlockSpec((1, W), const),
            pl.BlockSpec((tm, W), tok), pl.BlockSpec((tm, W), tok), pl.BlockSpec((tm, W), tok),
            pl.BlockSpec((W, W), const), pl.BlockSpec((1, W), const),
            pl.BlockSpec((4 * W, D), const),
        ],
        out_specs=pl.BlockSpec((tm, D), tok),
        out_shape=jax.ShapeDtypeStruct((M, D), F32),
        compiler_params=_cparams("parallel"),
        name="outproj",
    )(x, hf, hb, proj, mlstm_norm.reshape(1, W), yb, yc, ys, glu_w, glu_b.reshape(1, W), w_out)


def _prepare_layer(w, l):
    W = GROUP_W
    ng = 4 * MLSTM_HEADS
    w_in = w['w_in'][l]
    w_main = jnp.concatenate([w_in[:, :4 * W], w_in[:, 4 * W + ng:]], axis=1).astype(BF16)
    w_gate = jnp.zeros((w_in.shape[0], LANES), BF16).at[:, :ng].set(w_in[:, 4 * W:4 * W + ng].astype(BF16))
    p = {k: w[k][l] for k in ('ffn1_norm', 'mix_norm', 'ffn2_norm', 'mlstm_conv_w', 'mlstm_conv_b', 'mlstm_gate_bias',
                              'mlstm_norm', 'diff_lambda', 'diff_norm', 's5_glu_b')}
    for k in ('ffn1_w_gate', 'ffn1_w_up', 'ffn1_w_down', 'ffn2_w_gate', 'ffn2_w_up', 'ffn2_w_down', 's5_glu_w', 'w_out'):
        p[k] = w[k][l].astype(BF16)
    p['w_main'], p['w_gate'] = w_main, w_gate
    p['na_bias'] = _na_bias_table(w['na_rpb'][l])
    p['s5'] = _s5_params(w['s5_lambda_re'][l], w['s5_lambda_im'][l], w['s5_log_step'][l], w['s5_b_re'][l],
                         w['s5_b_im'][l], w['s5_c_re'][l], w['s5_c_im'][l], w['s5_d'][l])
    return p


def _mixers(x, p, B, T, layer_idx, rope_tabs):
    proj, gates = _inproj(x, p['mix_norm'], p['w_main'], p['w_gate'])
    qc, kc = _mlstm_conv(proj, p['mlstm_conv_w'], p['mlstm_conv_b'], T)
    hf, hb = _mlstm(qc, kc, proj, gates, p['mlstm_gate_bias'], B, T)
    qr, kr = _rope(proj, rope_tabs[0], rope_tabs[1], T)
    yb = _diff_attn(qr, kr, proj, p['diff_lambda'], p['diff_norm'], B, T, layer_idx)
    yc = _na_attn(proj, p['na_bias'], B, T)
    ys = _s5(proj, p['s5'], B, T)
    return _outproj(x, hf, hb, proj, p['mlstm_norm'], yb, yc, ys, p['s5_glu_w'], p['s5_glu_b'], p['w_out'])


def _trunk(x, layers, final_norm):
    B, T, D = x.shape
    x = x.reshape(B * T, D)
    rope_tabs = _rope_tables(T)
    for l, p in enumerate(layers):
        x = _ffn(x, p['ffn1_norm'], p['ffn1_w_gate'], p['ffn1_w_up'], p['ffn1_w_down'])
        x = _mixers(x, p, B, T, l, rope_tabs)
        last = l == len(layers) - 1
        x = _ffn(x, p['ffn2_norm'], p['ffn2_w_gate'], p['ffn2_w_up'], p['ffn2_w_down'],
                 final_g=final_norm if last else None)
    return x.reshape(B, T, D)


def kernel(x_prompt, x_sample, ffn1_norm, ffn1_w_gate, ffn1_w_up, ffn1_w_down, mix_norm, w_in, mlstm_conv_w, mlstm_conv_b, mlstm_gate_bias, mlstm_norm, diff_lambda, diff_norm, na_rpb, s5_lambda_re, s5_lambda_im, s5_log_step, s5_b_re, s5_b_im, s5_c_re, s5_c_im, s5_d, s5_glu_w, s5_glu_b, w_out, ffn2_norm, ffn2_w_gate, ffn2_w_up, ffn2_w_down, final_norm):
    w = dict(ffn1_norm=ffn1_norm, ffn1_w_gate=ffn1_w_gate, ffn1_w_up=ffn1_w_up, ffn1_w_down=ffn1_w_down,
             mix_norm=mix_norm, w_in=w_in, mlstm_conv_w=mlstm_conv_w, mlstm_conv_b=mlstm_conv_b,
             mlstm_gate_bias=mlstm_gate_bias, mlstm_norm=mlstm_norm, diff_lambda=diff_lambda, diff_norm=diff_norm,
             na_rpb=na_rpb, s5_lambda_re=s5_lambda_re, s5_lambda_im=s5_lambda_im, s5_log_step=s5_log_step,
             s5_b_re=s5_b_re, s5_b_im=s5_b_im, s5_c_re=s5_c_re, s5_c_im=s5_c_im, s5_d=s5_d,
             s5_glu_w=s5_glu_w, s5_glu_b=s5_glu_b, w_out=w_out, ffn2_norm=ffn2_norm, ffn2_w_gate=ffn2_w_gate,
             ffn2_w_up=ffn2_w_up, ffn2_w_down=ffn2_w_down)
    layers = [_prepare_layer(w, l) for l in range(ffn1_norm.shape[0])]
    return (_trunk(x_prompt, layers, final_norm), _trunk(x_sample, layers, final_norm))
```

```python
import functools
import math

import jax
import jax.numpy as jnp
from jax import lax
from jax.experimental import pallas as pl
from jax.experimental.pallas import tpu as pltpu

F32 = jnp.float32
BF16 = jnp.bfloat16

NORM_EPS = 1e-6
GROUP_W = 512
GRID_W = 64
MLSTM_HEADS = 4
MLSTM_DH = 128
MLSTM_CONV = 5
MLSTM_CHUNK = 128
DIFF_HEADS = 4
DIFF_DH = 64
ROPE_THETA = 10000.0
NA_HEADS = 8
NA_DH = 64
NA_WIN_ROWS = 8
NA_WIN_W = 16
NA_ROWS_PER_STEP = 8
S5_GROUP_CH = 16
S5_GROUPS = 32
S5_STATE = 64
S5_CHUNK = 32
LANES = 128
BF16_SUBLANES = 16
MASK_VALUE = -1e30
VMEM_LIMIT = 56 * 1024 * 1024


def _cparams(*sem):
    return pltpu.CompilerParams(dimension_semantics=sem, vmem_limit_bytes=VMEM_LIMIT)


def _rms(x, g):
    return x * lax.rsqrt(jnp.mean(x * x, axis=-1, keepdims=True) + NORM_EPS) * g


def _sigmoid(x):
    return 1.0 / (1.0 + jnp.exp(-x))


def _dot(a, b):
    return jnp.dot(a, b, preferred_element_type=F32)


def _dot_nt(a, b):
    return lax.dot_general(a, b, (((1,), (1,)), ((), ())), preferred_element_type=F32)


def _dot_tn(a, b):
    return lax.dot_general(a, b, (((0,), (0,)), ((), ())), preferred_element_type=F32)


def _split3(x):
    hi = x.astype(BF16)
    r1 = x - hi.astype(F32)
    mid = r1.astype(BF16)
    lo = (r1 - mid.astype(F32)).astype(BF16)
    return hi, mid, lo


def _ffn_kernel(x_ref, g_ref, wg_ref, wu_ref, wd_ref, *rest, final):
    if final:
        fg_ref, o_ref, h_ref = rest
    else:
        o_ref, h_ref = rest
    f = pl.program_id(1)
    nf = pl.num_programs(1)

    @pl.when(f == 0)
    def _():
        h_ref[...] = _rms(x_ref[...], g_ref[...]).astype(BF16)
        o_ref[...] = x_ref[...]

    h = h_ref[...]
    gate = _dot(h, wg_ref[...])
    up = _dot(h, wu_ref[...])
    act = (0.5 * gate * _sigmoid(gate) * up).astype(BF16)
    nchunk = 4
    cw = o_ref.shape[1] // nchunk
    for c in range(nchunk):
        o_ref[:, c * cw:(c + 1) * cw] += _dot(act, wd_ref[:, c * cw:(c + 1) * cw])

    if final:
        @pl.when(f == nf - 1)
        def _():
            o_ref[...] = _rms(o_ref[...], fg_ref[...])


def _ffn(x, g, wg, wu, wd, final_g=None, tm=1024, tf=512):
    M, D = x.shape
    F = wg.shape[1]
    final = final_g is not None
    in_specs = [
        pl.BlockSpec((tm, D), lambda i, f: (i, 0), pipeline_mode=pl.Buffered(1)),
        pl.BlockSpec((1, D), lambda i, f: (0, 0)),
        pl.BlockSpec((D, tf), lambda i, f: (0, f)),
        pl.BlockSpec((D, tf), lambda i, f: (0, f)),
        pl.BlockSpec((tf, D), lambda i, f: (f, 0)),
    ]
    args = [x, g.reshape(1, D), wg, wu, wd]
    if final:
        in_specs.append(pl.BlockSpec((1, D), lambda i, f: (0, 0)))
        args.append(final_g.reshape(1, D))
    return pl.pallas_call(
        functools.partial(_ffn_kernel, final=final),
        grid=(M // tm, F // tf),
        in_specs=in_specs,
        out_specs=pl.BlockSpec((tm, D), lambda i, f: (i, 0)),
        out_shape=jax.ShapeDtypeStruct((M, D), F32),
        scratch_shapes=[pltpu.VMEM((tm, D), BF16)],
        compiler_params=_cparams("parallel", "arbitrary"),
        name="ffn",
    )(*args)


def _inproj_kernel(x_ref, g_ref, w_ref, wgate_ref, o_ref, og_ref, h_ref):
    n = pl.program_id(1)

    @pl.when(n == 0)
    def _():
        h = _rms(x_ref[...], g_ref[...]).astype(BF16)
        h_ref[...] = h
        og_ref[...] = _dot(h, wgate_ref[...])

    o_ref[...] = _dot(h_ref[...], w_ref[...]).astype(BF16)


def _inproj(x, g, w_main, w_gate, tm=1024, tn=1408):
    M, D = x.shape
    N = w_main.shape[1]
    return pl.pallas_call(
        _inproj_kernel,
        grid=(M // tm, N // tn),
        in_specs=[
            pl.BlockSpec((tm, D), lambda i, n: (i, 0)),
            pl.BlockSpec((1, D), lambda i, n: (0, 0)),
            pl.BlockSpec((D, tn), lambda i, n: (0, n)),
            pl.BlockSpec((D, LANES), lambda i, n: (0, 0)),
        ],
        out_specs=[
            pl.BlockSpec((tm, tn), lambda i, n: (i, n)),
            pl.BlockSpec((tm, LANES), lambda i, n: (i, 0)),
        ],
        out_shape=[
            jax.ShapeDtypeStruct((M, N), BF16),
            jax.ShapeDtypeStruct((M, LANES), F32),
        ],
        scratch_shapes=[pltpu.VMEM((tm, D), BF16)],
        compiler_params=_cparams("parallel", "arbitrary"),
        name="inproj",
    )(x, g.reshape(1, D), w_main, w_gate)


def _conv_kernel(x_ref, prev_ref, next_ref, w_ref, b_ref, q_ref, k_ref, *, blocks_per_seq):
    tb = x_ref.shape[0]
    halo = prev_ref.shape[0]
    pos = pl.program_id(0) % blocks_per_seq
    prev = jnp.where(pos != 0, prev_ref[...].astype(F32), 0.0)
    nxt = jnp.where(pos != blocks_per_seq - 1, next_ref[...].astype(F32), 0.0)
    ext = jnp.concatenate([prev, x_ref[...].astype(F32), nxt], axis=0)
    rows = tb + 2 * halo
    pad = MLSTM_CONV // 2
    acc = None
    for j in range(MLSTM_CONV):
        shift = (pad - j) % rows
        sh = ext if shift == 0 else pltpu.roll(ext, shift, axis=0)
        term = sh[halo:halo + tb] * w_ref[j:j + 1, :]
        acc = term if acc is None else acc + term
    acc = acc + b_ref[...]
    y = acc * _sigmoid(acc)
    q_ref[...] = y[:, :GROUP_W].astype(BF16)
    k_ref[...] = (y[:, GROUP_W:] * (MLSTM_DH ** -0.5)).astype(BF16)


def _mlstm_conv(proj, conv_w, conv_b, T, tb=256):
    M = proj.shape[0]
    C = 2 * GROUP_W
    halo = BF16_SUBLANES
    r = tb // halo
    nhalo = M // halo
    w = jnp.zeros((8, C), F32).at[:MLSTM_CONV].set(conv_w)
    return pl.pallas_call(
        functools.partial(_conv_kernel, blocks_per_seq=T // tb),
        grid=(M // tb,),
        in_specs=[
            pl.BlockSpec((tb, C), lambda i: (i, 0)),
            pl.BlockSpec((halo, C), lambda i: (jnp.maximum(i * r - 1, 0), 0)),
            pl.BlockSpec((halo, C), lambda i: (jnp.minimum((i + 1) * r, nhalo - 1), 0)),
            pl.BlockSpec((8, C), lambda i: (0, 0)),
            pl.BlockSpec((1, C), lambda i: (0, 0)),
        ],
        out_specs=[
            pl.BlockSpec((tb, GROUP_W), lambda i: (i, 0)),
            pl.BlockSpec((tb, GROUP_W), lambda i: (i, 0)),
        ],
        out_shape=[jax.ShapeDtypeStruct((M, GROUP_W), BF16)] * 2,
        compiler_params=_cparams("parallel"),
        name="mlstm_conv",
    )(proj, proj, proj, w, conv_b.reshape(1, C))


def _mlstm_kernel(qf_ref, kf_ref, vf_ref, gf_ref, qb_ref, kb_ref, vb_ref, gb_ref, bias_ref,
                  of_ref, ob_ref, c_ref, m_ref):
    L = qf_ref.shape[0]
    H, dh = MLSTM_HEADS, MLSTM_DH

    @pl.when(pl.program_id(1) == 0)
    def _():
        c_ref[...] = jnp.zeros_like(c_ref)
        m_ref[...] = jnp.zeros_like(m_ref)

    row = lax.broadcasted_iota(jnp.int32, (L, L), 0)
    col = lax.broadcasted_iota(jnp.int32, (L, L), 1)
    ones_blk = jnp.ones((L, dh), BF16)

    for d, (q_ref, k_ref, v_ref, g_ref, o_ref) in enumerate(
            ((qf_ref, kf_ref, vf_ref, gf_ref, of_ref), (qb_ref, kb_ref, vb_ref, gb_ref, ob_ref))):
        seen = (col <= row) if d == 0 else (col >= row)
        tri = jnp.where(seen, 1.0, 0.0).astype(BF16)
        gt = g_ref[...] + bias_ref[...]
        lf = jnp.minimum(gt, 0.0) - jnp.log(1.0 + jnp.exp(-jnp.abs(gt)))
        hi, mid, lo = _split3(lf)
        bcum = _dot(tri, hi) + _dot(tri, mid) + _dot(tri, lo)
        bcum_t = bcum.T
        gt_t = gt.T
        end = L - 1 if d == 0 else 0
        for h in range(H):
            ic = 8 * d + h
            fc = 8 * d + 4 + h
            idx = d * H + h
            b_rep = jnp.broadcast_to(bcum[:, fc:fc + 1], (L, dh))
            i_rep = jnp.broadcast_to(gt[:, ic:ic + 1], (L, dh))
            b_row = bcum_t[fc:fc + 1, :]
            i_row = gt_t[ic:ic + 1, :]
            g_tot = b_rep[end:end + 1, :]
            m_prev = m_ref[idx]
            c_prev = c_ref[idx]
            q = q_ref[:, h * dh:(h + 1) * dh]
            k = k_ref[:, h * dh:(h + 1) * dh]
            v_aug = jnp.concatenate([v_ref[:, h * dh:(h + 1) * dh], ones_blk], axis=1)

            dlog = jnp.where(seen, b_rep - b_row + i_row, -jnp.inf)
            inter = b_rep + m_prev
            m_t = jnp.maximum(inter, jnp.max(dlog, axis=1, keepdims=True))
            s = (_dot_nt(q, k) * jnp.exp(dlog - m_t)).astype(BF16)
            s_inter = jnp.exp(inter - m_t)
            num_aug = jnp.concatenate([s_inter, s_inter], axis=1) * _dot(q, c_prev.astype(BF16)) + _dot(s, v_aug)
            den = jnp.maximum(jnp.abs(num_aug[:, dh:]), jnp.exp(-m_t))
            o_ref[:, h * dh:(h + 1) * dh] = num_aug[:, :dh] / den

            a_rep = g_tot - b_rep + i_rep
            m_new = jnp.maximum(g_tot + m_prev, jnp.max(a_rep, axis=0, keepdims=True))
            decay = jnp.exp(g_tot + m_prev - m_new)
            kw = (k.astype(F32) * jnp.exp(a_rep - m_new)).astype(BF16)
            c_ref[idx] = jnp.concatenate([decay, decay], axis=1) * c_prev + _dot_tn(kw, v_aug)
            m_ref[idx] = m_new


def _mlstm(qc, kc, proj, gates, gate_bias, B, T):
    M = qc.shape[0]
    L = MLSTM_CHUNK
    assert L == LANES == MLSTM_DH
    nc = T // L
    W = GROUP_W
    vblk = 2
    fwd = lambda b, i: (b * nc + i, 0)
    bwd = lambda b, i: (b * nc + nc - 1 - i, 0)
    fwd_v = lambda b, i: (b * nc + i, vblk)
    bwd_v = lambda b, i: (b * nc + nc - 1 - i, vblk)
    bias = jnp.zeros((1, LANES), F32).at[0, :4 * MLSTM_HEADS].set(gate_bias)
    return pl.pallas_call(
        _mlstm_kernel,
        grid=(B, nc),
        in_specs=[
            pl.BlockSpec((L, W), fwd), pl.BlockSpec((L, W), fwd), pl.BlockSpec((L, W), fwd_v),
            pl.BlockSpec((L, LANES), fwd),
            pl.BlockSpec((L, W), bwd), pl.BlockSpec((L, W), bwd), pl.BlockSpec((L, W), bwd_v),
            pl.BlockSpec((L, LANES), bwd),
            pl.BlockSpec((1, LANES), lambda b, i: (0, 0)),
        ],
        out_specs=[pl.BlockSpec((L, W), fwd), pl.BlockSpec((L, W), bwd)],
        out_shape=[jax.ShapeDtypeStruct((M, W), F32)] * 2,
        scratch_shapes=[
            pltpu.VMEM((2 * MLSTM_HEADS, MLSTM_DH, 2 * MLSTM_DH), F32),
            pltpu.VMEM((2 * MLSTM_HEADS, 1, LANES), F32),
        ],
        compiler_params=_cparams("parallel", "arbitrary"),
        name="mlstm",
    )(qc, kc, proj, gates, qc, kc, proj, gates, bias)


def _rope_kernel(x_ref, cos_ref, sin_ref, q_ref, k_ref):
    x = x_ref[...].astype(F32)
    reps = x.shape[1] // LANES
    cos = jnp.concatenate([cos_ref[...]] * reps, axis=1)
    sin = jnp.concatenate([sin_ref[...]] * reps, axis=1)
    lane = lax.broadcasted_iota(jnp.int32, x.shape, 1)
    width = x.shape[1]
    half = DIFF_DH // 2
    rot = jnp.where(lane % DIFF_DH < half, pltpu.roll(x, width - half, axis=1), pltpu.roll(x, half, axis=1))
    y = x * cos + rot * sin
    q_ref[...] = (y[:, :GROUP_W] * (DIFF_DH ** -0.5 * math.log2(math.e))).astype(BF16)
    k_ref[...] = y[:, GROUP_W:].astype(BF16)


def _rope_tables(T):
    half = DIFF_DH // 2
    inv = 1.0 / (ROPE_THETA ** (jnp.arange(0, DIFF_DH, 2, dtype=F32) / DIFF_DH))
    ang = jnp.arange(T, dtype=F32)[:, None] * inv[None, :]
    cos, sin = jnp.cos(ang), jnp.sin(ang)
    cos_t = jnp.concatenate([cos, cos] * (LANES // DIFF_DH), axis=1)
    sin_t = jnp.concatenate([-sin, sin] * (LANES // DIFF_DH), axis=1)
    return cos_t, sin_t


def _rope(proj, cos_t, sin_t, T, tb=256):
    M = proj.shape[0]
    nb = T // tb
    return pl.pallas_call(
        _rope_kernel,
        grid=(M // tb,),
        in_specs=[
            pl.BlockSpec((tb, 2 * GROUP_W), lambda i: (i, 2)),
            pl.BlockSpec((tb, LANES), lambda i: (i % nb, 0)),
            pl.BlockSpec((tb, LANES), lambda i: (i % nb, 0)),
        ],
        out_specs=[pl.BlockSpec((tb, GROUP_W), lambda i: (i, 0))] * 2,
        out_shape=[jax.ShapeDtypeStruct((M, GROUP_W), BF16)] * 2,
        compiler_params=_cparams("parallel"),
        name="rope",
    )(proj, cos_t, sin_t)


def _diff_kernel(q_ref, k_ref, v_ref, lp_ref, g_ref, o_ref, q2_ref, m_ref, acc_ref, *, lam_init, tks):
    ki = pl.program_id(3)
    tq = q_ref.shape[0]
    dv = v_ref.shape[1]
    ones = jnp.ones((tks, dv), BF16)

    @pl.when(ki == 0)
    def _():
        q = q_ref[...]
        lane = lax.broadcasted_iota(jnp.int32, q.shape, 1)
        zero = jnp.zeros_like(q)
        q2_ref[0:tq, :] = jnp.where(lane < DIFF_DH, q, zero)
        q2_ref[tq:2 * tq, :] = jnp.where(lane < DIFF_DH, zero, q)
        m_ref[...] = jnp.full_like(m_ref, -jnp.inf)
        acc_ref[...] = jnp.zeros_like(acc_ref)

    q2 = q2_ref[...]
    for j in range(k_ref.shape[0] // tks):
        s = _dot_nt(q2, k_ref[j * tks:(j + 1) * tks, :])
        tiles = [s[:, c * LANES:(c + 1) * LANES] for c in range(tks // LANES)]
        m_prev = m_ref[...]
        m_new = jnp.maximum(m_prev, jnp.max(functools.reduce(jnp.maximum, tiles), axis=1, keepdims=True))
        alpha = jnp.exp2(m_prev - m_new)
        p = jnp.concatenate([jnp.exp2(t - m_new).astype(BF16) for t in tiles], axis=1)
        v_aug = jnp.concatenate([v_ref[j * tks:(j + 1) * tks, :], ones], axis=1)
        acc_ref[...] = jnp.concatenate([alpha, alpha], axis=1) * acc_ref[...] + _dot(p, v_aug)
        m_ref[...] = m_new

    @pl.when(ki == pl.num_programs(3) - 1)
    def _():
        lp = lp_ref[...]
        lam = (jnp.exp(jnp.sum(lp[0:1] * lp[1:2], axis=1, keepdims=True))
               - jnp.exp(jnp.sum(lp[2:3] * lp[3:4], axis=1, keepdims=True)) + lam_init)
        o = acc_ref[:, 0:dv] / acc_ref[:, dv:2 * dv]
        out = o[0:tq] - lam * o[tq:2 * tq]
        o_ref[...] = (_rms(out, g_ref[...]) * (1.0 - lam_init)).astype(BF16)


def _diff_attn(qr, kr, proj, lam_params, norm_g, B, T, layer_idx, tq=512, tk=2048, tks=512):
    M = qr.shape[0]
    dv = 2 * DIFF_DH
    tk = min(tk, T)
    nq, nk = T // tq, T // tk
    vblk0 = (6 * GROUP_W) // dv
    lam_init = 0.8 - 0.6 * math.exp(-0.3 * layer_idx)
    return pl.pallas_call(
        functools.partial(_diff_kernel, lam_init=lam_init, tks=tks),
        grid=(B, DIFF_HEADS, nq, nk),
        in_specs=[
            pl.BlockSpec((tq, dv), lambda b, h, qi, ki: (b * nq + qi, h)),
            pl.BlockSpec((tk, dv), lambda b, h, qi, ki: (b * nk + ki, h)),
            pl.BlockSpec((tk, dv), lambda b, h, qi, ki: (b * nk + ki, vblk0 + h)),
            pl.BlockSpec((4, DIFF_DH), lambda b, h, qi, ki: (0, 0)),
            pl.BlockSpec((1, dv), lambda b, h, qi, ki: (0, 0)),
        ],
        out_specs=pl.BlockSpec((tq, dv), lambda b, h, qi, ki: (b * nq + qi, h)),
        out_shape=jax.ShapeDtypeStruct((M, GROUP_W), BF16),
        scratch_shapes=[
            pltpu.VMEM((2 * tq, dv), BF16),
            pltpu.VMEM((2 * tq, LANES), F32),
            pltpu.VMEM((2 * tq, 2 * dv), F32),
        ],
        compiler_params=_cparams("parallel", "parallel", "parallel", "arbitrary"),
        name="diff_attn",
    )(qr, kr, proj, lam_params, norm_g.reshape(1, dv))


def _na_kernel(q_ref, k_ref, v_ref, bias_ref, o_ref, *, rows):
    g = pl.program_id(2)
    win = NA_WIN_ROWS * GRID_W
    lane = lax.broadcasted_iota(jnp.int32, (GRID_W, 2 * NA_DH), 1)
    first = lane < NA_DH
    w = 2 * NA_DH
    ones = jnp.ones((win, w), BF16)
    scores, starts = [], []
    for jr in range(NA_ROWS_PER_STEP):
        r = g * NA_ROWS_PER_STEP + jr
        rs = jnp.clip(r - NA_WIN_ROWS // 2, 0, rows - NA_WIN_ROWS)
        start = pl.multiple_of(rs * GRID_W, GRID_W)
        q = q_ref[jr * GRID_W:(jr + 1) * GRID_W, :]
        zero = jnp.zeros_like(q)
        q2 = jnp.concatenate([jnp.where(first, q, zero), jnp.where(first, zero, q)], axis=0)
        s = _dot_nt(q2, k_ref[pl.ds(start, win), :]) * (NA_DH ** -0.5 * math.log2(math.e)) + bias_ref[r - rs]
        scores.append(s)
        starts.append(start)
    for jr in range(NA_ROWS_PER_STEP):
        s = scores[jr]
        e = jnp.exp2(s - jnp.max(s, axis=1, keepdims=True)).astype(BF16)
        pv = _dot(e, jnp.concatenate([v_ref[pl.ds(starts[jr], win), :], ones], axis=1))
        pv = pv[:, :w] / pv[:, w:]
        o_ref[jr * GRID_W:(jr + 1) * GRID_W, :] = jnp.where(first, pv[:GRID_W], pv[GRID_W:]).astype(BF16)


def _na_bias_table(rpb):
    c = jnp.arange(GRID_W)
    col_start = jnp.clip(c - NA_WIN_W // 2, 0, GRID_W - NA_WIN_W)
    kc = jnp.arange(GRID_W)
    valid = (kc[None, :] >= col_start[:, None]) & (kc[None, :] < col_start[:, None] + NA_WIN_W)
    col_off = jnp.clip(kc[None, :] - c[:, None] + NA_WIN_W - 1, 0, 2 * NA_WIN_W - 2)
    p = jnp.arange(NA_WIN_ROWS)
    k = jnp.arange(NA_WIN_ROWS)
    row_off = k[None, :] - p[:, None] + NA_WIN_ROWS - 1
    t = rpb.astype(F32)[:, row_off][:, :, :, col_off] * math.log2(math.e)
    t = jnp.where(valid[None, None, None], t, MASK_VALUE)
    t = t.transpose(0, 1, 3, 2, 4).reshape(NA_HEADS, NA_WIN_ROWS, GRID_W, NA_WIN_ROWS * GRID_W)
    t = t.reshape(NA_HEADS // 2, 2, NA_WIN_ROWS, GRID_W, NA_WIN_ROWS * GRID_W).transpose(0, 2, 1, 3, 4)
    return t.reshape(NA_HEADS // 2, NA_WIN_ROWS, 2 * GRID_W, NA_WIN_ROWS * GRID_W)


def _na_attn(proj, bias_table, B, T):
    M = proj.shape[0]
    rows = T // GRID_W
    tq = NA_ROWS_PER_STEP * GRID_W
    ng = rows // NA_ROWS_PER_STEP
    hp = NA_HEADS // 2
    w = 2 * NA_DH
    qblk0, kblk0, vblk0 = (7 * GROUP_W) // w, (8 * GROUP_W) // w, (9 * GROUP_W) // w
    return pl.pallas_call(
        functools.partial(_na_kernel, rows=rows),
        grid=(B, hp, ng),
        in_specs=[
            pl.BlockSpec((tq, w), lambda b, h, g: (b * ng + g, qblk0 + h)),
            pl.BlockSpec((T, w), lambda b, h, g: (b, kblk0 + h)),
            pl.BlockSpec((T, w), lambda b, h, g: (b, vblk0 + h)),
            pl.BlockSpec((None, NA_WIN_ROWS, 2 * GRID_W, NA_WIN_ROWS * GRID_W), lambda b, h, g: (h, 0, 0, 0)),
        ],
        out_specs=pl.BlockSpec((tq, w), lambda b, h, g: (b * ng + g, h)),
        out_shape=jax.ShapeDtypeStruct((M, GROUP_W), BF16),
        compiler_params=_cparams("parallel", "parallel", "arbitrary"),
        name="na_attn",
    )(proj, proj, proj, bias_table)


def _s5_params(lam_re, lam_im, log_step, b_re, b_im, c_re, c_im, d_skip):
    L, c, P, G = S5_CHUNK, S5_GROUP_CH, S5_STATE, S5_GROUPS
    lam = lax.complex(lam_re.astype(F32), lam_im.astype(F32))
    step = jnp.exp(log_step.astype(F32))[..., None]
    log_lam_bar = lam * step
    lam_bar = jnp.exp(log_lam_bar)
    b_mat = lax.complex(b_re.astype(F32), b_im.astype(F32))
    c_mat = lax.complex(c_re.astype(F32), c_im.astype(F32))
    b_bar = ((lam_bar - 1.0) / lam)[..., None] * b_mat[None]
    pw = jnp.exp(log_lam_bar[:, None] * jnp.arange(L + 1, dtype=F32)[None, :, None, None])
    cb = c_mat[None, :, :, None, :] * b_bar.transpose(0, 1, 3, 2)[:, :, None, :, :]
    s_i = jnp.arange(L)[:, None]
    t_i = jnp.arange(L)[None, :]
    lag = jnp.stack([t_i - s_i, s_i - t_i])
    lag_pw = jnp.where((lag >= 0)[..., None, None],
                       jnp.exp(log_lam_bar[:, None, None] * jnp.maximum(lag, 0).astype(F32)[..., None, None]),
                       0.0)
    hp = lax.Precision.HIGHEST
    m = (jnp.einsum('dgoip,dstgp->gsito', cb.real, lag_pw.real, precision=hp)
         - jnp.einsum('dgoip,dstgp->gsito', cb.imag, lag_pw.imag, precision=hp))
    m = m.reshape(G, L * c, L * c)
    ef = pw[0][L - 1 - jnp.arange(L)][:, :, :, None] * b_bar[0][None]
    eb = pw[1][jnp.arange(L)][:, :, :, None] * b_bar[1][None]
    to_rows = lambda e: e.transpose(1, 0, 3, 2).reshape(G, L * c, P)
    ef, eb = to_rows(ef), to_rows(eb)
    bs = jnp.concatenate([ef.real, ef.imag, eb.real, eb.imag, ef.imag, ef.real, eb.imag, eb.real], axis=-1)
    of = c_mat[:, None] * pw[0][1:L + 1].transpose(1, 0, 2)[:, :, None, :]
    ob = c_mat[:, None] * pw[1][L - jnp.arange(L)].transpose(1, 0, 2)[:, :, None, :]
    to_cols = lambda o: o.transpose(0, 3, 1, 2).reshape(G, P, L * c)
    of, ob = to_cols(of), to_cols(ob)
    cs = jnp.concatenate([of.real, -of.imag, ob.real, -ob.imag], axis=1)
    al = pw[:, L]
    a1 = jnp.concatenate([al[0].real, al[0].real, al[1].real, al[1].real], axis=-1)
    a2 = jnp.concatenate([-al[0].imag, al[0].imag, -al[1].imag, al[1].imag], axis=-1)
    a = jnp.stack([a1, a2, a1, -a2], axis=1)
    a = jnp.concatenate([a, jnp.zeros((G, 4, 4 * P), F32)], axis=1)
    dvec = jnp.tile(d_skip.astype(F32)[:, None, :], (1, L, 1)).reshape(G, 1, L * c)
    return m.astype(BF16), bs.astype(BF16), cs.astype(BF16), a, dvec


def _s5_kernel(x_ref, m_ref, bs_ref, cs_ref, a_ref, d_ref, o_ref, s_ref, h_ref):
    nc = x_ref.shape[0]
    x = x_ref[...]
    s_ref[...] = _dot(x, bs_ref[...])
    a1f, a1b = a_ref[0:1, 0:128], a_ref[0:1, 128:256]
    a2f, a2b = a_ref[1:2, 0:128], a_ref[1:2, 128:256]
    a2fs, a2bs = a_ref[3:4, 0:128], a_ref[3:4, 128:256]

    sub = 8
    nt = nc // sub

    def step(i, carry):
        hf, hfs, hb, hbs = carry
        rf = pl.multiple_of(i * sub, sub)
        rb = pl.multiple_of((nt - 1 - i) * sub, sub)
        sf, sfs = s_ref[pl.ds(rf, sub), 0:128], s_ref[pl.ds(rf, sub), 256:384]
        sb, sbs = s_ref[pl.ds(rb, sub), 128:256], s_ref[pl.ds(rb, sub), 384:512]
        hf_rows, hb_rows = [], [None] * sub
        for j in range(sub):
            jb = sub - 1 - j
            hf_rows.append(hf)
            hb_rows[jb] = hb
            hf, hfs = (a1f * hf + a2f * hfs + sf[j:j + 1], a1f * hfs + a2fs * hf + sfs[j:j + 1])
            hb, hbs = (a1b * hb + a2b * hbs + sb[jb:jb + 1], a1b * hbs + a2bs * hb + sbs[jb:jb + 1])
        h_ref[pl.ds(rf, sub), 0:128] = jnp.concatenate(hf_rows, axis=0)
        h_ref[pl.ds(rb, sub), 128:256] = jnp.concatenate(hb_rows, axis=0)
        return hf, hfs, hb, hbs

    z = jnp.zeros((1, 128), F32)
    lax.fori_loop(0, nt, step, (z, z, z, z))
    y = _dot(x, m_ref[...]) + _dot(h_ref[...].astype(BF16), cs_ref[...]) + x.astype(F32) * d_ref[...]
    o_ref[...] = y


def _s5(proj, params, B, T):
    m, bs, cs, a, dvec = params
    L, c, G = S5_CHUNK, S5_GROUP_CH, S5_GROUPS
    nc = T // L
    w = L * c
    u = proj[:, 10 * GROUP_W:]
    x = u.reshape(B, nc, L, G, c).transpose(0, 3, 1, 2, 4).reshape(B, G, nc, w)
    y = pl.pallas_call(
        _s5_kernel,
        grid=(B, G),
        in_specs=[
            pl.BlockSpec((None, None, nc, w), lambda b, g: (b, g, 0, 0)),
            pl.BlockSpec((None, w, w), lambda b, g: (g, 0, 0)),
            pl.BlockSpec((None, w, 512), lambda b, g: (g, 0, 0)),
            pl.BlockSpec((None, 256, w), lambda b, g: (g, 0, 0)),
            pl.BlockSpec((None, 8, 256), lambda b, g: (g, 0, 0)),
            pl.BlockSpec((None, 1, w), lambda b, g: (g, 0, 0)),
        ],
        out_specs=pl.BlockSpec((None, None, nc, w), lambda b, g: (b, g, 0, 0)),
        out_shape=jax.ShapeDtypeStruct((B, G, nc, w), F32),
        scratch_shapes=[pltpu.VMEM((nc, 512), F32), pltpu.VMEM((nc, 256), F32)],
        compiler_params=_cparams("parallel", "arbitrary"),
        name="s5",
    )(x, m, bs, cs, a, dvec)
    return y.reshape(B, G, nc, L, c).transpose(0, 2, 3, 1, 4).reshape(B * T, GROUP_W)


def _outproj_kernel(x_ref, hf_ref, hb_ref, og_ref, mg_ref, yb_ref, yc_ref, ys_ref, gw_ref, gb_ref, w_ref, o_ref):
    W = GROUP_W
    h = hf_ref[...] + hb_ref[...]
    parts = []
    for j in range(MLSTM_HEADS):
        hj = h[:, j * MLSTM_DH:(j + 1) * MLSTM_DH]
        parts.append(hj * lax.rsqrt(jnp.mean(hj * hj, axis=-1, keepdims=True) + NORM_EPS))
    hn = jnp.concatenate(parts, axis=1) * mg_ref[...]
    ya = (hn * _sigmoid(og_ref[...].astype(F32))).astype(BF16)
    y = ys_ref[...]
    z = 0.5 * y * (1.0 + jnp.tanh(math.sqrt(2.0 / math.pi) * (y + 0.044715 * (y * y * y))))
    yd = (z * _sigmoid(_dot(z.astype(BF16), gw_ref[...]) + gb_ref[...])).astype(BF16)
    acc = x_ref[...] + _dot(ya, w_ref[0:W, :])
    acc = acc + _dot(yb_ref[...], w_ref[W:2 * W, :])
    acc = acc + _dot(yc_ref[...], w_ref[2 * W:3 * W, :])
    acc = acc + _dot(yd, w_ref[3 * W:4 * W, :])
    o_ref[...] = acc


def _outproj(x, hf, hb, proj, mlstm_norm, yb, yc, ys, glu_w, glu_b, w_out, tm=512):
    M, D = x.shape
    W = GROUP_W
    tok = lambda i: (i, 0)
    const = lambda i: (0, 0)
    return pl.pallas_call(
        _outproj_kernel,
        grid=(M // tm,),
        in_specs=[
            pl.BlockSpec((tm, D), tok),
            pl.BlockSpec((tm, W), tok), pl.BlockSpec((tm, W), tok),
            pl.BlockSpec((tm, W), lambda i: (i, 3)),
            pl.BlockSpec((1, W), const),
            pl.BlockSpec((tm, W), tok), pl.BlockSpec((tm, W), tok), pl.BlockSpec((tm, W), tok),
            pl.BlockSpec((W, W), const), pl.BlockSpec((1, W), const),
            pl.BlockSpec((4 * W, D), const),
        ],
        out_specs=pl.BlockSpec((tm, D), tok),
        out_shape=jax.ShapeDtypeStruct((M, D), F32),
        compiler_params=_cparams("parallel"),
        name="outproj",
    )(x, hf, hb, proj, mlstm_norm.reshape(1, W), yb, yc, ys, glu_w, glu_b.reshape(1, W), w_out)


def _prepare_layer(w, l):
    W = GROUP_W
    ng = 4 * MLSTM_HEADS
    w_in = w['w_in'][l]
    w_main = jnp.concatenate([w_in[:, :4 * W], w_in[:, 4 * W + ng:]], axis=1).astype(BF16)
    w_gate = jnp.zeros((w_in.shape[0], LANES), BF16).at[:, :ng].set(w_in[:, 4 * W:4 * W + ng].astype(BF16))
    p = {k: w[k][l] for k in ('ffn1_norm', 'mix_norm', 'ffn2_norm', 'mlstm_conv_w', 'mlstm_conv_b', 'mlstm_gate_bias',
                              'mlstm_norm', 'diff_lambda', 'diff_norm', 's5_glu_b')}
    for k in ('ffn1_w_gate', 'ffn1_w_up', 'ffn1_w_down', 'ffn2_w_gate', 'ffn2_w_up', 'ffn2_w_down', 's5_glu_w', 'w_out'):
        p[k] = w[k][l].astype(BF16)
    p['w_main'], p['w_gate'] = w_main, w_gate
    p['na_bias'] = _na_bias_table(w['na_rpb'][l])
    p['s5'] = _s5_params(w['s5_lambda_re'][l], w['s5_lambda_im'][l], w['s5_log_step'][l], w['s5_b_re'][l],
                         w['s5_b_im'][l], w['s5_c_re'][l], w['s5_c_im'][l], w['s5_d'][l])
    return p


def _mixers(x, p, B, T, layer_idx, rope_tabs):
    proj, gates = _inproj(x, p['mix_norm'], p['w_main'], p['w_gate'])
    qc, kc = _mlstm_conv(proj, p['mlstm_conv_w'], p['mlstm_conv_b'], T)
    hf, hb = _mlstm(qc, kc, proj, gates, p['mlstm_gate_bias'], B, T)
    qr, kr = _rope(proj, rope_tabs[0], rope_tabs[1], T)
    yb = _diff_attn(qr, kr, proj, p['diff_lambda'], p['diff_norm'], B, T, layer_idx)
    yc = _na_attn(proj, p['na_bias'], B, T)
    ys = _s5(proj, p['s5'], B, T)
    return _outproj(x, hf, hb, proj, p['mlstm_norm'], yb, yc, ys, p['s5_glu_w'], p['s5_glu_b'], p['w_out'])


def _trunk(x, layers, final_norm):
    B, T, D = x.shape
    x = x.reshape(B * T, D)
    rope_tabs = _rope_tables(T)
    for l, p in enumerate(layers):
        x = _ffn(x, p['ffn1_norm'], p['ffn1_w_gate'], p['ffn1_w_up'], p['ffn1_w_down'])
        x = _mixers(x, p, B, T, l, rope_tabs)
        last = l == len(layers) - 1
        x = _ffn(x, p['ffn2_norm'], p['ffn2_w_gate'], p['ffn2_w_up'], p['ffn2_w_down'],
                 final_g=final_norm if last else None)
    return x.reshape(B, T, D)


def kernel(x_prompt, x_sample, ffn1_norm, ffn1_w_gate, ffn1_w_up, ffn1_w_down, mix_norm, w_in, mlstm_conv_w, mlstm_conv_b, mlstm_gate_bias, mlstm_norm, diff_lambda, diff_norm, na_rpb, s5_lambda_re, s5_lambda_im, s5_log_step, s5_b_re, s5_b_im, s5_c_re, s5_c_im, s5_d, s5_glu_w, s5_glu_b, w_out, ffn2_norm, ffn2_w_gate, ffn2_w_up, ffn2_w_down, final_norm):
    w = dict(ffn1_norm=ffn1_norm, ffn1_w_gate=ffn1_w_gate, ffn1_w_up=ffn1_w_up, ffn1_w_down=ffn1_w_down,
             mix_norm=mix_norm, w_in=w_in, mlstm_conv_w=mlstm_conv_w, mlstm_conv_b=mlstm_conv_b,
             mlstm_gate_bias=mlstm_gate_bias, mlstm_norm=mlstm_norm, diff_lambda=diff_lambda, diff_norm=diff_norm,
             na_rpb=na_rpb, s5_lambda_re=s5_lambda_re, s5_lambda_im=s5_lambda_im, s5_log_step=s5_log_step,
             s5_b_re=s5_b_re, s5_b_im=s5_b_im, s5_c_re=s5_c_re, s5_c_im=s5_c_im, s5_d=s5_d,
             s5_glu_w=s5_glu_w, s5_glu_b=s5_glu_b, w_out=w_out, ffn2_norm=ffn2_norm, ffn2_w_gate=ffn2_w_gate,
             ffn2_w_up=ffn2_w_up, ffn2_w_down=ffn2_w_down)
    layers = [_prepare_layer(w, l) for l in range(ffn1_norm.shape[0])]
    return (_trunk(x_prompt, layers, final_norm), _trunk(x_sample, layers, final_norm))
```

```python
import functools
import math

import jax
import jax.numpy as jnp
from jax import lax
from jax.experimental import pallas as pl
from jax.experimental.pallas import tpu as pltpu

F32 = jnp.float32
BF16 = jnp.bfloat16

NORM_EPS = 1e-6
GROUP_W = 512
GRID_W = 64
MLSTM_HEADS = 4
MLSTM_DH = 128
MLSTM_CONV = 5
MLSTM_CHUNK = 128
DIFF_HEADS = 4
DIFF_DH = 64
ROPE_THETA = 10000.0
NA_HEADS = 8
NA_DH = 64
NA_WIN_ROWS = 8
NA_WIN_W = 16
NA_ROWS_PER_STEP = 8
S5_GROUP_CH = 16
S5_GROUPS = 32
S5_STATE = 64
S5_CHUNK = 16
LANES = 128
BF16_SUBLANES = 16
MASK_VALUE = -1e30
VMEM_LIMIT = 56 * 1024 * 1024


def _cparams(*sem):
    return pltpu.CompilerParams(dimension_semantics=sem, vmem_limit_bytes=VMEM_LIMIT)


def _rms(x, g):
    return x * lax.rsqrt(jnp.mean(x * x, axis=-1, keepdims=True) + NORM_EPS) * g


def _sigmoid(x):
    return 1.0 / (1.0 + jnp.exp(-x))


def _dot(a, b):
    return jnp.dot(a, b, preferred_element_type=F32)


def _dot_nt(a, b):
    return lax.dot_general(a, b, (((1,), (1,)), ((), ())), preferred_element_type=F32)


def _dot_tn(a, b):
    return lax.dot_general(a, b, (((0,), (0,)), ((), ())), preferred_element_type=F32)


def _split3(x):
    hi = x.astype(BF16)
    r1 = x - hi.astype(F32)
    mid = r1.astype(BF16)
    lo = (r1 - mid.astype(F32)).astype(BF16)
    return hi, mid, lo


def _ffn_kernel(x_ref, g_ref, wg_ref, wu_ref, wd_ref, *rest, final):
    if final:
        fg_ref, o_ref, h_ref = rest
    else:
        o_ref, h_ref = rest
    f = pl.program_id(1)
    nf = pl.num_programs(1)

    @pl.when(f == 0)
    def _():
        h_ref[...] = _rms(x_ref[...], g_ref[...]).astype(BF16)
        o_ref[...] = x_ref[...]

    h = h_ref[...]
    gate = _dot(h, wg_ref[...])
    up = _dot(h, wu_ref[...])
    act = (0.5 * gate * _sigmoid(gate) * up).astype(BF16)
    nchunk = 4
    cw = o_ref.shape[1] // nchunk
    for c in range(nchunk):
        o_ref[:, c * cw:(c + 1) * cw] += _dot(act, wd_ref[:, c * cw:(c + 1) * cw])

    if final:
        @pl.when(f == nf - 1)
        def _():
            o_ref[...] = _rms(o_ref[...], fg_ref[...])


def _ffn(x, g, wg, wu, wd, final_g=None, tm=1024, tf=512):
    M, D = x.shape
    F = wg.shape[1]
    final = final_g is not None
    in_specs = [
        pl.BlockSpec((tm, D), lambda i, f: (i, 0), pipeline_mode=pl.Buffered(1)),
        pl.BlockSpec((1, D), lambda i, f: (0, 0)),
        pl.BlockSpec((D, tf), lambda i, f: (0, f)),
        pl.BlockSpec((D, tf), lambda i, f: (0, f)),
        pl.BlockSpec((tf, D), lambda i, f: (f, 0)),
    ]
    args = [x, g.reshape(1, D), wg, wu, wd]
    if final:
        in_specs.append(pl.BlockSpec((1, D), lambda i, f: (0, 0)))
        args.append(final_g.reshape(1, D))
    return pl.pallas_call(
        functools.partial(_ffn_kernel, final=final),
        grid=(M // tm, F // tf),
        in_specs=in_specs,
        out_specs=pl.BlockSpec((tm, D), lambda i, f: (i, 0)),
        out_shape=jax.ShapeDtypeStruct((M, D), F32),
        scratch_shapes=[pltpu.VMEM((tm, D), BF16)],
        compiler_params=_cparams("parallel", "arbitrary"),
        name="ffn",
    )(*args)


def _inproj_kernel(x_ref, g_ref, w_ref, wgate_ref, o_ref, og_ref, os_ref, h_ref):
    n = pl.program_id(1)

    @pl.when(n == 0)
    def _():
        h = _rms(x_ref[...], g_ref[...]).astype(BF16)
        h_ref[...] = h
        og_ref[...] = _dot(h, wgate_ref[...])

    res = _dot(h_ref[...], w_ref[...])
    o_ref[...] = res.astype(BF16)

    @pl.when(n == pl.num_programs(1) - 1)
    def _():
        os_ref[...] = res[:, res.shape[1] - GROUP_W:]


def _inproj(x, g, w_main, w_gate, tm=1024, tn=1408):
    M, D = x.shape
    N = w_main.shape[1]
    assert tn >= GROUP_W and N % tn == 0
    return pl.pallas_call(
        _inproj_kernel,
        grid=(M // tm, N // tn),
        in_specs=[
            pl.BlockSpec((tm, D), lambda i, n: (i, 0)),
            pl.BlockSpec((1, D), lambda i, n: (0, 0)),
            pl.BlockSpec((D, tn), lambda i, n: (0, n)),
            pl.BlockSpec((D, LANES), lambda i, n: (0, 0)),
        ],
        out_specs=[
            pl.BlockSpec((tm, tn), lambda i, n: (i, n)),
            pl.BlockSpec((tm, LANES), lambda i, n: (i, 0)),
            pl.BlockSpec((tm, GROUP_W), lambda i, n: (i, 0)),
        ],
        out_shape=[
            jax.ShapeDtypeStruct((M, N), BF16),
            jax.ShapeDtypeStruct((M, LANES), F32),
            jax.ShapeDtypeStruct((M, GROUP_W), F32),
        ],
        scratch_shapes=[pltpu.VMEM((tm, D), BF16)],
        compiler_params=_cparams("parallel", "arbitrary"),
        name="inproj",
    )(x, g.reshape(1, D), w_main, w_gate)


def _conv_kernel(x_ref, prev_ref, next_ref, w_ref, b_ref, q_ref, k_ref, *, blocks_per_seq):
    tb = x_ref.shape[0]
    halo = prev_ref.shape[0]
    pos = pl.program_id(0) % blocks_per_seq
    prev = jnp.where(pos != 0, prev_ref[...].astype(F32), 0.0)
    nxt = jnp.where(pos != blocks_per_seq - 1, next_ref[...].astype(F32), 0.0)
    ext = jnp.concatenate([prev, x_ref[...].astype(F32), nxt], axis=0)
    rows = tb + 2 * halo
    pad = MLSTM_CONV // 2
    acc = None
    for j in range(MLSTM_CONV):
        shift = (pad - j) % rows
        sh = ext if shift == 0 else pltpu.roll(ext, shift, axis=0)
        term = sh[halo:halo + tb] * w_ref[j:j + 1, :]
        acc = term if acc is None else acc + term
    acc = acc + b_ref[...]
    y = acc * _sigmoid(acc)
    q_ref[...] = y[:, :GROUP_W].astype(BF16)
    k_ref[...] = (y[:, GROUP_W:] * (MLSTM_DH ** -0.5)).astype(BF16)


def _mlstm_conv(proj, conv_w, conv_b, T, tb=256):
    M = proj.shape[0]
    C = 2 * GROUP_W
    halo = BF16_SUBLANES
    r = tb // halo
    nhalo = M // halo
    w = jnp.zeros((8, C), F32).at[:MLSTM_CONV].set(conv_w)
    return pl.pallas_call(
        functools.partial(_conv_kernel, blocks_per_seq=T // tb),
        grid=(M // tb,),
        in_specs=[
            pl.BlockSpec((tb, C), lambda i: (i, 0)),
            pl.BlockSpec((halo, C), lambda i: (jnp.maximum(i * r - 1, 0), 0)),
            pl.BlockSpec((halo, C), lambda i: (jnp.minimum((i + 1) * r, nhalo - 1), 0)),
            pl.BlockSpec((8, C), lambda i: (0, 0)),
            pl.BlockSpec((1, C), lambda i: (0, 0)),
        ],
        out_specs=[
            pl.BlockSpec((tb, GROUP_W), lambda i: (i, 0)),
            pl.BlockSpec((tb, GROUP_W), lambda i: (i, 0)),
        ],
        out_shape=[jax.ShapeDtypeStruct((M, GROUP_W), BF16)] * 2,
        compiler_params=_cparams("parallel"),
        name="mlstm_conv",
    )(proj, proj, proj, w, conv_b.reshape(1, C))


def _mlstm_kernel(qf_ref, kf_ref, vf_ref, gf_ref, qb_ref, kb_ref, vb_ref, gb_ref, bias_ref,
                  of_ref, ob_ref, c_ref, m_ref):
    L = qf_ref.shape[0]
    H, dh = MLSTM_HEADS, MLSTM_DH

    @pl.when(pl.program_id(1) == 0)
    def _():
        c_ref[...] = jnp.zeros_like(c_ref)
        m_ref[...] = jnp.zeros_like(m_ref)

    row = lax.broadcasted_iota(jnp.int32, (L, L), 0)
    col = lax.broadcasted_iota(jnp.int32, (L, L), 1)
    ones_blk = jnp.ones((L, dh), BF16)

    for d, (q_ref, k_ref, v_ref, g_ref, o_ref) in enumerate(
            ((qf_ref, kf_ref, vf_ref, gf_ref, of_ref), (qb_ref, kb_ref, vb_ref, gb_ref, ob_ref))):
        seen = (col <= row) if d == 0 else (col >= row)
        tri = jnp.where(seen, 1.0, 0.0).astype(BF16)
        gt = g_ref[...] + bias_ref[...]
        lf = jnp.minimum(gt, 0.0) - jnp.log(1.0 + jnp.exp(-jnp.abs(gt)))
        hi, mid, lo = _split3(lf)
        bcum = _dot(tri, hi) + _dot(tri, mid) + _dot(tri, lo)
        bcum_t = bcum.T
        gt_t = gt.T
        end = L - 1 if d == 0 else 0
        for h in range(H):
            ic = 8 * d + h
            fc = 8 * d + 4 + h
            idx = d * H + h
            b_rep = jnp.broadcast_to(bcum[:, fc:fc + 1], (L, dh))
            i_rep = jnp.broadcast_to(gt[:, ic:ic + 1], (L, dh))
            b_row = bcum_t[fc:fc + 1, :]
            i_row = gt_t[ic:ic + 1, :]
            g_tot = b_rep[end:end + 1, :]
            m_prev = m_ref[idx]
            c_prev = c_ref[idx]
            q = q_ref[:, h * dh:(h + 1) * dh]
            k = k_ref[:, h * dh:(h + 1) * dh]
            v_aug = jnp.concatenate([v_ref[:, h * dh:(h + 1) * dh], ones_blk], axis=1)

            dlog = jnp.where(seen, b_rep - b_row + i_row, -jnp.inf)
            inter = b_rep + m_prev
            m_t = jnp.maximum(inter, jnp.max(dlog, axis=1, keepdims=True))
            s = (_dot_nt(q, k) * jnp.exp(dlog - m_t)).astype(BF16)
            s_inter = jnp.exp(inter - m_t)
            num_aug = jnp.concatenate([s_inter, s_inter], axis=1) * _dot(q, c_prev.astype(BF16)) + _dot(s, v_aug)
            den = jnp.maximum(jnp.abs(num_aug[:, dh:]), jnp.exp(-m_t))
            o_ref[:, h * dh:(h + 1) * dh] = num_aug[:, :dh] / den

            a_rep = g_tot - b_rep + i_rep
            m_new = jnp.maximum(g_tot + m_prev, jnp.max(a_rep, axis=0, keepdims=True))
            decay = jnp.exp(g_tot + m_prev - m_new)
            kw = (k.astype(F32) * jnp.exp(a_rep - m_new)).astype(BF16)
            c_ref[idx] = jnp.concatenate([decay, decay], axis=1) * c_prev + _dot_tn(kw, v_aug)
            m_ref[idx] = m_new


def _mlstm(qc, kc, proj, gates, gate_bias, B, T):
    M = qc.shape[0]
    L = MLSTM_CHUNK
    assert L == LANES == MLSTM_DH
    nc = T // L
    W = GROUP_W
    vblk = 2
    fwd = lambda b, i: (b * nc + i, 0)
    bwd = lambda b, i: (b * nc + nc - 1 - i, 0)
    fwd_v = lambda b, i: (b * nc + i, vblk)
    bwd_v = lambda b, i: (b * nc + nc - 1 - i, vblk)
    bias = jnp.zeros((1, LANES), F32).at[0, :4 * MLSTM_HEADS].set(gate_bias)
    return pl.pallas_call(
        _mlstm_kernel,
        grid=(B, nc),
        in_specs=[
            pl.BlockSpec((L, W), fwd), pl.BlockSpec((L, W), fwd), pl.BlockSpec((L, W), fwd_v),
            pl.BlockSpec((L, LANES), fwd),
            pl.BlockSpec((L, W), bwd), pl.BlockSpec((L, W), bwd), pl.BlockSpec((L, W), bwd_v),
            pl.BlockSpec((L, LANES), bwd),
            pl.BlockSpec((1, LANES), lambda b, i: (0, 0)),
        ],
        out_specs=[pl.BlockSpec((L, W), fwd), pl.BlockSpec((L, W), bwd)],
        out_shape=[jax.ShapeDtypeStruct((M, W), F32)] * 2,
        scratch_shapes=[
            pltpu.VMEM((2 * MLSTM_HEADS, MLSTM_DH, 2 * MLSTM_DH), F32),
            pltpu.VMEM((2 * MLSTM_HEADS, 1, LANES), F32),
        ],
        compiler_params=_cparams("parallel", "arbitrary"),
        name="mlstm",
    )(qc, kc, proj, gates, qc, kc, proj, gates, bias)


def _rope_kernel(x_ref, cos_ref, sin_ref, q_ref, k_ref):
    x = x_ref[...].astype(F32)
    reps = x.shape[1] // LANES
    cos = jnp.concatenate([cos_ref[...]] * reps, axis=1)
    sin = jnp.concatenate([sin_ref[...]] * reps, axis=1)
    lane = lax.broadcasted_iota(jnp.int32, x.shape, 1)
    width = x.shape[1]
    half = DIFF_DH // 2
    rot = jnp.where(lane % DIFF_DH < half, pltpu.roll(x, width - half, axis=1), pltpu.roll(x, half, axis=1))
    y = x * cos + rot * sin
    q_ref[...] = (y[:, :GROUP_W] * (DIFF_DH ** -0.5 * math.log2(math.e))).astype(BF16)
    k_ref[...] = y[:, GROUP_W:].astype(BF16)


def _rope_tables(T):
    half = DIFF_DH // 2
    inv = 1.0 / (ROPE_THETA ** (jnp.arange(0, DIFF_DH, 2, dtype=F32) / DIFF_DH))
    ang = jnp.arange(T, dtype=F32)[:, None] * inv[None, :]
    cos, sin = jnp.cos(ang), jnp.sin(ang)
    cos_t = jnp.concatenate([cos, cos] * (LANES // DIFF_DH), axis=1)
    sin_t = jnp.concatenate([-sin, sin] * (LANES // DIFF_DH), axis=1)
    return cos_t, sin_t


def _rope(proj, cos_t, sin_t, T, tb=256):
    M = proj.shape[0]
    nb = T // tb
    return pl.pallas_call(
        _rope_kernel,
        grid=(M // tb,),
        in_specs=[
            pl.BlockSpec((tb, 2 * GROUP_W), lambda i: (i, 2)),
            pl.BlockSpec((tb, LANES), lambda i: (i % nb, 0)),
            pl.BlockSpec((tb, LANES), lambda i: (i % nb, 0)),
        ],
        out_specs=[pl.BlockSpec((tb, GROUP_W), lambda i: (i, 0))] * 2,
        out_shape=[jax.ShapeDtypeStruct((M, GROUP_W), BF16)] * 2,
        compiler_params=_cparams("parallel"),
        name="rope",
    )(proj, cos_t, sin_t)


def _diff_kernel(q_ref, k_ref, v_ref, lp_ref, g_ref, o_ref, q2_ref, m_ref, acc_ref, *, lam_init, tks):
    ki = pl.program_id(3)
    tq = q_ref.shape[0]
    dv = v_ref.shape[1]
    ones = jnp.ones((tks, dv), BF16)

    @pl.when(ki == 0)
    def _():
        q = q_ref[...]
        lane = lax.broadcasted_iota(jnp.int32, q.shape, 1)
        zero = jnp.zeros_like(q)
        q2_ref[0:tq, :] = jnp.where(lane < DIFF_DH, q, zero)
        q2_ref[tq:2 * tq, :] = jnp.where(lane < DIFF_DH, zero, q)
        m_ref[...] = jnp.full_like(m_ref, -jnp.inf)
        acc_ref[...] = jnp.zeros_like(acc_ref)

    q2 = q2_ref[...]
    for j in range(k_ref.shape[0] // tks):
        s = _dot_nt(q2, k_ref[j * tks:(j + 1) * tks, :])
        tiles = [s[:, c * LANES:(c + 1) * LANES] for c in range(tks // LANES)]
        m_prev = m_ref[...]
        m_new = jnp.maximum(m_prev, jnp.max(functools.reduce(jnp.maximum, tiles), axis=1, keepdims=True))
        alpha = jnp.exp2(m_prev - m_new)
        p = jnp.concatenate([jnp.exp2(t - m_new).astype(BF16) for t in tiles], axis=1)
        v_aug = jnp.concatenate([v_ref[j * tks:(j + 1) * tks, :], ones], axis=1)
        acc_ref[...] = jnp.concatenate([alpha, alpha], axis=1) * acc_ref[...] + _dot(p, v_aug)
        m_ref[...] = m_new

    @pl.when(ki == pl.num_programs(3) - 1)
    def _():
        lp = lp_ref[...]
        lam = (jnp.exp(jnp.sum(lp[0:1] * lp[1:2], axis=1, keepdims=True))
               - jnp.exp(jnp.sum(lp[2:3] * lp[3:4], axis=1, keepdims=True)) + lam_init)
        o = acc_ref[:, 0:dv] / acc_ref[:, dv:2 * dv]
        out = o[0:tq] - lam * o[tq:2 * tq]
        o_ref[...] = (_rms(out, g_ref[...]) * (1.0 - lam_init)).astype(BF16)


def _diff_attn(qr, kr, proj, lam_params, norm_g, B, T, layer_idx, tq=1024, tk=4096, tks=512):
    M = qr.shape[0]
    dv = 2 * DIFF_DH
    tk = min(tk, T)
    nq, nk = T // tq, T // tk
    vblk0 = (6 * GROUP_W) // dv
    lam_init = 0.8 - 0.6 * math.exp(-0.3 * layer_idx)
    return pl.pallas_call(
        functools.partial(_diff_kernel, lam_init=lam_init, tks=tks),
        grid=(B, DIFF_HEADS, nq, nk),
        in_specs=[
            pl.BlockSpec((tq, dv), lambda b, h, qi, ki: (b * nq + qi, h)),
            pl.BlockSpec((tk, dv), lambda b, h, qi, ki: (b * nk + ki, h)),
            pl.BlockSpec((tk, dv), lambda b, h, qi, ki: (b * nk + ki, vblk0 + h)),
            pl.BlockSpec((4, DIFF_DH), lambda b, h, qi, ki: (0, 0)),
            pl.BlockSpec((1, dv), lambda b, h, qi, ki: (0, 0)),
        ],
        out_specs=pl.BlockSpec((tq, dv), lambda b, h, qi, ki: (b * nq + qi, h)),
        out_shape=jax.ShapeDtypeStruct((M, GROUP_W), BF16),
        scratch_shapes=[
            pltpu.VMEM((2 * tq, dv), BF16),
            pltpu.VMEM((2 * tq, LANES), F32),
            pltpu.VMEM((2 * tq, 2 * dv), F32),
        ],
        compiler_params=_cparams("parallel", "parallel", "parallel", "arbitrary"),
        name="diff_attn",
    )(qr, kr, proj, lam_params, norm_g.reshape(1, dv))


def _na_kernel(q_ref, k_ref, v_ref, bias_ref, o_ref, *, rows):
    g = pl.program_id(2)
    win = NA_WIN_ROWS * GRID_W
    lane = lax.broadcasted_iota(jnp.int32, (GRID_W, 2 * NA_DH), 1)
    first = lane < NA_DH
    w = 2 * NA_DH
    ones = jnp.ones((win, w), BF16)
    scores, starts = [], []
    for jr in range(NA_ROWS_PER_STEP):
        r = g * NA_ROWS_PER_STEP + jr
        rs = jnp.clip(r - NA_WIN_ROWS // 2, 0, rows - NA_WIN_ROWS)
        start = pl.multiple_of(rs * GRID_W, GRID_W)
        q = q_ref[jr * GRID_W:(jr + 1) * GRID_W, :]
        zero = jnp.zeros_like(q)
        q2 = jnp.concatenate([jnp.where(first, q, zero), jnp.where(first, zero, q)], axis=0)
        s = _dot_nt(q2, k_ref[pl.ds(start, win), :]) * (NA_DH ** -0.5 * math.log2(math.e)) + bias_ref[r - rs]
        scores.append(s)
        starts.append(start)
    for jr in range(NA_ROWS_PER_STEP):
        s = scores[jr]
        e = jnp.exp2(s - jnp.max(s, axis=1, keepdims=True)).astype(BF16)
        pv = _dot(e, jnp.concatenate([v_ref[pl.ds(starts[jr], win), :], ones], axis=1))
        pv = pv[:, :w] / pv[:, w:]
        o_ref[jr * GRID_W:(jr + 1) * GRID_W, :] = jnp.where(first, pv[:GRID_W], pv[GRID_W:]).astype(BF16)


def _na_bias_table(rpb):
    c = jnp.arange(GRID_W)
    col_start = jnp.clip(c - NA_WIN_W // 2, 0, GRID_W - NA_WIN_W)
    kc = jnp.arange(GRID_W)
    valid = (kc[None, :] >= col_start[:, None]) & (kc[None, :] < col_start[:, None] + NA_WIN_W)
    col_off = jnp.clip(kc[None, :] - c[:, None] + NA_WIN_W - 1, 0, 2 * NA_WIN_W - 2)
    p = jnp.arange(NA_WIN_ROWS)
    k = jnp.arange(NA_WIN_ROWS)
    row_off = k[None, :] - p[:, None] + NA_WIN_ROWS - 1
    t = rpb.astype(F32)[:, row_off][:, :, :, col_off] * math.log2(math.e)
    t = jnp.where(valid[None, None, None], t, MASK_VALUE)
    t = t.transpose(0, 1, 3, 2, 4).reshape(NA_HEADS, NA_WIN_ROWS, GRID_W, NA_WIN_ROWS * GRID_W)
    t = t.reshape(NA_HEADS // 2, 2, NA_WIN_ROWS, GRID_W, NA_WIN_ROWS * GRID_W).transpose(0, 2, 1, 3, 4)
    return t.reshape(NA_HEADS // 2, NA_WIN_ROWS, 2 * GRID_W, NA_WIN_ROWS * GRID_W)


def _na_attn(proj, bias_table, B, T):
    M = proj.shape[0]
    rows = T // GRID_W
    tq = NA_ROWS_PER_STEP * GRID_W
    ng = rows // NA_ROWS_PER_STEP
    hp = NA_HEADS // 2
    w = 2 * NA_DH
    qblk0, kblk0, vblk0 = (7 * GROUP_W) // w, (8 * GROUP_W) // w, (9 * GROUP_W) // w
    return pl.pallas_call(
        functools.partial(_na_kernel, rows=rows),
        grid=(B, hp, ng),
        in_specs=[
            pl.BlockSpec((tq, w), lambda b, h, g: (b * ng + g, qblk0 + h)),
            pl.BlockSpec((T, w), lambda b, h, g: (b, kblk0 + h)),
            pl.BlockSpec((T, w), lambda b, h, g: (b, vblk0 + h)),
            pl.BlockSpec((None, NA_WIN_ROWS, 2 * GRID_W, NA_WIN_ROWS * GRID_W), lambda b, h, g: (h, 0, 0, 0)),
        ],
        out_specs=pl.BlockSpec((tq, w), lambda b, h, g: (b * ng + g, h)),
        out_shape=jax.ShapeDtypeStruct((M, GROUP_W), BF16),
        compiler_params=_cparams("parallel", "parallel", "arbitrary"),
        name="na_attn",
    )(proj, proj, proj, bias_table)


def _s5_params(lam_re, lam_im, log_step, b_re, b_im, c_re, c_im, d_skip):
    L, c, P, G = S5_CHUNK, S5_GROUP_CH, S5_STATE, S5_GROUPS
    K8 = LANES // c
    O = G // K8
    lam = lax.complex(lam_re.astype(F32), lam_im.astype(F32))
    step = jnp.exp(log_step.astype(F32))[..., None]
    log_lam_bar = lam * step
    lam_bar = jnp.exp(log_lam_bar)
    b_mat = lax.complex(b_re.astype(F32), b_im.astype(F32))
    c_mat = lax.complex(c_re.astype(F32), c_im.astype(F32))
    b_bar = ((lam_bar - 1.0) / lam)[..., None] * b_mat[None]
    pw = jnp.exp(log_lam_bar[:, None] * jnp.arange(L + 1, dtype=F32)[None, :, None, None])
    cb = c_mat[None, :, :, None, :] * b_bar.transpose(0, 1, 3, 2)[:, :, None, :, :]
    s_i = jnp.arange(L)[:, None]
    t_i = jnp.arange(L)[None, :]
    lag = jnp.stack([t_i - s_i, s_i - t_i])
    lag_pw = jnp.where((lag >= 0)[..., None, None],
                       jnp.exp(log_lam_bar[:, None, None] * jnp.maximum(lag, 0).astype(F32)[..., None, None]),
                       0.0)
    m = jnp.einsum('dgoip,dstgp->gsito', jnp.concatenate([cb.real, -cb.imag], axis=-1),
                   jnp.concatenate([lag_pw.real, lag_pw.imag], axis=-1), precision=lax.Precision.HIGH)
    eye = jnp.eye(K8, dtype=F32)
    m8 = jnp.einsum('agsitc,gh->asgithc', m.reshape(O, K8, L, c, L, c), eye).reshape(O, L * LANES, L * LANES)
    ef = pw[0][L - 1 - jnp.arange(L)][:, :, :, None] * b_bar[0][None]
    eb = pw[1][jnp.arange(L)][:, :, :, None] * b_bar[1][None]
    to_rows = lambda e: e.transpose(1, 0, 3, 2)
    ef, eb = to_rows(ef), to_rows(eb)
    bs = jnp.concatenate([ef.real, ef.imag, eb.real, eb.imag], axis=-1)
    bs8 = jnp.einsum('agsik,gh->asgihk', bs.reshape(O, K8, L, c, 4 * P), eye).reshape(O, L * LANES, K8 * 4 * P)
    of = c_mat[:, None] * pw[0][1:L + 1].transpose(1, 0, 2)[:, :, None, :]
    ob = c_mat[:, None] * pw[1][L - jnp.arange(L)].transpose(1, 0, 2)[:, :, None, :]
    to_cols = lambda o: o.transpose(0, 3, 1, 2)
    of, ob = to_cols(of), to_cols(ob)
    cs = jnp.concatenate([of.real, -of.imag, ob.real, -ob.imag], axis=1)
    cs8 = jnp.einsum('agktc,gh->agkthc', cs.reshape(O, K8, 4 * P, L, c), eye).reshape(O, K8 * 4 * P, L * LANES)
    al = pw[:, L]
    a1 = jnp.concatenate([al.real, al.real], axis=-1)
    a2 = jnp.concatenate([-al.imag, al.imag], axis=-1)
    a8 = jnp.stack([a1, a2, -a2], axis=1)
    a8 = a8.reshape(2, 3, O, K8, LANES).transpose(2, 0, 1, 3, 4).reshape(O, 6 * K8, LANES)
    d8 = d_skip.astype(F32).reshape(O, 1, LANES)
    return m8.astype(BF16), bs8.astype(BF16), cs8.astype(BF16), a8, d8


def _s5_kernel(u_ref, m_ref, bs_ref, cs_ref, a_ref, d_ref, y_ref, x_ref, sf_ref, sb_ref, sfs_ref, sbs_ref):
    L = S5_CHUNK
    K8 = LANES // S5_GROUP_CH
    sw = 4 * S5_STATE
    nc = u_ref.shape[0] // L
    for s in range(L):
        x_ref[:, s * LANES:(s + 1) * LANES] = u_ref[pl.ds(s, nc, stride=L), :].astype(BF16)
    x = x_ref[...]
    for g in range(K8):
        sg = _dot(x, bs_ref[:, g * sw:(g + 1) * sw])
        sf_ref[g * nc:(g + 1) * nc, :] = sg[:, 0:LANES]
        sb_ref[g * nc:(g + 1) * nc, :] = sg[:, LANES:2 * LANES]
        sfs_ref[g * nc:(g + 1) * nc, :] = pltpu.roll(sg[:, 0:LANES], LANES // 2, axis=1)
        sbs_ref[g * nc:(g + 1) * nc, :] = pltpu.roll(sg[:, LANES:2 * LANES], LANES // 2, axis=1)
    a1f, a2f, a2fs = a_ref[0:K8], a_ref[K8:2 * K8], a_ref[2 * K8:3 * K8]
    a1b, a2b, a2bs = a_ref[3 * K8:4 * K8], a_ref[4 * K8:5 * K8], a_ref[5 * K8:6 * K8]

    def step(c, carry):
        hf, hfs, hb, hbs = carry
        cb = nc - 1 - c
        fwd = pl.ds(c, K8, stride=nc)
        bwd = pl.ds(cb, K8, stride=nc)
        sf, sfs = sf_ref[fwd, :], sfs_ref[fwd, :]
        sb, sbs = sb_ref[bwd, :], sbs_ref[bwd, :]
        sf_ref[fwd, :] = hf
        sb_ref[bwd, :] = hb
        return (a1f * hf + a2f * hfs + sf, a1f * hfs + a2fs * hf + sfs,
                a1b * hb + a2b * hbs + sb, a1b * hbs + a2bs * hb + sbs)

    z = jnp.zeros((K8, LANES), F32)
    lax.fori_loop(0, nc, step, (z, z, z, z), unroll=8)
    y = _dot(x, m_ref[...])
    for g in range(K8):
        hg = jnp.concatenate([sf_ref[g * nc:(g + 1) * nc, :], sb_ref[g * nc:(g + 1) * nc, :]], axis=1)
        y = y + _dot(hg.astype(BF16), cs_ref[g * sw:(g + 1) * sw, :])
    for t in range(L):
        y_ref[pl.ds(t, nc, stride=L), :] = y[:, t * LANES:(t + 1) * LANES]
    y_ref[...] += u_ref[...] * d_ref[...]


def _s5(su, params, B, T):
    m8, bs8, cs8, a8, d8 = params
    L = S5_CHUNK
    nc = T // L
    O = m8.shape[0]
    kw = L * LANES
    sw = 4 * S5_STATE
    K8 = LANES // S5_GROUP_CH
    one = pl.Buffered(1)
    return pl.pallas_call(
        _s5_kernel,
        grid=(O, B),
        in_specs=[
            pl.BlockSpec((T, LANES), lambda o, b: (b, o), pipeline_mode=one),
            pl.BlockSpec((None, kw, kw), lambda o, b: (o, 0, 0), pipeline_mode=one),
            pl.BlockSpec((None, kw, K8 * sw), lambda o, b: (o, 0, 0), pipeline_mode=one),
            pl.BlockSpec((None, K8 * sw, kw), lambda o, b: (o, 0, 0), pipeline_mode=one),
            pl.BlockSpec((None, 6 * K8, LANES), lambda o, b: (o, 0, 0)),
            pl.BlockSpec((None, 1, LANES), lambda o, b: (o, 0, 0)),
        ],
        out_specs=pl.BlockSpec((T, LANES), lambda o, b: (b, o)),
        out_shape=jax.ShapeDtypeStruct((B * T, GROUP_W), F32),
        scratch_shapes=[pltpu.VMEM((nc, kw), BF16)] + [pltpu.VMEM((K8 * nc, LANES), F32)] * 4,
        compiler_params=_cparams("arbitrary", "arbitrary"),
        name="s5",
    )(su, m8, bs8, cs8, a8, d8)


def _outproj_kernel(x_ref, hf_ref, hb_ref, og_ref, mg_ref, yb_ref, yc_ref, ys_ref, gw_ref, gb_ref, w_ref, o_ref):
    W = GROUP_W
    h = hf_ref[...] + hb_ref[...]
    parts = []
    for j in range(MLSTM_HEADS):
        hj = h[:, j * MLSTM_DH:(j + 1) * MLSTM_DH]
        parts.append(hj * lax.rsqrt(jnp.mean(hj * hj, axis=-1, keepdims=True) + NORM_EPS))
    hn = jnp.concatenate(parts, axis=1) * mg_ref[...]
    ya = (hn * _sigmoid(og_ref[...].astype(F32))).astype(BF16)
    y = ys_ref[...]
    z = 0.5 * y * (1.0 + jnp.tanh(math.sqrt(2.0 / math.pi) * (y + 0.044715 * (y * y * y))))
    yd = (z * _sigmoid(_dot(z.astype(BF16), gw_ref[...]) + gb_ref[...])).astype(BF16)
    acc = x_ref[...] + _dot(ya, w_ref[0:W, :])
    acc = acc + _dot(yb_ref[...], w_ref[W:2 * W, :])
    acc = acc + _dot(yc_ref[...], w_ref[2 * W:3 * W, :])
    acc = acc + _dot(yd, w_ref[3 * W:4 * W, :])
    o_ref[...] = acc


def _outproj(x, hf, hb, proj, mlstm_norm, yb, yc, ys, glu_w, glu_b, w_out, tm=512):
    M, D = x.shape
    W = GROUP_W
    tok = lambda i: (i, 0)
    const = lambda i: (0, 0)
    return pl.pallas_call(
        _outproj_kernel,
        grid=(M // tm,),
        in_specs=[
            pl.BlockSpec((tm, D), tok),
            pl.BlockSpec((tm, W), tok), pl.BlockSpec((tm, W), tok),
            pl.BlockSpec((tm, W), lambda i: (i, 3)),
            pl.BlockSpec((1, W), const),
            pl.BlockSpec((tm, W), tok), pl.BlockSpec((tm, W), tok), pl.BlockSpec((tm, W), tok),
            pl.BlockSpec((W, W), const), pl.BlockSpec((1, W), const),
            pl.BlockSpec((4 * W, D), const),
        ],
        out_specs=pl.BlockSpec((tm, D), tok),
        out_shape=jax.ShapeDtypeStruct((M, D), F32),
        compiler_params=_cparams("parallel"),
        name="outproj",
    )(x, hf, hb, proj, mlstm_norm.reshape(1, W), yb, yc, ys, glu_w, glu_b.reshape(1, W), w_out)


def _prepare_layer(w, l):
    W = GROUP_W
    ng = 4 * MLSTM_HEADS
    w_in = w['w_in'][l]
    w_main = jnp.concatenate([w_in[:, :4 * W], w_in[:, 4 * W + ng:]], axis=1).astype(BF16)
    w_gate = jnp.zeros((w_in.shape[0], LANES), BF16).at[:, :ng].set(w_in[:, 4 * W:4 * W + ng].astype(BF16))
    p = {k: w[k][l] for k in ('ffn1_norm', 'mix_norm', 'ffn2_norm', 'mlstm_conv_w', 'mlstm_conv_b', 'mlstm_gate_bias',
                              'mlstm_norm', 'diff_lambda', 'diff_norm', 's5_glu_b')}
    for k in ('ffn1_w_gate', 'ffn1_w_up', 'ffn1_w_down', 'ffn2_w_gate', 'ffn2_w_up', 'ffn2_w_down', 's5_glu_w', 'w_out'):
        p[k] = w[k][l].astype(BF16)
    p['w_main'], p['w_gate'] = w_main, w_gate
    p['na_bias'] = _na_bias_table(w['na_rpb'][l])
    p['s5'] = _s5_params(w['s5_lambda_re'][l], w['s5_lambda_im'][l], w['s5_log_step'][l], w['s5_b_re'][l],
                         w['s5_b_im'][l], w['s5_c_re'][l], w['s5_c_im'][l], w['s5_d'][l])
    return p


def _mixers(x, p, B, T, layer_idx, rope_tabs):
    proj, gates, su = _inproj(x, p['mix_norm'], p['w_main'], p['w_gate'])
    qc, kc = _mlstm_conv(proj, p['mlstm_conv_w'], p['mlstm_conv_b'], T)
    hf, hb = _mlstm(qc, kc, proj, gates, p['mlstm_gate_bias'], B, T)
    qr, kr = _rope(proj, rope_tabs[0], rope_tabs[1], T)
    yb = _diff_attn(qr, kr, proj, p['diff_lambda'], p['diff_norm'], B, T, layer_idx)
    yc = _na_attn(proj, p['na_bias'], B, T)
    ys = _s5(su, p['s5'], B, T)
    return _outproj(x, hf, hb, proj, p['mlstm_norm'], yb, yc, ys, p['s5_glu_w'], p['s5_glu_b'], p['w_out'])


def _trunk(x, layers, final_norm):
    B, T, D = x.shape
    x = x.reshape(B * T, D)
    rope_tabs = _rope_tables(T)
    for l, p in enumerate(layers):
        x = _ffn(x, p['ffn1_norm'], p['ffn1_w_gate'], p['ffn1_w_up'], p['ffn1_w_down'])
        x = _mixers(x, p, B, T, l, rope_tabs)
        last = l == len(layers) - 1
        x = _ffn(x, p['ffn2_norm'], p['ffn2_w_gate'], p['ffn2_w_up'], p['ffn2_w_down'],
                 final_g=final_norm if last else None)
    return x.reshape(B, T, D)


def kernel(x_prompt, x_sample, ffn1_norm, ffn1_w_gate, ffn1_w_up, ffn1_w_down, mix_norm, w_in, mlstm_conv_w, mlstm_conv_b, mlstm_gate_bias, mlstm_norm, diff_lambda, diff_norm, na_rpb, s5_lambda_re, s5_lambda_im, s5_log_step, s5_b_re, s5_b_im, s5_c_re, s5_c_im, s5_d, s5_glu_w, s5_glu_b, w_out, ffn2_norm, ffn2_w_gate, ffn2_w_up, ffn2_w_down, final_norm):
    w = dict(ffn1_norm=ffn1_norm, ffn1_w_gate=ffn1_w_gate, ffn1_w_up=ffn1_w_up, ffn1_w_down=ffn1_w_down,
             mix_norm=mix_norm, w_in=w_in, mlstm_conv_w=mlstm_conv_w, mlstm_conv_b=mlstm_conv_b,
             mlstm_gate_bias=mlstm_gate_bias, mlstm_norm=mlstm_norm, diff_lambda=diff_lambda, diff_norm=diff_norm,
             na_rpb=na_rpb, s5_lambda_re=s5_lambda_re, s5_lambda_im=s5_lambda_im, s5_log_step=s5_log_step,
             s5_b_re=s5_b_re, s5_b_im=s5_b_im, s5_c_re=s5_c_re, s5_c_im=s5_c_im, s5_d=s5_d,
             s5_glu_w=s5_glu_w, s5_glu_b=s5_glu_b, w_out=w_out, ffn2_norm=ffn2_norm, ffn2_w_gate=ffn2_w_gate,
             ffn2_w_up=ffn2_w_up, ffn2_w_down=ffn2_w_down)
    layers = [_prepare_layer(w, l) for l in range(ffn1_norm.shape[0])]
    return (_trunk(x_prompt, layers, final_norm), _trunk(x_sample, layers, final_norm))
```

```python
import functools
import math

import jax
import jax.numpy as jnp
from jax import lax
from jax.experimental import pallas as pl
from jax.experimental.pallas import tpu as pltpu

F32 = jnp.float32
BF16 = jnp.bfloat16

NORM_EPS = 1e-6
GROUP_W = 512
GRID_W = 64
MLSTM_HEADS = 4
MLSTM_DH = 128
MLSTM_CONV = 5
MLSTM_CHUNK = 128
DIFF_HEADS = 4
DIFF_DH = 64
ROPE_THETA = 10000.0
NA_HEADS = 8
NA_DH = 64
NA_WIN_ROWS = 8
NA_WIN_W = 16
NA_ROWS_PER_STEP = 8
S5_GROUP_CH = 16
S5_GROUPS = 32
S5_STATE = 64
S5_CHUNK = 16
LANES = 128
BF16_SUBLANES = 16
MASK_VALUE = -1e30
VMEM_LIMIT = 56 * 1024 * 1024


def _cparams(*sem):
    return pltpu.CompilerParams(dimension_semantics=sem, vmem_limit_bytes=VMEM_LIMIT)


def _rms(x, g):
    return x * lax.rsqrt(jnp.mean(x * x, axis=-1, keepdims=True) + NORM_EPS) * g


def _sigmoid(x):
    return 1.0 / (1.0 + jnp.exp(-x))


def _dot(a, b):
    return jnp.dot(a, b, preferred_element_type=F32)


def _dot_nt(a, b):
    return lax.dot_general(a, b, (((1,), (1,)), ((), ())), preferred_element_type=F32)


def _dot_tn(a, b):
    return lax.dot_general(a, b, (((0,), (0,)), ((), ())), preferred_element_type=F32)


def _split3(x):
    hi = x.astype(BF16)
    r1 = x - hi.astype(F32)
    mid = r1.astype(BF16)
    lo = (r1 - mid.astype(F32)).astype(BF16)
    return hi, mid, lo


def _ffn_kernel(x_ref, g_ref, wg_ref, wu_ref, wd_ref, *rest, final):
    if final:
        fg_ref, o_ref, h_ref = rest
    else:
        o_ref, h_ref = rest
    f = pl.program_id(1)
    nf = pl.num_programs(1)

    @pl.when(f == 0)
    def _():
        h_ref[...] = _rms(x_ref[...], g_ref[...]).astype(BF16)
        o_ref[...] = x_ref[...]

    h = h_ref[...]
    gate = _dot(h, wg_ref[...])
    up = _dot(h, wu_ref[...])
    act = (0.5 * gate * _sigmoid(gate) * up).astype(BF16)
    nchunk = 4
    cw = o_ref.shape[1] // nchunk
    for c in range(nchunk):
        o_ref[:, c * cw:(c + 1) * cw] += _dot(act, wd_ref[:, c * cw:(c + 1) * cw])

    if final:
        @pl.when(f == nf - 1)
        def _():
            o_ref[...] = _rms(o_ref[...], fg_ref[...])


def _ffn(x, g, wg, wu, wd, final_g=None, tm=1024, tf=512):
    M, D = x.shape
    F = wg.shape[1]
    final = final_g is not None
    in_specs = [
        pl.BlockSpec((tm, D), lambda i, f: (i, 0), pipeline_mode=pl.Buffered(1)),
        pl.BlockSpec((1, D), lambda i, f: (0, 0)),
        pl.BlockSpec((D, tf), lambda i, f: (0, f)),
        pl.BlockSpec((D, tf), lambda i, f: (0, f)),
        pl.BlockSpec((tf, D), lambda i, f: (f, 0)),
    ]
    args = [x, g.reshape(1, D), wg, wu, wd]
    if final:
        in_specs.append(pl.BlockSpec((1, D), lambda i, f: (0, 0)))
        args.append(final_g.reshape(1, D))
    return pl.pallas_call(
        functools.partial(_ffn_kernel, final=final),
        grid=(M // tm, F // tf),
        in_specs=in_specs,
        out_specs=pl.BlockSpec((tm, D), lambda i, f: (i, 0)),
        out_shape=jax.ShapeDtypeStruct((M, D), F32),
        scratch_shapes=[pltpu.VMEM((tm, D), BF16)],
        compiler_params=_cparams("parallel", "arbitrary"),
        name="ffn",
    )(*args)


def _inproj_kernel(x_ref, g_ref, w_ref, wgate_ref, o_ref, og_ref, os_ref, h_ref):
    n = pl.program_id(1)

    @pl.when(n == 0)
    def _():
        h = _rms(x_ref[...], g_ref[...]).astype(BF16)
        h_ref[...] = h
        og_ref[...] = _dot(h, wgate_ref[...])

    res = _dot(h_ref[...], w_ref[...])
    o_ref[...] = res.astype(BF16)

    @pl.when(n == pl.num_programs(1) - 1)
    def _():
        os_ref[...] = res[:, res.shape[1] - GROUP_W:]


def _inproj(x, g, w_main, w_gate, tm=1024, tn=1408):
    M, D = x.shape
    N = w_main.shape[1]
    assert tn >= GROUP_W and N % tn == 0
    return pl.pallas_call(
        _inproj_kernel,
        grid=(M // tm, N // tn),
        in_specs=[
            pl.BlockSpec((tm, D), lambda i, n: (i, 0)),
            pl.BlockSpec((1, D), lambda i, n: (0, 0)),
            pl.BlockSpec((D, tn), lambda i, n: (0, n)),
            pl.BlockSpec((D, LANES), lambda i, n: (0, 0)),
        ],
        out_specs=[
            pl.BlockSpec((tm, tn), lambda i, n: (i, n)),
            pl.BlockSpec((tm, LANES), lambda i, n: (i, 0)),
            pl.BlockSpec((tm, GROUP_W), lambda i, n: (i, 0)),
        ],
        out_shape=[
            jax.ShapeDtypeStruct((M, N), BF16),
            jax.ShapeDtypeStruct((M, LANES), F32),
            jax.ShapeDtypeStruct((M, GROUP_W), F32),
        ],
        scratch_shapes=[pltpu.VMEM((tm, D), BF16)],
        compiler_params=_cparams("parallel", "arbitrary"),
        name="inproj",
    )(x, g.reshape(1, D), w_main, w_gate)


def _conv_kernel(x_ref, prev_ref, next_ref, w_ref, b_ref, q_ref, k_ref, *, blocks_per_seq):
    tb = x_ref.shape[0]
    halo = prev_ref.shape[0]
    pos = pl.program_id(0) % blocks_per_seq
    prev = jnp.where(pos != 0, prev_ref[...].astype(F32), 0.0)
    nxt = jnp.where(pos != blocks_per_seq - 1, next_ref[...].astype(F32), 0.0)
    ext = jnp.concatenate([prev, x_ref[...].astype(F32), nxt], axis=0)
    rows = tb + 2 * halo
    pad = MLSTM_CONV // 2
    acc = None
    for j in range(MLSTM_CONV):
        shift = (pad - j) % rows
        sh = ext if shift == 0 else pltpu.roll(ext, shift, axis=0)
        term = sh[halo:halo + tb] * w_ref[j:j + 1, :]
        acc = term if acc is None else acc + term
    acc = acc + b_ref[...]
    y = acc * _sigmoid(acc)
    q_ref[...] = y[:, :GROUP_W].astype(BF16)
    k_ref[...] = (y[:, GROUP_W:] * (MLSTM_DH ** -0.5)).astype(BF16)


def _mlstm_conv(proj, conv_w, conv_b, T, tb=512):
    M = proj.shape[0]
    C = 2 * GROUP_W
    halo = BF16_SUBLANES
    r = tb // halo
    nhalo = M // halo
    w = jnp.zeros((8, C), F32).at[:MLSTM_CONV].set(conv_w)
    return pl.pallas_call(
        functools.partial(_conv_kernel, blocks_per_seq=T // tb),
        grid=(M // tb,),
        in_specs=[
            pl.BlockSpec((tb, C), lambda i: (i, 0)),
            pl.BlockSpec((halo, C), lambda i: (jnp.maximum(i * r - 1, 0), 0)),
            pl.BlockSpec((halo, C), lambda i: (jnp.minimum((i + 1) * r, nhalo - 1), 0)),
            pl.BlockSpec((8, C), lambda i: (0, 0)),
            pl.BlockSpec((1, C), lambda i: (0, 0)),
        ],
        out_specs=[
            pl.BlockSpec((tb, GROUP_W), lambda i: (i, 0)),
            pl.BlockSpec((tb, GROUP_W), lambda i: (i, 0)),
        ],
        out_shape=[jax.ShapeDtypeStruct((M, GROUP_W), BF16)] * 2,
        compiler_params=_cparams("parallel"),
        name="mlstm_conv",
    )(proj, proj, proj, w, conv_b.reshape(1, C))


def _mlstm_kernel(qf_ref, kf_ref, vf_ref, gf_ref, qb_ref, kb_ref, vb_ref, gb_ref, bias_ref,
                  of_ref, ob_ref, c_ref, m_ref):
    L = qf_ref.shape[0]
    H, dh = MLSTM_HEADS, MLSTM_DH

    @pl.when(pl.program_id(1) == 0)
    def _():
        c_ref[...] = jnp.zeros_like(c_ref)
        m_ref[...] = jnp.zeros_like(m_ref)

    row = lax.broadcasted_iota(jnp.int32, (L, L), 0)
    col = lax.broadcasted_iota(jnp.int32, (L, L), 1)
    ones_blk = jnp.ones((L, dh), BF16)

    dirs = ((qf_ref, kf_ref, vf_ref, gf_ref, of_ref), (qb_ref, kb_ref, vb_ref, gb_ref, ob_ref))
    gate_terms = []
    for d, (_, _, _, g_ref, _) in enumerate(dirs):
        seen = (col <= row) if d == 0 else (col >= row)
        tri = jnp.where(seen, 1.0, 0.0).astype(BF16)
        gt = g_ref[...] + bias_ref[...]
        lf = jnp.minimum(gt, 0.0) - jnp.log(1.0 + jnp.exp(-jnp.abs(gt)))
        hi, mid, lo = _split3(lf)
        bcum = _dot(tri, hi) + _dot(tri, mid) + _dot(tri, lo)
        gate_terms.append((seen, gt, bcum, bcum.T, gt.T))

    for d, (q_ref, k_ref, v_ref, _, o_ref) in enumerate(dirs):
        seen, gt, bcum, bcum_t, gt_t = gate_terms[d]
        end = L - 1 if d == 0 else 0
        for h in range(H):
            ic = 8 * d + h
            fc = 8 * d + 4 + h
            idx = d * H + h
            b_rep = jnp.broadcast_to(bcum[:, fc:fc + 1], (L, dh))
            i_rep = jnp.broadcast_to(gt[:, ic:ic + 1], (L, dh))
            b_row = bcum_t[fc:fc + 1, :]
            i_row = gt_t[ic:ic + 1, :]
            g_tot = b_rep[end:end + 1, :]
            m_prev = m_ref[idx]
            c_prev = c_ref[idx]
            q = q_ref[:, h * dh:(h + 1) * dh]
            k = k_ref[:, h * dh:(h + 1) * dh]
            v_aug = jnp.concatenate([v_ref[:, h * dh:(h + 1) * dh], ones_blk], axis=1)

            dlog = jnp.where(seen, b_rep - b_row + i_row, -jnp.inf)
            inter = b_rep + m_prev
            m_t = jnp.maximum(inter, jnp.max(dlog, axis=1, keepdims=True))
            s = (_dot_nt(q, k) * jnp.exp(dlog - m_t)).astype(BF16)
            s_inter = jnp.exp(inter - m_t)
            num_aug = jnp.concatenate([s_inter, s_inter], axis=1) * _dot(q, c_prev.astype(BF16)) + _dot(s, v_aug)
            den = jnp.maximum(jnp.abs(num_aug[:, dh:]), jnp.exp(-m_t))
            o_ref[:, h * dh:(h + 1) * dh] = num_aug[:, :dh] / den

            a_rep = g_tot - b_rep + i_rep
            m_new = jnp.maximum(g_tot + m_prev, jnp.max(a_rep, axis=0, keepdims=True))
            decay = jnp.exp(g_tot + m_prev - m_new)
            kw = (k.astype(F32) * jnp.exp(a_rep - m_new)).astype(BF16)
            c_ref[idx] = jnp.concatenate([decay, decay], axis=1) * c_prev + _dot_tn(kw, v_aug)
            m_ref[idx] = m_new


def _mlstm(qc, kc, proj, gates, gate_bias, B, T):
    M = qc.shape[0]
    L = MLSTM_CHUNK
    assert L == LANES == MLSTM_DH
    nc = T // L
    W = GROUP_W
    vblk = 2
    fwd = lambda b, i: (b * nc + i, 0)
    bwd = lambda b, i: (b * nc + nc - 1 - i, 0)
    fwd_v = lambda b, i: (b * nc + i, vblk)
    bwd_v = lambda b, i: (b * nc + nc - 1 - i, vblk)
    bias = jnp.zeros((1, LANES), F32).at[0, :4 * MLSTM_HEADS].set(gate_bias)
    return pl.pallas_call(
        _mlstm_kernel,
        grid=(B, nc),
        in_specs=[
            pl.BlockSpec((L, W), fwd), pl.BlockSpec((L, W), fwd), pl.BlockSpec((L, W), fwd_v),
            pl.BlockSpec((L, LANES), fwd),
            pl.BlockSpec((L, W), bwd), pl.BlockSpec((L, W), bwd), pl.BlockSpec((L, W), bwd_v),
            pl.BlockSpec((L, LANES), bwd),
            pl.BlockSpec((1, LANES), lambda b, i: (0, 0)),
        ],
        out_specs=[pl.BlockSpec((L, W), fwd), pl.BlockSpec((L, W), bwd)],
        out_shape=[jax.ShapeDtypeStruct((M, W), F32)] * 2,
        scratch_shapes=[
            pltpu.VMEM((2 * MLSTM_HEADS, MLSTM_DH, 2 * MLSTM_DH), F32),
            pltpu.VMEM((2 * MLSTM_HEADS, 1, LANES), F32),
        ],
        compiler_params=_cparams("parallel", "arbitrary"),
        name="mlstm",
    )(qc, kc, proj, gates, qc, kc, proj, gates, bias)


def _rope_kernel(x_ref, cos_ref, sin_ref, q_ref, k_ref):
    x = x_ref[...].astype(F32)
    reps = x.shape[1] // LANES
    cos = jnp.concatenate([cos_ref[...]] * reps, axis=1)
    sin = jnp.concatenate([sin_ref[...]] * reps, axis=1)
    lane = lax.broadcasted_iota(jnp.int32, x.shape, 1)
    width = x.shape[1]
    half = DIFF_DH // 2
    rot = jnp.where(lane % DIFF_DH < half, pltpu.roll(x, width - half, axis=1), pltpu.roll(x, half, axis=1))
    y = x * cos + rot * sin
    q_ref[...] = (y[:, :GROUP_W] * (DIFF_DH ** -0.5 * math.log2(math.e))).astype(BF16)
    k_ref[...] = y[:, GROUP_W:].astype(BF16)


def _rope_tables(T):
    half = DIFF_DH // 2
    inv = 1.0 / (ROPE_THETA ** (jnp.arange(0, DIFF_DH, 2, dtype=F32) / DIFF_DH))
    ang = jnp.arange(T, dtype=F32)[:, None] * inv[None, :]
    cos, sin = jnp.cos(ang), jnp.sin(ang)
    cos_t = jnp.concatenate([cos, cos] * (LANES // DIFF_DH), axis=1)
    sin_t = jnp.concatenate([-sin, sin] * (LANES // DIFF_DH), axis=1)
    return cos_t, sin_t


def _rope(proj, cos_t, sin_t, T, tb=256):
    M = proj.shape[0]
    nb = T // tb
    return pl.pallas_call(
        _rope_kernel,
        grid=(M // tb,),
        in_specs=[
            pl.BlockSpec((tb, 2 * GROUP_W), lambda i: (i, 2)),
            pl.BlockSpec((tb, LANES), lambda i: (i % nb, 0)),
            pl.BlockSpec((tb, LANES), lambda i: (i % nb, 0)),
        ],
        out_specs=[pl.BlockSpec((tb, GROUP_W), lambda i: (i, 0))] * 2,
        out_shape=[jax.ShapeDtypeStruct((M, GROUP_W), BF16)] * 2,
        compiler_params=_cparams("parallel"),
        name="rope",
    )(proj, cos_t, sin_t)


def _diff_kernel(q_ref, k_ref, v_ref, lp_ref, g_ref, o_ref, q2_ref, m_ref, acc_ref, *, lam_init, tks):
    ki = pl.program_id(3)
    tq = q_ref.shape[0]
    dv = v_ref.shape[1]
    ones = jnp.ones((tks, dv), BF16)

    @pl.when(ki == 0)
    def _():
        q = q_ref[...]
        lane = lax.broadcasted_iota(jnp.int32, q.shape, 1)
        zero = jnp.zeros_like(q)
        q2_ref[0:tq, :] = jnp.where(lane < DIFF_DH, q, zero)
        q2_ref[tq:2 * tq, :] = jnp.where(lane < DIFF_DH, zero, q)
        m_ref[...] = jnp.full_like(m_ref, -jnp.inf)
        acc_ref[...] = jnp.zeros_like(acc_ref)

    q2 = q2_ref[...]
    for j in range(k_ref.shape[0] // tks):
        s = _dot_nt(q2, k_ref[j * tks:(j + 1) * tks, :])
        tiles = [s[:, c * LANES:(c + 1) * LANES] for c in range(tks // LANES)]
        m_prev = m_ref[...]
        m_new = jnp.maximum(m_prev, jnp.max(functools.reduce(jnp.maximum, tiles), axis=1, keepdims=True))
        alpha = jnp.exp2(m_prev - m_new)
        p = jnp.concatenate([jnp.exp2(t - m_new).astype(BF16) for t in tiles], axis=1)
        v_aug = jnp.concatenate([v_ref[j * tks:(j + 1) * tks, :], ones], axis=1)
        acc_ref[...] = jnp.concatenate([alpha, alpha], axis=1) * acc_ref[...] + _dot(p, v_aug)
        m_ref[...] = m_new

    @pl.when(ki == pl.num_programs(3) - 1)
    def _():
        lp = lp_ref[...]
        lam = (jnp.exp(jnp.sum(lp[0:1] * lp[1:2], axis=1, keepdims=True))
               - jnp.exp(jnp.sum(lp[2:3] * lp[3:4], axis=1, keepdims=True)) + lam_init)
        o = acc_ref[:, 0:dv] / acc_ref[:, dv:2 * dv]
        out = o[0:tq] - lam * o[tq:2 * tq]
        o_ref[...] = (_rms(out, g_ref[...]) * (1.0 - lam_init)).astype(BF16)


def _diff_attn(qr, kr, proj, lam_params, norm_g, B, T, layer_idx, tq=1024, tk=4096, tks=512):
    M = qr.shape[0]
    dv = 2 * DIFF_DH
    tk = min(tk, T)
    nq, nk = T // tq, T // tk
    vblk0 = (6 * GROUP_W) // dv
    lam_init = 0.8 - 0.6 * math.exp(-0.3 * layer_idx)
    return pl.pallas_call(
        functools.partial(_diff_kernel, lam_init=lam_init, tks=tks),
        grid=(B, DIFF_HEADS, nq, nk),
        in_specs=[
            pl.BlockSpec((tq, dv), lambda b, h, qi, ki: (b * nq + qi, h)),
            pl.BlockSpec((tk, dv), lambda b, h, qi, ki: (b * nk + ki, h)),
            pl.BlockSpec((tk, dv), lambda b, h, qi, ki: (b * nk + ki, vblk0 + h)),
            pl.BlockSpec((4, DIFF_DH), lambda b, h, qi, ki: (0, 0)),
            pl.BlockSpec((1, dv), lambda b, h, qi, ki: (0, 0)),
        ],
        out_specs=pl.BlockSpec((tq, dv), lambda b, h, qi, ki: (b * nq + qi, h)),
        out_shape=jax.ShapeDtypeStruct((M, GROUP_W), BF16),
        scratch_shapes=[
            pltpu.VMEM((2 * tq, dv), BF16),
            pltpu.VMEM((2 * tq, LANES), F32),
            pltpu.VMEM((2 * tq, 2 * dv), F32),
        ],
        compiler_params=_cparams("parallel", "parallel", "parallel", "arbitrary"),
        name="diff_attn",
    )(qr, kr, proj, lam_params, norm_g.reshape(1, dv))


def _na_kernel(q_ref, k_ref, v_ref, bias_ref, o_ref, *, rows):
    g = pl.program_id(2)
    win = NA_WIN_ROWS * GRID_W
    lane = lax.broadcasted_iota(jnp.int32, (GRID_W, 2 * NA_DH), 1)
    first = lane < NA_DH
    w = 2 * NA_DH
    ones = jnp.ones((win, w), BF16)
    scores, starts = [], []
    for jr in range(NA_ROWS_PER_STEP):
        r = g * NA_ROWS_PER_STEP + jr
        rs = jnp.clip(r - NA_WIN_ROWS // 2, 0, rows - NA_WIN_ROWS)
        start = pl.multiple_of(rs * GRID_W, GRID_W)
        q = q_ref[jr * GRID_W:(jr + 1) * GRID_W, :]
        zero = jnp.zeros_like(q)
        q2 = jnp.concatenate([jnp.where(first, q, zero), jnp.where(first, zero, q)], axis=0)
        s = _dot_nt(q2, k_ref[pl.ds(start, win), :]) * (NA_DH ** -0.5 * math.log2(math.e)) + bias_ref[r - rs]
        scores.append(s)
        starts.append(start)
    for jr in range(NA_ROWS_PER_STEP):
        s = scores[jr]
        e = jnp.exp2(s - jnp.max(s, axis=1, keepdims=True)).astype(BF16)
        pv = _dot(e, jnp.concatenate([v_ref[pl.ds(starts[jr], win), :], ones], axis=1))
        pv = pv[:, :w] / pv[:, w:]
        o_ref[jr * GRID_W:(jr + 1) * GRID_W, :] = jnp.where(first, pv[:GRID_W], pv[GRID_W:]).astype(BF16)


def _na_bias_table(rpb):
    c = jnp.arange(GRID_W)
    col_start = jnp.clip(c - NA_WIN_W // 2, 0, GRID_W - NA_WIN_W)
    kc = jnp.arange(GRID_W)
    valid = (kc[None, :] >= col_start[:, None]) & (kc[None, :] < col_start[:, None] + NA_WIN_W)
    col_off = jnp.clip(kc[None, :] - c[:, None] + NA_WIN_W - 1, 0, 2 * NA_WIN_W - 2)
    p = jnp.arange(NA_WIN_ROWS)
    k = jnp.arange(NA_WIN_ROWS)
    row_off = k[None, :] - p[:, None] + NA_WIN_ROWS - 1
    t = rpb.astype(F32)[:, row_off][:, :, :, col_off] * math.log2(math.e)
    t = jnp.where(valid[None, None, None], t, MASK_VALUE)
    t = t.transpose(0, 1, 3, 2, 4).reshape(NA_HEADS, NA_WIN_ROWS, GRID_W, NA_WIN_ROWS * GRID_W)
    t = t.reshape(NA_HEADS // 2, 2, NA_WIN_ROWS, GRID_W, NA_WIN_ROWS * GRID_W).transpose(0, 2, 1, 3, 4)
    return t.reshape(NA_HEADS // 2, NA_WIN_ROWS, 2 * GRID_W, NA_WIN_ROWS * GRID_W)


def _na_attn(proj, bias_table, B, T):
    M = proj.shape[0]
    rows = T // GRID_W
    tq = NA_ROWS_PER_STEP * GRID_W
    ng = rows // NA_ROWS_PER_STEP
    hp = NA_HEADS // 2
    w = 2 * NA_DH
    qblk0, kblk0, vblk0 = (7 * GROUP_W) // w, (8 * GROUP_W) // w, (9 * GROUP_W) // w
    return pl.pallas_call(
        functools.partial(_na_kernel, rows=rows),
        grid=(B, hp, ng),
        in_specs=[
            pl.BlockSpec((tq, w), lambda b, h, g: (b * ng + g, qblk0 + h)),
            pl.BlockSpec((T, w), lambda b, h, g: (b, kblk0 + h)),
            pl.BlockSpec((T, w), lambda b, h, g: (b, vblk0 + h)),
            pl.BlockSpec((None, NA_WIN_ROWS, 2 * GRID_W, NA_WIN_ROWS * GRID_W), lambda b, h, g: (h, 0, 0, 0)),
        ],
        out_specs=pl.BlockSpec((tq, w), lambda b, h, g: (b * ng + g, h)),
        out_shape=jax.ShapeDtypeStruct((M, GROUP_W), BF16),
        compiler_params=_cparams("parallel", "parallel", "arbitrary"),
        name="na_attn",
    )(proj, proj, proj, bias_table)


def _s5_params(lam_re, lam_im, log_step, b_re, b_im, c_re, c_im, d_skip):
    L, c, P, G = S5_CHUNK, S5_GROUP_CH, S5_STATE, S5_GROUPS
    K8 = LANES // c
    O = G // K8
    lam = lax.complex(lam_re.astype(F32), lam_im.astype(F32))
    step = jnp.exp(log_step.astype(F32))[..., None]
    log_lam_bar = lam * step
    lam_bar = jnp.exp(log_lam_bar)
    b_mat = lax.complex(b_re.astype(F32), b_im.astype(F32))
    c_mat = lax.complex(c_re.astype(F32), c_im.astype(F32))
    b_bar = ((lam_bar - 1.0) / lam)[..., None] * b_mat[None]
    pw = jnp.exp(log_lam_bar[:, None] * jnp.arange(L + 1, dtype=F32)[None, :, None, None])
    cb = c_mat[None, :, :, None, :] * b_bar.transpose(0, 1, 3, 2)[:, :, None, :, :]
    s_i = jnp.arange(L)[:, None]
    t_i = jnp.arange(L)[None, :]
    lag = jnp.stack([t_i - s_i, s_i - t_i])
    lag_pw = jnp.where((lag >= 0)[..., None, None],
                       jnp.exp(log_lam_bar[:, None, None] * jnp.maximum(lag, 0).astype(F32)[..., None, None]),
                       0.0)
    m = jnp.einsum('dgoip,dstgp->gsito', jnp.concatenate([cb.real, -cb.imag], axis=-1),
                   jnp.concatenate([lag_pw.real, lag_pw.imag], axis=-1), precision=lax.Precision.HIGH)
    eye = jnp.eye(K8, dtype=F32)
    lane_of = jnp.arange(K8)[:, None] * c + jnp.arange(c)[None, :]
    sel = (jnp.arange(LANES)[None, None, :] == lane_of[:, :, None]).astype(BF16)
    m8 = jnp.einsum('agsitc,gcl->asgitl', m.astype(BF16).reshape(O, K8, L, c, L, c), sel,
                    preferred_element_type=BF16).reshape(O, L * LANES, L * LANES)
    ef = pw[0][L - 1 - jnp.arange(L)][:, :, :, None] * b_bar[0][None]
    eb = pw[1][jnp.arange(L)][:, :, :, None] * b_bar[1][None]
    to_rows = lambda e: e.transpose(1, 0, 3, 2)
    ef, eb = to_rows(ef), to_rows(eb)
    bs = jnp.concatenate([ef.real, ef.imag, eb.real, eb.imag], axis=-1)
    bs8 = jnp.einsum('agsik,gh->asgihk', bs.reshape(O, K8, L, c, 4 * P), eye).reshape(O, L * LANES, K8 * 4 * P)
    of = c_mat[:, None] * pw[0][1:L + 1].transpose(1, 0, 2)[:, :, None, :]
    ob = c_mat[:, None] * pw[1][L - jnp.arange(L)].transpose(1, 0, 2)[:, :, None, :]
    to_cols = lambda o: o.transpose(0, 3, 1, 2)
    of, ob = to_cols(of), to_cols(ob)
    cs = jnp.concatenate([of.real, -of.imag, ob.real, -ob.imag], axis=1)
    cs8 = jnp.einsum('agktc,gh->agkthc', cs.reshape(O, K8, 4 * P, L, c), eye).reshape(O, K8 * 4 * P, L * LANES)
    al = pw[:, L]
    a1 = jnp.concatenate([al.real, al.real], axis=-1)
    a2 = jnp.concatenate([-al.imag, al.imag], axis=-1)
    a8 = jnp.stack([a1, a2, -a2], axis=1)
    a8 = a8.reshape(2, 3, O, K8, LANES).transpose(2, 0, 1, 3, 4).reshape(O, 6 * K8, LANES)
    d8 = d_skip.astype(F32).reshape(O, 1, LANES)
    return m8.astype(BF16), bs8.astype(BF16), cs8.astype(BF16), a8, d8


def _s5_kernel(u_ref, m_ref, bs_ref, cs_ref, a_ref, d_ref, y_ref, x_ref, sf_ref, sb_ref, sfs_ref, sbs_ref):
    L = S5_CHUNK
    K8 = LANES // S5_GROUP_CH
    sw = 4 * S5_STATE
    nc = u_ref.shape[0] // L
    for s in range(L):
        x_ref[:, s * LANES:(s + 1) * LANES] = u_ref[pl.ds(s, nc, stride=L), :].astype(BF16)
    x = x_ref[...]
    for g in range(K8):
        sg = _dot(x, bs_ref[:, g * sw:(g + 1) * sw])
        sf_ref[g * nc:(g + 1) * nc, :] = sg[:, 0:LANES]
        sb_ref[g * nc:(g + 1) * nc, :] = sg[:, LANES:2 * LANES]
        sfs_ref[g * nc:(g + 1) * nc, :] = pltpu.roll(sg[:, 0:LANES], LANES // 2, axis=1)
        sbs_ref[g * nc:(g + 1) * nc, :] = pltpu.roll(sg[:, LANES:2 * LANES], LANES // 2, axis=1)
    a1f, a2f, a2fs = a_ref[0:K8], a_ref[K8:2 * K8], a_ref[2 * K8:3 * K8]
    a1b, a2b, a2bs = a_ref[3 * K8:4 * K8], a_ref[4 * K8:5 * K8], a_ref[5 * K8:6 * K8]

    def step(c, carry):
        hf, hfs, hb, hbs = carry
        cb = nc - 1 - c
        fwd = pl.ds(c, K8, stride=nc)
        bwd = pl.ds(cb, K8, stride=nc)
        sf, sfs = sf_ref[fwd, :], sfs_ref[fwd, :]
        sb, sbs = sb_ref[bwd, :], sbs_ref[bwd, :]
        sf_ref[fwd, :] = hf
        sb_ref[bwd, :] = hb
        return (a1f * hf + a2f * hfs + sf, a1f * hfs + a2fs * hf + sfs,
                a1b * hb + a2b * hbs + sb, a1b * hbs + a2bs * hb + sbs)

    z = jnp.zeros((K8, LANES), F32)
    lax.fori_loop(0, nc, step, (z, z, z, z), unroll=8)
    y = _dot(x, m_ref[...])
    for g in range(K8):
        hg = jnp.concatenate([sf_ref[g * nc:(g + 1) * nc, :], sb_ref[g * nc:(g + 1) * nc, :]], axis=1)
        y = y + _dot(hg.astype(BF16), cs_ref[g * sw:(g + 1) * sw, :])
    for t in range(L):
        y_ref[pl.ds(t, nc, stride=L), :] = y[:, t * LANES:(t + 1) * LANES]
    y_ref[...] += u_ref[...] * d_ref[...]


def _s5(su, params, B, T):
    m8, bs8, cs8, a8, d8 = params
    L = S5_CHUNK
    nc = T // L
    O = m8.shape[0]
    kw = L * LANES
    sw = 4 * S5_STATE
    K8 = LANES // S5_GROUP_CH
    one = pl.Buffered(1)
    return pl.pallas_call(
        _s5_kernel,
        grid=(O, B),
        in_specs=[
            pl.BlockSpec((T, LANES), lambda o, b: (b, o), pipeline_mode=one),
            pl.BlockSpec((None, kw, kw), lambda o, b: (o, 0, 0), pipeline_mode=one),
            pl.BlockSpec((None, kw, K8 * sw), lambda o, b: (o, 0, 0), pipeline_mode=one),
            pl.BlockSpec((None, K8 * sw, kw), lambda o, b: (o, 0, 0), pipeline_mode=one),
            pl.BlockSpec((None, 6 * K8, LANES), lambda o, b: (o, 0, 0)),
            pl.BlockSpec((None, 1, LANES), lambda o, b: (o, 0, 0)),
        ],
        out_specs=pl.BlockSpec((T, LANES), lambda o, b: (b, o)),
        out_shape=jax.ShapeDtypeStruct((B * T, GROUP_W), F32),
        scratch_shapes=[pltpu.VMEM((nc, kw), BF16)] + [pltpu.VMEM((K8 * nc, LANES), F32)] * 4,
        compiler_params=_cparams("arbitrary", "arbitrary"),
        name="s5",
    )(su, m8, bs8, cs8, a8, d8)


def _outproj_kernel(x_ref, hf_ref, hb_ref, og_ref, mg_ref, yb_ref, yc_ref, ys_ref, gw_ref, gb_ref, w_ref, o_ref):
    W = GROUP_W
    h = hf_ref[...] + hb_ref[...]
    parts = []
    for j in range(MLSTM_HEADS):
        hj = h[:, j * MLSTM_DH:(j + 1) * MLSTM_DH]
        parts.append(hj * lax.rsqrt(jnp.mean(hj * hj, axis=-1, keepdims=True) + NORM_EPS))
    hn = jnp.concatenate(parts, axis=1) * mg_ref[...]
    ya = (hn * _sigmoid(og_ref[...].astype(F32))).astype(BF16)
    y = ys_ref[...]
    z = 0.5 * y * (1.0 + jnp.tanh(math.sqrt(2.0 / math.pi) * (y + 0.044715 * (y * y * y))))
    yd = (z * _sigmoid(_dot(z.astype(BF16), gw_ref[...]) + gb_ref[...])).astype(BF16)
    acc = x_ref[...] + _dot(ya, w_ref[0:W, :])
    acc = acc + _dot(yb_ref[...], w_ref[W:2 * W, :])
    acc = acc + _dot(yc_ref[...], w_ref[2 * W:3 * W, :])
    acc = acc + _dot(yd, w_ref[3 * W:4 * W, :])
    o_ref[...] = acc


def _outproj(x, hf, hb, proj, mlstm_norm, yb, yc, ys, glu_w, glu_b, w_out, tm=512):
    M, D = x.shape
    W = GROUP_W
    tok = lambda i: (i, 0)
    const = lambda i: (0, 0)
    return pl.pallas_call(
        _outproj_kernel,
        grid=(M // tm,),
        in_specs=[
            pl.BlockSpec((tm, D), tok),
            pl.BlockSpec((tm, W), tok), pl.BlockSpec((tm, W), tok),
            pl.BlockSpec((tm, W), lambda i: (i, 3)),
            pl.BlockSpec((1, W), const),
            pl.BlockSpec((tm, W), tok), pl.BlockSpec((tm, W), tok), pl.BlockSpec((tm, W), tok),
            pl.BlockSpec((W, W), const), pl.BlockSpec((1, W), const),
            pl.BlockSpec((4 * W, D), const),
        ],
        out_specs=pl.BlockSpec((tm, D), tok),
        out_shape=jax.ShapeDtypeStruct((M, D), F32),
        compiler_params=_cparams("parallel"),
        name="outproj",
    )(x, hf, hb, proj, mlstm_norm.reshape(1, W), yb, yc, ys, glu_w, glu_b.reshape(1, W), w_out)


def _prepare_layer(w, l):
    W = GROUP_W
    ng = 4 * MLSTM_HEADS
    w_in = w['w_in'][l]
    w_main = jnp.concatenate([w_in[:, :4 * W], w_in[:, 4 * W + ng:]], axis=1).astype(BF16)
    w_gate = jnp.zeros((w_in.shape[0], LANES), BF16).at[:, :ng].set(w_in[:, 4 * W:4 * W + ng].astype(BF16))
    p = {k: w[k][l] for k in ('ffn1_norm', 'mix_norm', 'ffn2_norm', 'mlstm_conv_w', 'mlstm_conv_b', 'mlstm_gate_bias',
                              'mlstm_norm', 'diff_lambda', 'diff_norm', 's5_glu_b')}
    for k in ('ffn1_w_gate', 'ffn1_w_up', 'ffn1_w_down', 'ffn2_w_gate', 'ffn2_w_up', 'ffn2_w_down', 's5_glu_w', 'w_out'):
        p[k] = w[k][l].astype(BF16)
    p['w_main'], p['w_gate'] = w_main, w_gate
    p['na_bias'] = _na_bias_table(w['na_rpb'][l])
    p['s5'] = _s5_params(w['s5_lambda_re'][l], w['s5_lambda_im'][l], w['s5_log_step'][l], w['s5_b_re'][l],
                         w['s5_b_im'][l], w['s5_c_re'][l], w['s5_c_im'][l], w['s5_d'][l])
    return p


def _mixers(x, p, B, T, layer_idx, rope_tabs):
    proj, gates, su = _inproj(x, p['mix_norm'], p['w_main'], p['w_gate'])
    qc, kc = _mlstm_conv(proj, p['mlstm_conv_w'], p['mlstm_conv_b'], T)
    hf, hb = _mlstm(qc, kc, proj, gates, p['mlstm_gate_bias'], B, T)
    qr, kr = _rope(proj, rope_tabs[0], rope_tabs[1], T)
    yb = _diff_attn(qr, kr, proj, p['diff_lambda'], p['diff_norm'], B, T, layer_idx)
    yc = _na_attn(proj, p['na_bias'], B, T)
    ys = _s5(su, p['s5'], B, T)
    return _outproj(x, hf, hb, proj, p['mlstm_norm'], yb, yc, ys, p['s5_glu_w'], p['s5_glu_b'], p['w_out'])


def _trunk(x, layers, final_norm):
    B, T, D = x.shape
    x = x.reshape(B * T, D)
    rope_tabs = _rope_tables(T)
    for l, p in enumerate(layers):
        x = _ffn(x, p['ffn1_norm'], p['ffn1_w_gate'], p['ffn1_w_up'], p['ffn1_w_down'])
        x = _mixers(x, p, B, T, l, rope_tabs)
        last = l == len(layers) - 1
        x = _ffn(x, p['ffn2_norm'], p['ffn2_w_gate'], p['ffn2_w_up'], p['ffn2_w_down'],
                 final_g=final_norm if last else None)
    return x.reshape(B, T, D)


def kernel(x_prompt, x_sample, ffn1_norm, ffn1_w_gate, ffn1_w_up, ffn1_w_down, mix_norm, w_in, mlstm_conv_w, mlstm_conv_b, mlstm_gate_bias, mlstm_norm, diff_lambda, diff_norm, na_rpb, s5_lambda_re, s5_lambda_im, s5_log_step, s5_b_re, s5_b_im, s5_c_re, s5_c_im, s5_d, s5_glu_w, s5_glu_b, w_out, ffn2_norm, ffn2_w_gate, ffn2_w_up, ffn2_w_down, final_norm):
    w = dict(ffn1_norm=ffn1_norm, ffn1_w_gate=ffn1_w_gate, ffn1_w_up=ffn1_w_up, ffn1_w_down=ffn1_w_down,
             mix_norm=mix_norm, w_in=w_in, mlstm_conv_w=mlstm_conv_w, mlstm_conv_b=mlstm_conv_b,
             mlstm_gate_bias=mlstm_gate_bias, mlstm_norm=mlstm_norm, diff_lambda=diff_lambda, diff_norm=diff_norm,
             na_rpb=na_rpb, s5_lambda_re=s5_lambda_re, s5_lambda_im=s5_lambda_im, s5_log_step=s5_log_step,
             s5_b_re=s5_b_re, s5_b_im=s5_b_im, s5_c_re=s5_c_re, s5_c_im=s5_c_im, s5_d=s5_d,
             s5_glu_w=s5_glu_w, s5_glu_b=s5_glu_b, w_out=w_out, ffn2_norm=ffn2_norm, ffn2_w_gate=ffn2_w_gate,
             ffn2_w_up=ffn2_w_up, ffn2_w_down=ffn2_w_down)
    layers = [_prepare_layer(w, l) for l in range(ffn1_norm.shape[0])]
    return (_trunk(x_prompt, layers, final_norm), _trunk(x_sample, layers, final_norm))
```

```python
import functools
import math

import jax
import jax.numpy as jnp
from jax import lax
from jax.experimental import pallas as pl
from jax.experimental.pallas import tpu as pltpu

F32 = jnp.float32
BF16 = jnp.bfloat16

NORM_EPS = 1e-6
GROUP_W = 512
GRID_W = 64
MLSTM_HEADS = 4
MLSTM_DH = 128
MLSTM_CONV = 5
MLSTM_CHUNK = 128
MLSTM_CHUNKS_PER_STEP = 4
DIFF_HEADS = 4
DIFF_DH = 64
ROPE_THETA = 10000.0
NA_HEADS = 8
NA_DH = 64
NA_WIN_ROWS = 8
NA_WIN_W = 16
NA_ROWS_PER_STEP = 16
S5_GROUP_CH = 16
S5_GROUPS = 32
S5_STATE = 64
S5_CHUNK = 16
LANES = 128
BF16_SUBLANES = 16
MASK_VALUE = -1e30
VMEM_LIMIT = 56 * 1024 * 1024


def _cparams(*sem):
    return pltpu.CompilerParams(dimension_semantics=sem, vmem_limit_bytes=VMEM_LIMIT)


def _rms(x, g):
    return x * lax.rsqrt(jnp.mean(x * x, axis=-1, keepdims=True) + NORM_EPS) * g


def _sigmoid(x):
    return 1.0 / (1.0 + jnp.exp(-x))


def _dot(a, b):
    return jnp.dot(a, b, preferred_element_type=F32)


def _dot_nt(a, b):
    return lax.dot_general(a, b, (((1,), (1,)), ((), ())), preferred_element_type=F32)


def _dot_tn(a, b):
    return lax.dot_general(a, b, (((0,), (0,)), ((), ())), preferred_element_type=F32)


def _split3(x):
    hi = x.astype(BF16)
    r1 = x - hi.astype(F32)
    mid = r1.astype(BF16)
    lo = (r1 - mid.astype(F32)).astype(BF16)
    return hi, mid, lo


def _ffn_kernel(x_ref, g_ref, wg_ref, wu_ref, wd_ref, *rest, final):
    if final:
        fg_ref, o_ref, h_ref = rest
    else:
        o_ref, h_ref = rest
    f = pl.program_id(1)
    nf = pl.num_programs(1)

    @pl.when(f == 0)
    def _():
        h_ref[...] = _rms(x_ref[...], g_ref[...]).astype(BF16)
        o_ref[...] = x_ref[...]

    h = h_ref[...]
    gate = _dot(h, wg_ref[...])
    up = _dot(h, wu_ref[...])
    act = (0.5 * gate * _sigmoid(gate) * up).astype(BF16)
    nchunk = 4
    cw = o_ref.shape[1] // nchunk
    for c in range(nchunk):
        o_ref[:, c * cw:(c + 1) * cw] += _dot(act, wd_ref[:, c * cw:(c + 1) * cw])

    if final:
        @pl.when(f == nf - 1)
        def _():
            o_ref[...] = _rms(o_ref[...], fg_ref[...])


def _ffn(x, g, wg, wu, wd, final_g=None, tm=1024, tf=512):
    M, D = x.shape
    F = wg.shape[1]
    final = final_g is not None
    in_specs = [
        pl.BlockSpec((tm, D), lambda i, f: (i, 0), pipeline_mode=pl.Buffered(1)),
        pl.BlockSpec((1, D), lambda i, f: (0, 0)),
        pl.BlockSpec((D, tf), lambda i, f: (0, f)),
        pl.BlockSpec((D, tf), lambda i, f: (0, f)),
        pl.BlockSpec((tf, D), lambda i, f: (f, 0)),
    ]
    args = [x, g.reshape(1, D), wg, wu, wd]
    if final:
        in_specs.append(pl.BlockSpec((1, D), lambda i, f: (0, 0)))
        args.append(final_g.reshape(1, D))
    return pl.pallas_call(
        functools.partial(_ffn_kernel, final=final),
        grid=(M // tm, F // tf),
        in_specs=in_specs,
        out_specs=pl.BlockSpec((tm, D), lambda i, f: (i, 0)),
        out_shape=jax.ShapeDtypeStruct((M, D), F32),
        scratch_shapes=[pltpu.VMEM((tm, D), BF16)],
        compiler_params=_cparams("parallel", "arbitrary"),
        name="ffn",
    )(*args)


def _inproj_kernel(x_ref, g_ref, w_ref, wgate_ref, o_ref, og_ref, os_ref, h_ref):
    n = pl.program_id(1)

    @pl.when(n == 0)
    def _():
        h = _rms(x_ref[...], g_ref[...]).astype(BF16)
        h_ref[...] = h
        og_ref[...] = _dot(h, wgate_ref[...])

    res = _dot(h_ref[...], w_ref[...])
    o_ref[...] = res.astype(BF16)

    @pl.when(n == pl.num_programs(1) - 1)
    def _():
        os_ref[...] = res[:, res.shape[1] - GROUP_W:]


def _inproj(x, g, w_main, w_gate, tm=1024, tn=1408):
    M, D = x.shape
    N = w_main.shape[1]
    assert tn >= GROUP_W and N % tn == 0
    return pl.pallas_call(
        _inproj_kernel,
        grid=(M // tm, N // tn),
        in_specs=[
            pl.BlockSpec((tm, D), lambda i, n: (i, 0)),
            pl.BlockSpec((1, D), lambda i, n: (0, 0)),
            pl.BlockSpec((D, tn), lambda i, n: (0, n)),
            pl.BlockSpec((D, LANES), lambda i, n: (0, 0)),
        ],
        out_specs=[
            pl.BlockSpec((tm, tn), lambda i, n: (i, n)),
            pl.BlockSpec((tm, LANES), lambda i, n: (i, 0)),
            pl.BlockSpec((tm, GROUP_W), lambda i, n: (i, 0)),
        ],
        out_shape=[
            jax.ShapeDtypeStruct((M, N), BF16),
            jax.ShapeDtypeStruct((M, LANES), F32),
            jax.ShapeDtypeStruct((M, GROUP_W), F32),
        ],
        scratch_shapes=[pltpu.VMEM((tm, D), BF16)],
        compiler_params=_cparams("parallel", "arbitrary"),
        name="inproj",
    )(x, g.reshape(1, D), w_main, w_gate)


def _conv_kernel(x_ref, prev_ref, next_ref, w_ref, b_ref, q_ref, k_ref, *, blocks_per_seq):
    tb = x_ref.shape[0]
    halo = prev_ref.shape[0]
    pos = pl.program_id(0) % blocks_per_seq
    prev = jnp.where(pos != 0, prev_ref[...].astype(F32), 0.0)
    nxt = jnp.where(pos != blocks_per_seq - 1, next_ref[...].astype(F32), 0.0)
    ext = jnp.concatenate([prev, x_ref[...].astype(F32), nxt], axis=0)
    rows = tb + 2 * halo
    pad = MLSTM_CONV // 2
    acc = None
    for j in range(MLSTM_CONV):
        shift = (pad - j) % rows
        sh = ext if shift == 0 else pltpu.roll(ext, shift, axis=0)
        term = sh[halo:halo + tb] * w_ref[j:j + 1, :]
        acc = term if acc is None else acc + term
    acc = acc + b_ref[...]
    y = acc * _sigmoid(acc)
    q_ref[...] = y[:, :GROUP_W].astype(BF16)
    k_ref[...] = (y[:, GROUP_W:] * (MLSTM_DH ** -0.5)).astype(BF16)


def _mlstm_conv(proj, conv_w, conv_b, T, tb=512):
    M = proj.shape[0]
    C = 2 * GROUP_W
    halo = BF16_SUBLANES
    r = tb // halo
    nhalo = M // halo
    w = jnp.zeros((8, C), F32).at[:MLSTM_CONV].set(conv_w)
    return pl.pallas_call(
        functools.partial(_conv_kernel, blocks_per_seq=T // tb),
        grid=(M // tb,),
        in_specs=[
            pl.BlockSpec((tb, C), lambda i: (i, 0)),
            pl.BlockSpec((halo, C), lambda i: (jnp.maximum(i * r - 1, 0), 0)),
            pl.BlockSpec((halo, C), lambda i: (jnp.minimum((i + 1) * r, nhalo - 1), 0)),
            pl.BlockSpec((8, C), lambda i: (0, 0)),
            pl.BlockSpec((1, C), lambda i: (0, 0)),
        ],
        out_specs=[
            pl.BlockSpec((tb, GROUP_W), lambda i: (i, 0)),
            pl.BlockSpec((tb, GROUP_W), lambda i: (i, 0)),
        ],
        out_shape=[jax.ShapeDtypeStruct((M, GROUP_W), BF16)] * 2,
        compiler_params=_cparams("parallel"),
        name="mlstm_conv",
    )(proj, proj, proj, w, conv_b.reshape(1, C))


def _mlstm_kernel(qf_ref, kf_ref, vf_ref, gf_ref, qb_ref, kb_ref, vb_ref, gb_ref, bias_ref,
                  of_ref, ob_ref, c_ref, m_ref):
    L = MLSTM_CHUNK
    nsub = qf_ref.shape[0] // L
    H, dh = MLSTM_HEADS, MLSTM_DH

    @pl.when(pl.program_id(1) == 0)
    def _():
        c_ref[...] = jnp.zeros_like(c_ref)
        m_ref[...] = jnp.zeros_like(m_ref)

    row = lax.broadcasted_iota(jnp.int32, (L, L), 0)
    col = lax.broadcasted_iota(jnp.int32, (L, L), 1)
    ones_blk = jnp.ones((L, dh), BF16)

    dirs = ((qf_ref, kf_ref, vf_ref, gf_ref, of_ref), (qb_ref, kb_ref, vb_ref, gb_ref, ob_ref))
    visits = [(d, (j if d == 0 else nsub - 1 - j) * L) for j in range(nsub) for d in range(2)]
    gate_terms = []
    for d, r0 in visits:
        seen = (col <= row) if d == 0 else (col >= row)
        tri = jnp.where(seen, 1.0, 0.0).astype(BF16)
        gt = dirs[d][3][r0:r0 + L, :] + bias_ref[...]
        lf = jnp.minimum(gt, 0.0) - jnp.log(1.0 + jnp.exp(-jnp.abs(gt)))
        hi, mid, lo = _split3(lf)
        bcum = _dot(tri, hi) + _dot(tri, mid) + _dot(tri, lo)
        gate_terms.append((seen, gt, bcum, bcum.T, gt.T))

    for (d, r0), (seen, gt, bcum, bcum_t, gt_t) in zip(visits, gate_terms):
        q_ref, k_ref, v_ref, _, o_ref = dirs[d]
        rows = slice(r0, r0 + L)
        end = L - 1 if d == 0 else 0
        for h in range(H):
            ic = 8 * d + h
            fc = 8 * d + 4 + h
            idx = d * H + h
            b_rep = jnp.broadcast_to(bcum[:, fc:fc + 1], (L, dh))
            i_rep = jnp.broadcast_to(gt[:, ic:ic + 1], (L, dh))
            b_row = bcum_t[fc:fc + 1, :]
            i_row = gt_t[ic:ic + 1, :]
            g_tot = b_rep[end:end + 1, :]
            m_prev = m_ref[idx]
            c_prev = c_ref[idx]
            q = q_ref[rows, h * dh:(h + 1) * dh]
            k = k_ref[rows, h * dh:(h + 1) * dh]
            v_aug = jnp.concatenate([v_ref[rows, h * dh:(h + 1) * dh], ones_blk], axis=1)

            dlog = jnp.where(seen, b_rep - b_row + i_row, -jnp.inf)
            inter = b_rep + m_prev
            m_t = jnp.maximum(inter, jnp.max(dlog, axis=1, keepdims=True))
            s = (_dot_nt(q, k) * jnp.exp(dlog - m_t)).astype(BF16)
            s_inter = jnp.exp(inter - m_t)
            num_aug = jnp.concatenate([s_inter, s_inter], axis=1) * _dot(q, c_prev.astype(BF16)) + _dot(s, v_aug)
            den = jnp.maximum(jnp.abs(num_aug[:, dh:]), jnp.exp(-m_t))
            o_ref[rows, h * dh:(h + 1) * dh] = num_aug[:, :dh] / den

            a_rep = g_tot - b_rep + i_rep
            m_new = jnp.maximum(g_tot + m_prev, jnp.max(a_rep, axis=0, keepdims=True))
            decay = jnp.exp(g_tot + m_prev - m_new)
            kw = (k.astype(F32) * jnp.exp(a_rep - m_new)).astype(BF16)
            c_ref[idx] = jnp.concatenate([decay, decay], axis=1) * c_prev + _dot_tn(kw, v_aug)
            m_ref[idx] = m_new


def _mlstm(qc, kc, proj, gates, gate_bias, B, T):
    M = qc.shape[0]
    L = MLSTM_CHUNK
    assert L == LANES == MLSTM_DH
    tb = MLSTM_CHUNKS_PER_STEP * L
    nc = T // tb
    W = GROUP_W
    vblk = 2
    fwd = lambda b, i: (b * nc + i, 0)
    bwd = lambda b, i: (b * nc + nc - 1 - i, 0)
    fwd_v = lambda b, i: (b * nc + i, vblk)
    bwd_v = lambda b, i: (b * nc + nc - 1 - i, vblk)
    bias = jnp.zeros((1, LANES), F32).at[0, :4 * MLSTM_HEADS].set(gate_bias)
    return pl.pallas_call(
        _mlstm_kernel,
        grid=(B, nc),
        in_specs=[
            pl.BlockSpec((tb, W), fwd), pl.BlockSpec((tb, W), fwd), pl.BlockSpec((tb, W), fwd_v),
            pl.BlockSpec((tb, LANES), fwd),
            pl.BlockSpec((tb, W), bwd), pl.BlockSpec((tb, W), bwd), pl.BlockSpec((tb, W), bwd_v),
            pl.BlockSpec((tb, LANES), bwd),
            pl.BlockSpec((1, LANES), lambda b, i: (0, 0)),
        ],
        out_specs=[pl.BlockSpec((tb, W), fwd), pl.BlockSpec((tb, W), bwd)],
        out_shape=[jax.ShapeDtypeStruct((M, W), F32)] * 2,
        scratch_shapes=[
            pltpu.VMEM((2 * MLSTM_HEADS, MLSTM_DH, 2 * MLSTM_DH), F32),
            pltpu.VMEM((2 * MLSTM_HEADS, 1, LANES), F32),
        ],
        compiler_params=_cparams("parallel", "arbitrary"),
        name="mlstm",
    )(qc, kc, proj, gates, qc, kc, proj, gates, bias)


def _rope_kernel(x_ref, cos_ref, sin_ref, q_ref, k_ref):
    x = x_ref[...].astype(F32)
    reps = x.shape[1] // LANES
    cos = jnp.concatenate([cos_ref[...]] * reps, axis=1)
    sin = jnp.concatenate([sin_ref[...]] * reps, axis=1)
    lane = lax.broadcasted_iota(jnp.int32, x.shape, 1)
    width = x.shape[1]
    half = DIFF_DH // 2
    rot = jnp.where(lane % DIFF_DH < half, pltpu.roll(x, width - half, axis=1), pltpu.roll(x, half, axis=1))
    y = x * cos + rot * sin
    q_ref[...] = (y[:, :GROUP_W] * (DIFF_DH ** -0.5 * math.log2(math.e))).astype(BF16)
    k_ref[...] = y[:, GROUP_W:].astype(BF16)


def _rope_tables(T):
    half = DIFF_DH // 2
    inv = 1.0 / (ROPE_THETA ** (jnp.arange(0, DIFF_DH, 2, dtype=F32) / DIFF_DH))
    ang = jnp.arange(T, dtype=F32)[:, None] * inv[None, :]
    cos, sin = jnp.cos(ang), jnp.sin(ang)
    cos_t = jnp.concatenate([cos, cos] * (LANES // DIFF_DH), axis=1)
    sin_t = jnp.concatenate([-sin, sin] * (LANES // DIFF_DH), axis=1)
    return cos_t, sin_t


def _rope(proj, cos_t, sin_t, T, tb=256):
    M = proj.shape[0]
    nb = T // tb
    return pl.pallas_call(
        _rope_kernel,
        grid=(M // tb,),
        in_specs=[
            pl.BlockSpec((tb, 2 * GROUP_W), lambda i: (i, 2)),
            pl.BlockSpec((tb, LANES), lambda i: (i % nb, 0)),
            pl.BlockSpec((tb, LANES), lambda i: (i % nb, 0)),
        ],
        out_specs=[pl.BlockSpec((tb, GROUP_W), lambda i: (i, 0))] * 2,
        out_shape=[jax.ShapeDtypeStruct((M, GROUP_W), BF16)] * 2,
        compiler_params=_cparams("parallel"),
        name="rope",
    )(proj, cos_t, sin_t)


def _diff_kernel(q_ref, k_ref, v_ref, lp_ref, g_ref, o_ref, q2_ref, m_ref, acc_ref, *, lam_init, tks):
    ki = pl.program_id(3)
    tq = q_ref.shape[0]
    dv = v_ref.shape[1]
    ones = jnp.ones((tks, dv), BF16)

    @pl.when(ki == 0)
    def _():
        q = q_ref[...]
        lane = lax.broadcasted_iota(jnp.int32, q.shape, 1)
        zero = jnp.zeros_like(q)
        q2_ref[0:tq, :] = jnp.where(lane < DIFF_DH, q, zero)
        q2_ref[tq:2 * tq, :] = jnp.where(lane < DIFF_DH, zero, q)
        m_ref[...] = jnp.full_like(m_ref, -jnp.inf)
        acc_ref[...] = jnp.zeros_like(acc_ref)

    q2 = q2_ref[...]
    for j in range(k_ref.shape[0] // tks):
        s = _dot_nt(q2, k_ref[j * tks:(j + 1) * tks, :])
        tiles = [s[:, c * LANES:(c + 1) * LANES] for c in range(tks // LANES)]
        m_prev = m_ref[...]
        m_new = jnp.maximum(m_prev, jnp.max(functools.reduce(jnp.maximum, tiles), axis=1, keepdims=True))
        alpha = jnp.exp2(m_prev - m_new)
        p = jnp.concatenate([jnp.exp2(t - m_new).astype(BF16) for t in tiles], axis=1)
        v_aug = jnp.concatenate([v_ref[j * tks:(j + 1) * tks, :], ones], axis=1)
        acc_ref[...] = jnp.concatenate([alpha, alpha], axis=1) * acc_ref[...] + _dot(p, v_aug)
        m_ref[...] = m_new

    @pl.when(ki == pl.num_programs(3) - 1)
    def _():
        lp = lp_ref[...]
        lam = (jnp.exp(jnp.sum(lp[0:1] * lp[1:2], axis=1, keepdims=True))
               - jnp.exp(jnp.sum(lp[2:3] * lp[3:4], axis=1, keepdims=True)) + lam_init)
        o = acc_ref[:, 0:dv] / acc_ref[:, dv:2 * dv]
        out = o[0:tq] - lam * o[tq:2 * tq]
        o_ref[...] = (_rms(out, g_ref[...]) * (1.0 - lam_init)).astype(BF16)


def _diff_attn(qr, kr, proj, lam_params, norm_g, B, T, layer_idx, tq=1024, tk=4096, tks=512):
    M = qr.shape[0]
    dv = 2 * DIFF_DH
    tk = min(tk, T)
    nq, nk = T // tq, T // tk
    vblk0 = (6 * GROUP_W) // dv
    lam_init = 0.8 - 0.6 * math.exp(-0.3 * layer_idx)
    return pl.pallas_call(
        functools.partial(_diff_kernel, lam_init=lam_init, tks=tks),
        grid=(B, DIFF_HEADS, nq, nk),
        in_specs=[
            pl.BlockSpec((tq, dv), lambda b, h, qi, ki: (b * nq + qi, h)),
            pl.BlockSpec((tk, dv), lambda b, h, qi, ki: (b * nk + ki, h)),
            pl.BlockSpec((tk, dv), lambda b, h, qi, ki: (b * nk + ki, vblk0 + h)),
            pl.BlockSpec((4, DIFF_DH), lambda b, h, qi, ki: (0, 0)),
            pl.BlockSpec((1, dv), lambda b, h, qi, ki: (0, 0)),
        ],
        out_specs=pl.BlockSpec((tq, dv), lambda b, h, qi, ki: (b * nq + qi, h)),
        out_shape=jax.ShapeDtypeStruct((M, GROUP_W), BF16),
        scratch_shapes=[
            pltpu.VMEM((2 * tq, dv), BF16),
            pltpu.VMEM((2 * tq, LANES), F32),
            pltpu.VMEM((2 * tq, 2 * dv), F32),
        ],
        compiler_params=_cparams("parallel", "parallel", "parallel", "arbitrary"),
        name="diff_attn",
    )(qr, kr, proj, lam_params, norm_g.reshape(1, dv))


def _na_kernel(q_ref, k_ref, v_ref, bias_ref, o_ref, *, rows):
    g = pl.program_id(2)
    win = NA_WIN_ROWS * GRID_W
    lane = lax.broadcasted_iota(jnp.int32, (GRID_W, 2 * NA_DH), 1)
    first = lane < NA_DH
    w = 2 * NA_DH
    ones = jnp.ones((win, w), BF16)
    scores, starts = [], []
    for jr in range(NA_ROWS_PER_STEP):
        r = g * NA_ROWS_PER_STEP + jr
        rs = jnp.clip(r - NA_WIN_ROWS // 2, 0, rows - NA_WIN_ROWS)
        start = pl.multiple_of(rs * GRID_W, GRID_W)
        q = q_ref[jr * GRID_W:(jr + 1) * GRID_W, :]
        zero = jnp.zeros_like(q)
        q2 = jnp.concatenate([jnp.where(first, q, zero), jnp.where(first, zero, q)], axis=0)
        s = _dot_nt(q2, k_ref[pl.ds(start, win), :]) * (NA_DH ** -0.5 * math.log2(math.e)) + bias_ref[r - rs]
        scores.append(s)
        starts.append(start)
    for jr in range(NA_ROWS_PER_STEP):
        s = scores[jr]
        e = jnp.exp2(s - jnp.max(s, axis=1, keepdims=True)).astype(BF16)
        pv = _dot(e, jnp.concatenate([v_ref[pl.ds(starts[jr], win), :], ones], axis=1))
        pv = pv[:, :w] / pv[:, w:]
        o_ref[jr * GRID_W:(jr + 1) * GRID_W, :] = jnp.where(first, pv[:GRID_W], pv[GRID_W:]).astype(BF16)


def _na_bias_table(rpb):
    c = jnp.arange(GRID_W)
    col_start = jnp.clip(c - NA_WIN_W // 2, 0, GRID_W - NA_WIN_W)
    kc = jnp.arange(GRID_W)
    valid = (kc[None, :] >= col_start[:, None]) & (kc[None, :] < col_start[:, None] + NA_WIN_W)
    col_off = kc[None, :] - c[:, None] + NA_WIN_W - 1
    p = jnp.arange(NA_WIN_ROWS)
    k = jnp.arange(NA_WIN_ROWS)
    row_off = k[None, :] - p[:, None] + NA_WIN_ROWS - 1
    pick_row = (row_off[..., None] == jnp.arange(2 * NA_WIN_ROWS - 1)).astype(F32)
    pick_col = (col_off[..., None] == jnp.arange(2 * NA_WIN_W - 1)).astype(F32)
    t = jnp.einsum('hrq,pkr,cjq->hpckj', rpb.astype(F32), pick_row, pick_col,
                   precision=lax.Precision.HIGHEST) * math.log2(math.e)
    t = jnp.where(valid[None, None, :, None, :], t, MASK_VALUE)
    t = t.reshape(NA_HEADS, NA_WIN_ROWS, GRID_W, NA_WIN_ROWS * GRID_W)
    t = t.reshape(NA_HEADS // 2, 2, NA_WIN_ROWS, GRID_W, NA_WIN_ROWS * GRID_W).transpose(0, 2, 1, 3, 4)
    return t.reshape(NA_HEADS // 2, NA_WIN_ROWS, 2 * GRID_W, NA_WIN_ROWS * GRID_W)


def _na_attn(proj, bias_table, B, T):
    M = proj.shape[0]
    rows = T // GRID_W
    tq = NA_ROWS_PER_STEP * GRID_W
    ng = rows // NA_ROWS_PER_STEP
    hp = NA_HEADS // 2
    w = 2 * NA_DH
    qblk0, kblk0, vblk0 = (7 * GROUP_W) // w, (8 * GROUP_W) // w, (9 * GROUP_W) // w
    return pl.pallas_call(
        functools.partial(_na_kernel, rows=rows),
        grid=(B, hp, ng),
        in_specs=[
            pl.BlockSpec((tq, w), lambda b, h, g: (b * ng + g, qblk0 + h)),
            pl.BlockSpec((T, w), lambda b, h, g: (b, kblk0 + h)),
            pl.BlockSpec((T, w), lambda b, h, g: (b, vblk0 + h)),
            pl.BlockSpec((None, NA_WIN_ROWS, 2 * GRID_W, NA_WIN_ROWS * GRID_W), lambda b, h, g: (h, 0, 0, 0)),
        ],
        out_specs=pl.BlockSpec((tq, w), lambda b, h, g: (b * ng + g, h)),
        out_shape=jax.ShapeDtypeStruct((M, GROUP_W), BF16),
        compiler_params=_cparams("parallel", "parallel", "arbitrary"),
        name="na_attn",
    )(proj, proj, proj, bias_table)


def _s5_params(lam_re, lam_im, log_step, b_re, b_im, c_re, c_im, d_skip):
    L, c, P, G = S5_CHUNK, S5_GROUP_CH, S5_STATE, S5_GROUPS
    K8 = LANES // c
    O = G // K8
    lam = lax.complex(lam_re.astype(F32), lam_im.astype(F32))
    step = jnp.exp(log_step.astype(F32))[..., None]
    log_lam_bar = lam * step
    lam_bar = jnp.exp(log_lam_bar)
    b_mat = lax.complex(b_re.astype(F32), b_im.astype(F32))
    c_mat = lax.complex(c_re.astype(F32), c_im.astype(F32))
    b_bar = ((lam_bar - 1.0) / lam)[..., None] * b_mat[None]
    pw = jnp.exp(log_lam_bar[:, None] * jnp.arange(L + 1, dtype=F32)[None, :, None, None])
    cb = c_mat[None, :, :, None, :] * b_bar.transpose(0, 1, 3, 2)[:, :, None, :, :]
    s_i = jnp.arange(L)[:, None]
    t_i = jnp.arange(L)[None, :]
    lag = jnp.stack([t_i - s_i, s_i - t_i])
    lag_pw = jnp.where((lag >= 0)[..., None, None],
                       jnp.exp(log_lam_bar[:, None, None] * jnp.maximum(lag, 0).astype(F32)[..., None, None]),
                       0.0)
    m = jnp.einsum('dgoip,dstgp->gsito', jnp.concatenate([cb.real, -cb.imag], axis=-1),
                   jnp.concatenate([lag_pw.real, lag_pw.imag], axis=-1), precision=lax.Precision.HIGH)
    eye = jnp.eye(K8, dtype=F32)
    lane_of = jnp.arange(K8)[:, None] * c + jnp.arange(c)[None, :]
    sel = (jnp.arange(LANES)[None, None, :] == lane_of[:, :, None]).astype(BF16)
    m8 = jnp.einsum('agsitc,gcl->asgitl', m.astype(BF16).reshape(O, K8, L, c, L, c), sel,
                    preferred_element_type=BF16).reshape(O, L * LANES, L * LANES)
    ef = pw[0][L - 1 - jnp.arange(L)][:, :, :, None] * b_bar[0][None]
    eb = pw[1][jnp.arange(L)][:, :, :, None] * b_bar[1][None]
    to_rows = lambda e: e.transpose(1, 0, 3, 2)
    ef, eb = to_rows(ef), to_rows(eb)
    bs = jnp.concatenate([ef.real, ef.imag, eb.real, eb.imag], axis=-1).astype(BF16)
    bs8 = jnp.einsum('agsik,gh->asgihk', bs.reshape(O, K8, L, c, 4 * P), eye.astype(BF16),
                     preferred_element_type=BF16).reshape(O, L * LANES, K8 * 4 * P)
    of = c_mat[:, None] * pw[0][1:L + 1].transpose(1, 0, 2)[:, :, None, :]
    ob = c_mat[:, None] * pw[1][L - jnp.arange(L)].transpose(1, 0, 2)[:, :, None, :]
    to_cols = lambda o: o.transpose(0, 3, 1, 2)
    of, ob = to_cols(of), to_cols(ob)
    cs = jnp.concatenate([of.real, -of.imag, ob.real, -ob.imag], axis=1)
    cs8 = jnp.einsum('agktc,gcl->agktl', cs.astype(BF16).reshape(O, K8, 4 * P, L, c), sel,
                     preferred_element_type=BF16).reshape(O, K8 * 4 * P, L * LANES)
    al = pw[:, L]
    a1 = jnp.concatenate([al.real, al.real], axis=-1)
    a2 = jnp.concatenate([-al.imag, al.imag], axis=-1)
    a8 = jnp.stack([a1, a2, -a2], axis=1)
    a8 = a8.reshape(2, 3, O, K8, LANES).transpose(2, 0, 1, 3, 4).reshape(O, 6 * K8, LANES)
    d8 = d_skip.astype(F32).reshape(O, 1, LANES)
    return m8.astype(BF16), bs8.astype(BF16), cs8.astype(BF16), a8, d8


def _s5_kernel(u_ref, m_ref, bs_ref, cs_ref, a_ref, d_ref, y_ref, x_ref, sf_ref, sb_ref, sfs_ref, sbs_ref):
    L = S5_CHUNK
    K8 = LANES // S5_GROUP_CH
    sw = 4 * S5_STATE
    nc = u_ref.shape[0] // L
    for s in range(L):
        x_ref[:, s * LANES:(s + 1) * LANES] = u_ref[pl.ds(s, nc, stride=L), :].astype(BF16)
    x = x_ref[...]
    for g in range(K8):
        sg = _dot(x, bs_ref[:, g * sw:(g + 1) * sw])
        sf_ref[g * nc:(g + 1) * nc, :] = sg[:, 0:LANES]
        sb_ref[g * nc:(g + 1) * nc, :] = sg[:, LANES:2 * LANES]
        sfs_ref[g * nc:(g + 1) * nc, :] = pltpu.roll(sg[:, 0:LANES], LANES // 2, axis=1)
        sbs_ref[g * nc:(g + 1) * nc, :] = pltpu.roll(sg[:, LANES:2 * LANES], LANES // 2, axis=1)
    a1f, a2f, a2fs = a_ref[0:K8], a_ref[K8:2 * K8], a_ref[2 * K8:3 * K8]
    a1b, a2b, a2bs = a_ref[3 * K8:4 * K8], a_ref[4 * K8:5 * K8], a_ref[5 * K8:6 * K8]

    def step(c, carry):
        hf, hfs, hb, hbs = carry
        cb = nc - 1 - c
        fwd = pl.ds(c, K8, stride=nc)
        bwd = pl.ds(cb, K8, stride=nc)
        sf, sfs = sf_ref[fwd, :], sfs_ref[fwd, :]
        sb, sbs = sb_ref[bwd, :], sbs_ref[bwd, :]
        sf_ref[fwd, :] = hf
        sb_ref[bwd, :] = hb
        return (a1f * hf + a2f * hfs + sf, a1f * hfs + a2fs * hf + sfs,
                a1b * hb + a2b * hbs + sb, a1b * hbs + a2bs * hb + sbs)

    z = jnp.zeros((K8, LANES), F32)
    lax.fori_loop(0, nc, step, (z, z, z, z), unroll=8)
    y = _dot(x, m_ref[...])
    for g in range(K8):
        hg = jnp.concatenate([sf_ref[g * nc:(g + 1) * nc, :], sb_ref[g * nc:(g + 1) * nc, :]], axis=1)
        y = y + _dot(hg.astype(BF16), cs_ref[g * sw:(g + 1) * sw, :])
    for t in range(L):
        y_ref[pl.ds(t, nc, stride=L), :] = y[:, t * LANES:(t + 1) * LANES]
    y_ref[...] += u_ref[...] * d_ref[...]


def _s5(su, params, B, T):
    m8, bs8, cs8, a8, d8 = params
    L = S5_CHUNK
    nc = T // L
    O = m8.shape[0]
    kw = L * LANES
    sw = 4 * S5_STATE
    K8 = LANES // S5_GROUP_CH
    one = pl.Buffered(1)
    return pl.pallas_call(
        _s5_kernel,
        grid=(O, B),
        in_specs=[
            pl.BlockSpec((T, LANES), lambda o, b: (b, o), pipeline_mode=one),
            pl.BlockSpec((None, kw, kw), lambda o, b: (o, 0, 0), pipeline_mode=one),
            pl.BlockSpec((None, kw, K8 * sw), lambda o, b: (o, 0, 0), pipeline_mode=one),
            pl.BlockSpec((None, K8 * sw, kw), lambda o, b: (o, 0, 0), pipeline_mode=one),
            pl.BlockSpec((None, 6 * K8, LANES), lambda o, b: (o, 0, 0)),
            pl.BlockSpec((None, 1, LANES), lambda o, b: (o, 0, 0)),
        ],
        out_specs=pl.BlockSpec((T, LANES), lambda o, b: (b, o)),
        out_shape=jax.ShapeDtypeStruct((B * T, GROUP_W), F32),
        scratch_shapes=[pltpu.VMEM((nc, kw), BF16)] + [pltpu.VMEM((K8 * nc, LANES), F32)] * 4,
        compiler_params=_cparams("arbitrary", "arbitrary"),
        name="s5",
    )(su, m8, bs8, cs8, a8, d8)


def _outproj_kernel(x_ref, hf_ref, hb_ref, og_ref, mg_ref, yb_ref, yc_ref, ys_ref, gw_ref, gb_ref, w_ref, o_ref):
    W = GROUP_W
    h = hf_ref[...] + hb_ref[...]
    parts = []
    for j in range(MLSTM_HEADS):
        hj = h[:, j * MLSTM_DH:(j + 1) * MLSTM_DH]
        parts.append(hj * lax.rsqrt(jnp.mean(hj * hj, axis=-1, keepdims=True) + NORM_EPS))
    hn = jnp.concatenate(parts, axis=1) * mg_ref[...]
    ya = (hn * _sigmoid(og_ref[...].astype(F32))).astype(BF16)
    y = ys_ref[...]
    z = 0.5 * y * (1.0 + jnp.tanh(math.sqrt(2.0 / math.pi) * (y + 0.044715 * (y * y * y))))
    yd = (z * _sigmoid(_dot(z.astype(BF16), gw_ref[...]) + gb_ref[...])).astype(BF16)
    acc = x_ref[...] + _dot(ya, w_ref[0:W, :])
    acc = acc + _dot(yb_ref[...], w_ref[W:2 * W, :])
    acc = acc + _dot(yc_ref[...], w_ref[2 * W:3 * W, :])
    acc = acc + _dot(yd, w_ref[3 * W:4 * W, :])
    o_ref[...] = acc


def _outproj(x, hf, hb, proj, mlstm_norm, yb, yc, ys, glu_w, glu_b, w_out, tm=512):
    M, D = x.shape
    W = GROUP_W
    tok = lambda i: (i, 0)
    const = lambda i: (0, 0)
    return pl.pallas_call(
        _outproj_kernel,
        grid=(M // tm,),
        in_specs=[
            pl.BlockSpec((tm, D), tok),
            pl.BlockSpec((tm, W), tok), pl.BlockSpec((tm, W), tok),
            pl.BlockSpec((tm, W), lambda i: (i, 3)),
            pl.BlockSpec((1, W), const),
            pl.BlockSpec((tm, W), tok), pl.BlockSpec((tm, W), tok), pl.BlockSpec((tm, W), tok),
            pl.BlockSpec((W, W), const), pl.BlockSpec((1, W), const),
            pl.BlockSpec((4 * W, D), const),
        ],
        out_specs=pl.BlockSpec((tm, D), tok),
        out_shape=jax.ShapeDtypeStruct((M, D), F32),
        compiler_params=_cparams("parallel"),
        name="outproj",
    )(x, hf, hb, proj, mlstm_norm.reshape(1, W), yb, yc, ys, glu_w, glu_b.reshape(1, W), w_out)


def _prepare_layer(w, l):
    W = GROUP_W
    ng = 4 * MLSTM_HEADS
    w_in = w['w_in'][l]
    w_main = jnp.concatenate([w_in[:, :4 * W], w_in[:, 4 * W + ng:]], axis=1).astype(BF16)
    w_gate = jnp.zeros((w_in.shape[0], LANES), BF16).at[:, :ng].set(w_in[:, 4 * W:4 * W + ng].astype(BF16))
    p = {k: w[k][l] for k in ('ffn1_norm', 'mix_norm', 'ffn2_norm', 'mlstm_conv_w', 'mlstm_conv_b', 'mlstm_gate_bias',
                              'mlstm_norm', 'diff_lambda', 'diff_norm', 's5_glu_b')}
    for k in ('ffn1_w_gate', 'ffn1_w_up', 'ffn1_w_down', 'ffn2_w_gate', 'ffn2_w_up', 'ffn2_w_down', 's5_glu_w', 'w_out'):
        p[k] = w[k][l].astype(BF16)
    p['w_main'], p['w_gate'] = w_main, w_gate
    p['na_bias'] = _na_bias_table(w['na_rpb'][l])
    p['s5'] = _s5_params(w['s5_lambda_re'][l], w['s5_lambda_im'][l], w['s5_log_step'][l], w['s5_b_re'][l],
                         w['s5_b_im'][l], w['s5_c_re'][l], w['s5_c_im'][l], w['s5_d'][l])
    return p


def _mixers(x, p, B, T, layer_idx, rope_tabs):
    proj, gates, su = _inproj(x, p['mix_norm'], p['w_main'], p['w_gate'])
    qc, kc = _mlstm_conv(proj, p['mlstm_conv_w'], p['mlstm_conv_b'], T)
    hf, hb = _mlstm(qc, kc, proj, gates, p['mlstm_gate_bias'], B, T)
    qr, kr = _rope(proj, rope_tabs[0], rope_tabs[1], T)
    yb = _diff_attn(qr, kr, proj, p['diff_lambda'], p['diff_norm'], B, T, layer_idx)
    yc = _na_attn(proj, p['na_bias'], B, T)
    ys = _s5(su, p['s5'], B, T)
    return _outproj(x, hf, hb, proj, p['mlstm_norm'], yb, yc, ys, p['s5_glu_w'], p['s5_glu_b'], p['w_out'])


def _trunk(x, layers, final_norm):
    B, T, D = x.shape
    x = x.reshape(B * T, D)
    rope_tabs = _rope_tables(T)
    for l, p in enumerate(layers):
        x = _ffn(x, p['ffn1_norm'], p['ffn1_w_gate'], p['ffn1_w_up'], p['ffn1_w_down'])
        x = _mixers(x, p, B, T, l, rope_tabs)
        last = l == len(layers) - 1
        x = _ffn(x, p['ffn2_norm'], p['ffn2_w_gate'], p['ffn2_w_up'], p['ffn2_w_down'],
                 final_g=final_norm if last else None)
    return x.reshape(B, T, D)


def kernel(x_prompt, x_sample, ffn1_norm, ffn1_w_gate, ffn1_w_up, ffn1_w_down, mix_norm, w_in, mlstm_conv_w, mlstm_conv_b, mlstm_gate_bias, mlstm_norm, diff_lambda, diff_norm, na_rpb, s5_lambda_re, s5_lambda_im, s5_log_step, s5_b_re, s5_b_im, s5_c_re, s5_c_im, s5_d, s5_glu_w, s5_glu_b, w_out, ffn2_norm, ffn2_w_gate, ffn2_w_up, ffn2_w_down, final_norm):
    w = dict(ffn1_norm=ffn1_norm, ffn1_w_gate=ffn1_w_gate, ffn1_w_up=ffn1_w_up, ffn1_w_down=ffn1_w_down,
             mix_norm=mix_norm, w_in=w_in, mlstm_conv_w=mlstm_conv_w, mlstm_conv_b=mlstm_conv_b,
             mlstm_gate_bias=mlstm_gate_bias, mlstm_norm=mlstm_norm, diff_lambda=diff_lambda, diff_norm=diff_norm,
             na_rpb=na_rpb, s5_lambda_re=s5_lambda_re, s5_lambda_im=s5_lambda_im, s5_log_step=s5_log_step,
             s5_b_re=s5_b_re, s5_b_im=s5_b_im, s5_c_re=s5_c_re, s5_c_im=s5_c_im, s5_d=s5_d,
             s5_glu_w=s5_glu_w, s5_glu_b=s5_glu_b, w_out=w_out, ffn2_norm=ffn2_norm, ffn2_w_gate=ffn2_w_gate,
             ffn2_w_up=ffn2_w_up, ffn2_w_down=ffn2_w_down)
    layers = [_prepare_layer(w, l) for l in range(ffn1_norm.shape[0])]
    return (_trunk(x_prompt, layers, final_norm), _trunk(x_sample, layers, final_norm))
```

```python
import functools
import math

import jax
import jax.numpy as jnp
from jax import lax
from jax.experimental import pallas as pl
from jax.experimental.pallas import tpu as pltpu

F32 = jnp.float32
BF16 = jnp.bfloat16

NORM_EPS = 1e-6
GROUP_W = 512
GRID_W = 64
MLSTM_HEADS = 4
MLSTM_DH = 128
MLSTM_CONV = 5
MLSTM_CHUNK = 128
MLSTM_CHUNKS_PER_STEP = 4
DIFF_HEADS = 4
DIFF_DH = 64
ROPE_THETA = 10000.0
NA_HEADS = 8
NA_DH = 64
NA_WIN_ROWS = 8
NA_WIN_W = 16
NA_ROWS_PER_STEP = 16
S5_GROUP_CH = 16
S5_GROUPS = 32
S5_STATE = 64
S5_CHUNK = 16
LANES = 128
BF16_SUBLANES = 16
MASK_VALUE = -1e30
VMEM_LIMIT = 56 * 1024 * 1024


def _cparams(*sem):
    return pltpu.CompilerParams(dimension_semantics=sem, vmem_limit_bytes=VMEM_LIMIT)


def _rms(x, g):
    return x * lax.rsqrt(jnp.mean(x * x, axis=-1, keepdims=True) + NORM_EPS) * g


def _sigmoid(x):
    return 1.0 / (1.0 + jnp.exp(-x))


def _dot(a, b):
    return jnp.dot(a, b, preferred_element_type=F32)


def _dot_nt(a, b):
    return lax.dot_general(a, b, (((1,), (1,)), ((), ())), preferred_element_type=F32)


def _dot_tn(a, b):
    return lax.dot_general(a, b, (((0,), (0,)), ((), ())), preferred_element_type=F32)


def _split3(x):
    hi = x.astype(BF16)
    r1 = x - hi.astype(F32)
    mid = r1.astype(BF16)
    lo = (r1 - mid.astype(F32)).astype(BF16)
    return hi, mid, lo


def _ffn_kernel(x_ref, g_ref, wg_ref, wu_ref, wd_ref, *rest, final):
    if final:
        fg_ref, o_ref, h_ref = rest
    else:
        o_ref, h_ref = rest
    f = pl.program_id(1)
    nf = pl.num_programs(1)

    @pl.when(f == 0)
    def _():
        h_ref[...] = _rms(x_ref[...], g_ref[...]).astype(BF16)
        o_ref[...] = x_ref[...]

    h = h_ref[...]
    gate = _dot(h, wg_ref[...])
    up = _dot(h, wu_ref[...])
    act = (0.5 * gate * _sigmoid(gate) * up).astype(BF16)
    nchunk = 4
    cw = o_ref.shape[1] // nchunk
    for c in range(nchunk):
        o_ref[:, c * cw:(c + 1) * cw] += _dot(act, wd_ref[:, c * cw:(c + 1) * cw])

    if final:
        @pl.when(f == nf - 1)
        def _():
            o_ref[...] = _rms(o_ref[...], fg_ref[...])


def _ffn_prenormed_kernel(x_hbm, h_ref, wg_ref, wu_ref, wd_ref, *rest, final):
    if final:
        fg_ref, o_ref, sem = rest
    else:
        o_ref, sem = rest
    i = pl.program_id(0)
    f = pl.program_id(1)
    tm = o_ref.shape[0]
    residual = pltpu.make_async_copy(x_hbm.at[pl.ds(i * tm, tm), :], o_ref, sem)

    @pl.when(f == 0)
    def _():
        residual.start()

    h = h_ref[...]
    gate = _dot(h, wg_ref[...])
    up = _dot(h, wu_ref[...])
    act = (0.5 * gate * _sigmoid(gate) * up).astype(BF16)

    @pl.when(f == 0)
    def _():
        residual.wait()

    nchunk = 4
    cw = o_ref.shape[1] // nchunk
    for c in range(nchunk):
        o_ref[:, c * cw:(c + 1) * cw] += _dot(act, wd_ref[:, c * cw:(c + 1) * cw])

    if final:
        @pl.when(f == pl.num_programs(1) - 1)
        def _():
            o_ref[...] = _rms(o_ref[...], fg_ref[...])


def _ffn_prenormed(x, h, wg, wu, wd, final_g=None, tm=1024, tf=512):
    M, D = x.shape
    F = wg.shape[1]
    final = final_g is not None
    in_specs = [
        pl.BlockSpec(memory_space=pl.ANY),
        pl.BlockSpec((tm, D), lambda i, f: (i, 0)),
        pl.BlockSpec((D, tf), lambda i, f: (0, f)),
        pl.BlockSpec((D, tf), lambda i, f: (0, f)),
        pl.BlockSpec((tf, D), lambda i, f: (f, 0)),
    ]
    args = [x, h, wg, wu, wd]
    if final:
        in_specs.append(pl.BlockSpec((1, D), lambda i, f: (0, 0)))
        args.append(final_g.reshape(1, D))
    return pl.pallas_call(
        functools.partial(_ffn_prenormed_kernel, final=final),
        grid=(M // tm, F // tf),
        in_specs=in_specs,
        out_specs=pl.BlockSpec((tm, D), lambda i, f: (i, 0)),
        out_shape=jax.ShapeDtypeStruct((M, D), F32),
        scratch_shapes=[pltpu.SemaphoreType.DMA(())],
        compiler_params=_cparams("parallel", "arbitrary"),
        name="ffn_prenormed",
    )(*args)


def _ffn(x, g, wg, wu, wd, final_g=None, tm=1024, tf=512):
    M, D = x.shape
    F = wg.shape[1]
    final = final_g is not None
    in_specs = [
        pl.BlockSpec((tm, D), lambda i, f: (i, 0), pipeline_mode=pl.Buffered(1)),
        pl.BlockSpec((1, D), lambda i, f: (0, 0)),
        pl.BlockSpec((D, tf), lambda i, f: (0, f)),
        pl.BlockSpec((D, tf), lambda i, f: (0, f)),
        pl.BlockSpec((tf, D), lambda i, f: (f, 0)),
    ]
    args = [x, g.reshape(1, D), wg, wu, wd]
    if final:
        in_specs.append(pl.BlockSpec((1, D), lambda i, f: (0, 0)))
        args.append(final_g.reshape(1, D))
    return pl.pallas_call(
        functools.partial(_ffn_kernel, final=final),
        grid=(M // tm, F // tf),
        in_specs=in_specs,
        out_specs=pl.BlockSpec((tm, D), lambda i, f: (i, 0)),
        out_shape=jax.ShapeDtypeStruct((M, D), F32),
        scratch_shapes=[pltpu.VMEM((tm, D), BF16)],
        compiler_params=_cparams("parallel", "arbitrary"),
        name="ffn",
    )(*args)


def _inproj_kernel(x_ref, g_ref, w_ref, wgate_ref, o_ref, og_ref, os_ref, h_ref):
    n = pl.program_id(1)

    @pl.when(n == 0)
    def _():
        h = _rms(x_ref[...], g_ref[...]).astype(BF16)
        h_ref[...] = h
        og_ref[...] = _dot(h, wgate_ref[...])

    res = _dot(h_ref[...], w_ref[...])
    o_ref[...] = res.astype(BF16)

    @pl.when(n == pl.num_programs(1) - 1)
    def _():
        os_ref[...] = res[:, res.shape[1] - GROUP_W:]


def _inproj(x, g, w_main, w_gate, tm=1024, tn=1408):
    M, D = x.shape
    N = w_main.shape[1]
    assert tn >= GROUP_W and N % tn == 0
    return pl.pallas_call(
        _inproj_kernel,
        grid=(M // tm, N // tn),
        in_specs=[
            pl.BlockSpec((tm, D), lambda i, n: (i, 0)),
            pl.BlockSpec((1, D), lambda i, n: (0, 0)),
            pl.BlockSpec((D, tn), lambda i, n: (0, n)),
            pl.BlockSpec((D, LANES), lambda i, n: (0, 0)),
        ],
        out_specs=[
            pl.BlockSpec((tm, tn), lambda i, n: (i, n)),
            pl.BlockSpec((tm, LANES), lambda i, n: (i, 0)),
            pl.BlockSpec((tm, GROUP_W), lambda i, n: (i, 0)),
        ],
        out_shape=[
            jax.ShapeDtypeStruct((M, N), BF16),
            jax.ShapeDtypeStruct((M, LANES), F32),
            jax.ShapeDtypeStruct((M, GROUP_W), F32),
        ],
        scratch_shapes=[pltpu.VMEM((tm, D), BF16)],
        compiler_params=_cparams("parallel", "arbitrary"),
        name="inproj",
    )(x, g.reshape(1, D), w_main, w_gate)


def _conv_kernel(x_ref, prev_ref, next_ref, w_ref, b_ref, q_ref, k_ref, *, blocks_per_seq):
    tb = x_ref.shape[0]
    halo = prev_ref.shape[0]
    pos = pl.program_id(0) % blocks_per_seq
    prev = jnp.where(pos != 0, prev_ref[...].astype(F32), 0.0)
    nxt = jnp.where(pos != blocks_per_seq - 1, next_ref[...].astype(F32), 0.0)
    ext = jnp.concatenate([prev, x_ref[...].astype(F32), nxt], axis=0)
    rows = tb + 2 * halo
    pad = MLSTM_CONV // 2
    acc = None
    for j in range(MLSTM_CONV):
        shift = (pad - j) % rows
        sh = ext if shift == 0 else pltpu.roll(ext, shift, axis=0)
        term = sh[halo:halo + tb] * w_ref[j:j + 1, :]
        acc = term if acc is None else acc + term
    acc = acc + b_ref[...]
    y = acc * _sigmoid(acc)
    q_ref[...] = y[:, :GROUP_W].astype(BF16)
    k_ref[...] = (y[:, GROUP_W:] * (MLSTM_DH ** -0.5)).astype(BF16)


def _mlstm_conv(proj, conv_w, conv_b, T, tb=512):
    M = proj.shape[0]
    C = 2 * GROUP_W
    halo = BF16_SUBLANES
    r = tb // halo
    nhalo = M // halo
    w = jnp.zeros((8, C), F32).at[:MLSTM_CONV].set(conv_w)
    return pl.pallas_call(
        functools.partial(_conv_kernel, blocks_per_seq=T // tb),
        grid=(M // tb,),
        in_specs=[
            pl.BlockSpec((tb, C), lambda i: (i, 0)),
            pl.BlockSpec((halo, C), lambda i: (jnp.maximum(i * r - 1, 0), 0)),
            pl.BlockSpec((halo, C), lambda i: (jnp.minimum((i + 1) * r, nhalo - 1), 0)),
            pl.BlockSpec((8, C), lambda i: (0, 0)),
            pl.BlockSpec((1, C), lambda i: (0, 0)),
        ],
        out_specs=[
            pl.BlockSpec((tb, GROUP_W), lambda i: (i, 0)),
            pl.BlockSpec((tb, GROUP_W), lambda i: (i, 0)),
        ],
        out_shape=[jax.ShapeDtypeStruct((M, GROUP_W), BF16)] * 2,
        compiler_params=_cparams("parallel"),
        name="mlstm_conv",
    )(proj, proj, proj, w, conv_b.reshape(1, C))


def _mlstm_kernel(qf_ref, kf_ref, vf_ref, gf_ref, qb_ref, kb_ref, vb_ref, gb_ref, bias_ref,
                  of_ref, ob_ref, c_ref, m_ref):
    L = MLSTM_CHUNK
    nsub = qf_ref.shape[0] // L
    H, dh = MLSTM_HEADS, MLSTM_DH

    @pl.when(pl.program_id(1) == 0)
    def _():
        c_ref[...] = jnp.zeros_like(c_ref)
        m_ref[...] = jnp.zeros_like(m_ref)

    row = lax.broadcasted_iota(jnp.int32, (L, L), 0)
    col = lax.broadcasted_iota(jnp.int32, (L, L), 1)
    ones_blk = jnp.ones((L, dh), BF16)

    dirs = ((qf_ref, kf_ref, vf_ref, gf_ref, of_ref), (qb_ref, kb_ref, vb_ref, gb_ref, ob_ref))
    visits = [(d, (j if d == 0 else nsub - 1 - j) * L) for j in range(nsub) for d in range(2)]
    gate_terms = []
    for d, r0 in visits:
        seen = (col <= row) if d == 0 else (col >= row)
        tri = jnp.where(seen, 1.0, 0.0).astype(BF16)
        gt = dirs[d][3][r0:r0 + L, :] + bias_ref[...]
        lf = jnp.minimum(gt, 0.0) - jnp.log(1.0 + jnp.exp(-jnp.abs(gt)))
        hi, mid, lo = _split3(lf)
        bcum = _dot(tri, hi) + _dot(tri, mid) + _dot(tri, lo)
        gate_terms.append((seen, gt, bcum, bcum.T, gt.T))

    for (d, r0), (seen, gt, bcum, bcum_t, gt_t) in zip(visits, gate_terms):
        q_ref, k_ref, v_ref, _, o_ref = dirs[d]
        rows = slice(r0, r0 + L)
        end = L - 1 if d == 0 else 0
        for h in range(H):
            ic = 8 * d + h
            fc = 8 * d + 4 + h
            idx = d * H + h
            b_rep = jnp.broadcast_to(bcum[:, fc:fc + 1], (L, dh))
            i_rep = jnp.broadcast_to(gt[:, ic:ic + 1], (L, dh))
            b_row = bcum_t[fc:fc + 1, :]
            i_row = gt_t[ic:ic + 1, :]
            g_tot = b_rep[end:end + 1, :]
            m_prev = m_ref[idx]
            c_prev = c_ref[idx]
            q = q_ref[rows, h * dh:(h + 1) * dh]
            k = k_ref[rows, h * dh:(h + 1) * dh]
            v_aug = jnp.concatenate([v_ref[rows, h * dh:(h + 1) * dh], ones_blk], axis=1)

            dlog = jnp.where(seen, b_rep - b_row + i_row, -jnp.inf)
            inter = b_rep + m_prev
            m_t = jnp.maximum(inter, jnp.max(dlog, axis=1, keepdims=True))
            s = (_dot_nt(q, k) * jnp.exp(dlog - m_t)).astype(BF16)
            s_inter = jnp.exp(inter - m_t)
            num_aug = jnp.concatenate([s_inter, s_inter], axis=1) * _dot(q, c_prev.astype(BF16)) + _dot(s, v_aug)
            den = jnp.maximum(jnp.abs(num_aug[:, dh:]), jnp.exp(-m_t))
            o_ref[rows, h * dh:(h + 1) * dh] = num_aug[:, :dh] / den

            a_rep = g_tot - b_rep + i_rep
            m_new = jnp.maximum(g_tot + m_prev, jnp.max(a_rep, axis=0, keepdims=True))
            decay = jnp.exp(g_tot + m_prev - m_new)
            kw = (k.astype(F32) * jnp.exp(a_rep - m_new)).astype(BF16)
            c_ref[idx] = jnp.concatenate([decay, decay], axis=1) * c_prev + _dot_tn(kw, v_aug)
            m_ref[idx] = m_new


def _mlstm(qc, kc, proj, gates, gate_bias, B, T):
    M = qc.shape[0]
    L = MLSTM_CHUNK
    assert L == LANES == MLSTM_DH
    tb = MLSTM_CHUNKS_PER_STEP * L
    nc = T // tb
    W = GROUP_W
    vblk = 2
    fwd = lambda b, i: (b * nc + i, 0)
    bwd = lambda b, i: (b * nc + nc - 1 - i, 0)
    fwd_v = lambda b, i: (b * nc + i, vblk)
    bwd_v = lambda b, i: (b * nc + nc - 1 - i, vblk)
    bias = jnp.zeros((1, LANES), F32).at[0, :4 * MLSTM_HEADS].set(gate_bias)
    return pl.pallas_call(
        _mlstm_kernel,
        grid=(B, nc),
        in_specs=[
            pl.BlockSpec((tb, W), fwd), pl.BlockSpec((tb, W), fwd), pl.BlockSpec((tb, W), fwd_v),
            pl.BlockSpec((tb, LANES), fwd),
            pl.BlockSpec((tb, W), bwd), pl.BlockSpec((tb, W), bwd), pl.BlockSpec((tb, W), bwd_v),
            pl.BlockSpec((tb, LANES), bwd),
            pl.BlockSpec((1, LANES), lambda b, i: (0, 0)),
        ],
        out_specs=[pl.BlockSpec((tb, W), fwd), pl.BlockSpec((tb, W), bwd)],
        out_shape=[jax.ShapeDtypeStruct((M, W), F32)] * 2,
        scratch_shapes=[
            pltpu.VMEM((2 * MLSTM_HEADS, MLSTM_DH, 2 * MLSTM_DH), F32),
            pltpu.VMEM((2 * MLSTM_HEADS, 1, LANES), F32),
        ],
        compiler_params=_cparams("parallel", "arbitrary"),
        name="mlstm",
    )(qc, kc, proj, gates, qc, kc, proj, gates, bias)


def _rope_kernel(x_ref, cos_ref, sin_ref, q_ref, k_ref):
    x = x_ref[...].astype(F32)
    reps = x.shape[1] // LANES
    cos = jnp.concatenate([cos_ref[...]] * reps, axis=1)
    sin = jnp.concatenate([sin_ref[...]] * reps, axis=1)
    lane = lax.broadcasted_iota(jnp.int32, x.shape, 1)
    width = x.shape[1]
    half = DIFF_DH // 2
    rot = jnp.where(lane % DIFF_DH < half, pltpu.roll(x, width - half, axis=1), pltpu.roll(x, half, axis=1))
    y = x * cos + rot * sin
    q_ref[...] = (y[:, :GROUP_W] * (DIFF_DH ** -0.5 * math.log2(math.e))).astype(BF16)
    k_ref[...] = y[:, GROUP_W:].astype(BF16)


def _rope_tables(T):
    half = DIFF_DH // 2
    inv = 1.0 / (ROPE_THETA ** (jnp.arange(0, DIFF_DH, 2, dtype=F32) / DIFF_DH))
    ang = jnp.arange(T, dtype=F32)[:, None] * inv[None, :]
    cos, sin = jnp.cos(ang), jnp.sin(ang)
    cos_t = jnp.concatenate([cos, cos] * (LANES // DIFF_DH), axis=1)
    sin_t = jnp.concatenate([-sin, sin] * (LANES // DIFF_DH), axis=1)
    return cos_t, sin_t


def _rope(proj, cos_t, sin_t, T, tb=256):
    M = proj.shape[0]
    nb = T // tb
    return pl.pallas_call(
        _rope_kernel,
        grid=(M // tb,),
        in_specs=[
            pl.BlockSpec((tb, 2 * GROUP_W), lambda i: (i, 2)),
            pl.BlockSpec((tb, LANES), lambda i: (i % nb, 0)),
            pl.BlockSpec((tb, LANES), lambda i: (i % nb, 0)),
        ],
        out_specs=[pl.BlockSpec((tb, GROUP_W), lambda i: (i, 0))] * 2,
        out_shape=[jax.ShapeDtypeStruct((M, GROUP_W), BF16)] * 2,
        compiler_params=_cparams("parallel"),
        name="rope",
    )(proj, cos_t, sin_t)


def _diff_kernel(q_ref, k_ref, v_ref, lp_ref, g_ref, o_ref, q2_ref, m_ref, acc_ref, *, lam_init, tks):
    ki = pl.program_id(3)
    tq = q_ref.shape[0]
    dv = v_ref.shape[1]
    ones = jnp.ones((tks, dv), BF16)

    @pl.when(ki == 0)
    def _():
        q = q_ref[...]
        lane = lax.broadcasted_iota(jnp.int32, q.shape, 1)
        zero = jnp.zeros_like(q)
        q2_ref[0:tq, :] = jnp.where(lane < DIFF_DH, q, zero)
        q2_ref[tq:2 * tq, :] = jnp.where(lane < DIFF_DH, zero, q)
        m_ref[...] = jnp.full_like(m_ref, -jnp.inf)
        acc_ref[...] = jnp.zeros_like(acc_ref)

    q2 = q2_ref[...]
    for j in range(k_ref.shape[0] // tks):
        s = _dot_nt(q2, k_ref[j * tks:(j + 1) * tks, :])
        tiles = [s[:, c * LANES:(c + 1) * LANES] for c in range(tks // LANES)]
        m_prev = m_ref[...]
        m_new = jnp.maximum(m_prev, jnp.max(functools.reduce(jnp.maximum, tiles), axis=1, keepdims=True))
        alpha = jnp.exp2(m_prev - m_new)
        p = jnp.concatenate([jnp.exp2(t - m_new).astype(BF16) for t in tiles], axis=1)
        v_aug = jnp.concatenate([v_ref[j * tks:(j + 1) * tks, :], ones], axis=1)
        acc_ref[...] = jnp.concatenate([alpha, alpha], axis=1) * acc_ref[...] + _dot(p, v_aug)
        m_ref[...] = m_new

    @pl.when(ki == pl.num_programs(3) - 1)
    def _():
        lp = lp_ref[...]
        lam = (jnp.exp(jnp.sum(lp[0:1] * lp[1:2], axis=1, keepdims=True))
               - jnp.exp(jnp.sum(lp[2:3] * lp[3:4], axis=1, keepdims=True)) + lam_init)
        o = acc_ref[:, 0:dv] / acc_ref[:, dv:2 * dv]
        out = o[0:tq] - lam * o[tq:2 * tq]
        o_ref[...] = (_rms(out, g_ref[...]) * (1.0 - lam_init)).astype(BF16)


def _diff_attn(qr, kr, proj, lam_params, norm_g, B, T, layer_idx, tq=1024, tk=4096, tks=512):
    M = qr.shape[0]
    dv = 2 * DIFF_DH
    tk = min(tk, T)
    nq, nk = T // tq, T // tk
    vblk0 = (6 * GROUP_W) // dv
    lam_init = 0.8 - 0.6 * math.exp(-0.3 * layer_idx)
    return pl.pallas_call(
        functools.partial(_diff_kernel, lam_init=lam_init, tks=tks),
        grid=(B, DIFF_HEADS, nq, nk),
        in_specs=[
            pl.BlockSpec((tq, dv), lambda b, h, qi, ki: (b * nq + qi, h)),
            pl.BlockSpec((tk, dv), lambda b, h, qi, ki: (b * nk + ki, h)),
            pl.BlockSpec((tk, dv), lambda b, h, qi, ki: (b * nk + ki, vblk0 + h)),
            pl.BlockSpec((4, DIFF_DH), lambda b, h, qi, ki: (0, 0)),
            pl.BlockSpec((1, dv), lambda b, h, qi, ki: (0, 0)),
        ],
        out_specs=pl.BlockSpec((tq, dv), lambda b, h, qi, ki: (b * nq + qi, h)),
        out_shape=jax.ShapeDtypeStruct((M, GROUP_W), BF16),
        scratch_shapes=[
            pltpu.VMEM((2 * tq, dv), BF16),
            pltpu.VMEM((2 * tq, LANES), F32),
            pltpu.VMEM((2 * tq, 2 * dv), F32),
        ],
        compiler_params=_cparams("parallel", "parallel", "parallel", "arbitrary"),
        name="diff_attn",
    )(qr, kr, proj, lam_params, norm_g.reshape(1, dv))


def _na_kernel(q_ref, k_ref, v_ref, bias_ref, o_ref, *, rows):
    g = pl.program_id(2)
    win = NA_WIN_ROWS * GRID_W
    lane = lax.broadcasted_iota(jnp.int32, (GRID_W, 2 * NA_DH), 1)
    first = lane < NA_DH
    w = 2 * NA_DH
    ones = jnp.ones((win, w), BF16)
    scores, starts = [], []
    for jr in range(NA_ROWS_PER_STEP):
        r = g * NA_ROWS_PER_STEP + jr
        rs = jnp.clip(r - NA_WIN_ROWS // 2, 0, rows - NA_WIN_ROWS)
        start = pl.multiple_of(rs * GRID_W, GRID_W)
        q = q_ref[jr * GRID_W:(jr + 1) * GRID_W, :]
        zero = jnp.zeros_like(q)
        q2 = jnp.concatenate([jnp.where(first, q, zero), jnp.where(first, zero, q)], axis=0)
        s = _dot_nt(q2, k_ref[pl.ds(start, win), :]) * (NA_DH ** -0.5 * math.log2(math.e)) + bias_ref[r - rs]
        scores.append(s)
        starts.append(start)
    for jr in range(NA_ROWS_PER_STEP):
        s = scores[jr]
        e = jnp.exp2(s - jnp.max(s, axis=1, keepdims=True)).astype(BF16)
        pv = _dot(e, jnp.concatenate([v_ref[pl.ds(starts[jr], win), :], ones], axis=1))
        pv = pv[:, :w] / pv[:, w:]
        o_ref[jr * GRID_W:(jr + 1) * GRID_W, :] = jnp.where(first, pv[:GRID_W], pv[GRID_W:]).astype(BF16)


def _na_bias_table(rpb):
    c = jnp.arange(GRID_W)
    col_start = jnp.clip(c - NA_WIN_W // 2, 0, GRID_W - NA_WIN_W)
    kc = jnp.arange(GRID_W)
    valid = (kc[None, :] >= col_start[:, None]) & (kc[None, :] < col_start[:, None] + NA_WIN_W)
    col_off = kc[None, :] - c[:, None] + NA_WIN_W - 1
    p = jnp.arange(NA_WIN_ROWS)
    k = jnp.arange(NA_WIN_ROWS)
    row_off = k[None, :] - p[:, None] + NA_WIN_ROWS - 1
    pick_row = (row_off[..., None] == jnp.arange(2 * NA_WIN_ROWS - 1)).astype(F32)
    pick_col = (col_off[..., None] == jnp.arange(2 * NA_WIN_W - 1)).astype(F32)
    t = jnp.einsum('hrq,pkr,cjq->hpckj', rpb.astype(F32), pick_row, pick_col,
                   precision=lax.Precision.HIGHEST) * math.log2(math.e)
    t = jnp.where(valid[None, None, :, None, :], t, MASK_VALUE)
    t = t.reshape(NA_HEADS, NA_WIN_ROWS, GRID_W, NA_WIN_ROWS * GRID_W)
    t = t.reshape(NA_HEADS // 2, 2, NA_WIN_ROWS, GRID_W, NA_WIN_ROWS * GRID_W).transpose(0, 2, 1, 3, 4)
    return t.reshape(NA_HEADS // 2, NA_WIN_ROWS, 2 * GRID_W, NA_WIN_ROWS * GRID_W)


def _na_attn(proj, bias_table, B, T):
    M = proj.shape[0]
    rows = T // GRID_W
    tq = NA_ROWS_PER_STEP * GRID_W
    ng = rows // NA_ROWS_PER_STEP
    hp = NA_HEADS // 2
    w = 2 * NA_DH
    qblk0, kblk0, vblk0 = (7 * GROUP_W) // w, (8 * GROUP_W) // w, (9 * GROUP_W) // w
    return pl.pallas_call(
        functools.partial(_na_kernel, rows=rows),
        grid=(B, hp, ng),
        in_specs=[
            pl.BlockSpec((tq, w), lambda b, h, g: (b * ng + g, qblk0 + h)),
            pl.BlockSpec((T, w), lambda b, h, g: (b, kblk0 + h)),
            pl.BlockSpec((T, w), lambda b, h, g: (b, vblk0 + h)),
            pl.BlockSpec((None, NA_WIN_ROWS, 2 * GRID_W, NA_WIN_ROWS * GRID_W), lambda b, h, g: (h, 0, 0, 0)),
        ],
        out_specs=pl.BlockSpec((tq, w), lambda b, h, g: (b * ng + g, h)),
        out_shape=jax.ShapeDtypeStruct((M, GROUP_W), BF16),
        compiler_params=_cparams("parallel", "parallel", "arbitrary"),
        name="na_attn",
    )(proj, proj, proj, bias_table)


def _s5_params(lam_re, lam_im, log_step, b_re, b_im, c_re, c_im, d_skip):
    L, c, P, G = S5_CHUNK, S5_GROUP_CH, S5_STATE, S5_GROUPS
    K8 = LANES // c
    O = G // K8
    lam = lax.complex(lam_re.astype(F32), lam_im.astype(F32))
    step = jnp.exp(log_step.astype(F32))[..., None]
    log_lam_bar = lam * step
    lam_bar = jnp.exp(log_lam_bar)
    b_mat = lax.complex(b_re.astype(F32), b_im.astype(F32))
    c_mat = lax.complex(c_re.astype(F32), c_im.astype(F32))
    b_bar = ((lam_bar - 1.0) / lam)[..., None] * b_mat[None]
    pw = jnp.exp(log_lam_bar[:, None] * jnp.arange(L + 1, dtype=F32)[None, :, None, None])
    cb = c_mat[None, :, :, None, :] * b_bar.transpose(0, 1, 3, 2)[:, :, None, :, :]
    s_i = jnp.arange(L)[:, None]
    t_i = jnp.arange(L)[None, :]
    lag = jnp.stack([t_i - s_i, s_i - t_i])
    lag_pw = jnp.where((lag >= 0)[..., None, None],
                       jnp.exp(log_lam_bar[:, None, None] * jnp.maximum(lag, 0).astype(F32)[..., None, None]),
                       0.0)
    m = jnp.einsum('dgoip,dstgp->gsito', jnp.concatenate([cb.real, -cb.imag], axis=-1),
                   jnp.concatenate([lag_pw.real, lag_pw.imag], axis=-1), precision=lax.Precision.HIGH)
    eye = jnp.eye(K8, dtype=F32)
    lane_of = jnp.arange(K8)[:, None] * c + jnp.arange(c)[None, :]
    sel = (jnp.arange(LANES)[None, None, :] == lane_of[:, :, None]).astype(BF16)
    m8 = jnp.einsum('agsitc,gcl->asgitl', m.astype(BF16).reshape(O, K8, L, c, L, c), sel,
                    preferred_element_type=BF16).reshape(O, L * LANES, L * LANES)
    ef = pw[0][L - 1 - jnp.arange(L)][:, :, :, None] * b_bar[0][None]
    eb = pw[1][jnp.arange(L)][:, :, :, None] * b_bar[1][None]
    to_rows = lambda e: e.transpose(1, 0, 3, 2)
    ef, eb = to_rows(ef), to_rows(eb)
    bs = jnp.concatenate([ef.real, ef.imag, eb.real, eb.imag], axis=-1).astype(BF16)
    bs8 = jnp.einsum('agsik,gh->asgihk', bs.reshape(O, K8, L, c, 4 * P), eye.astype(BF16),
                     preferred_element_type=BF16).reshape(O, L * LANES, K8 * 4 * P)
    of = c_mat[:, None] * pw[0][1:L + 1].transpose(1, 0, 2)[:, :, None, :]
    ob = c_mat[:, None] * pw[1][L - jnp.arange(L)].transpose(1, 0, 2)[:, :, None, :]
    to_cols = lambda o: o.transpose(0, 3, 1, 2)
    of, ob = to_cols(of), to_cols(ob)
    cs = jnp.concatenate([of.real, -of.imag, ob.real, -ob.imag], axis=1)
    cs8 = jnp.einsum('agktc,gcl->agktl', cs.astype(BF16).reshape(O, K8, 4 * P, L, c), sel,
                     preferred_element_type=BF16).reshape(O, K8 * 4 * P, L * LANES)
    al = pw[:, L]
    a1 = jnp.concatenate([al.real, al.real], axis=-1)
    a2 = jnp.concatenate([-al.imag, al.imag], axis=-1)
    a8 = jnp.stack([a1, a2, -a2], axis=1)
    a8 = a8.reshape(2, 3, O, K8, LANES).transpose(2, 0, 1, 3, 4).reshape(O, 6 * K8, LANES)
    d8 = d_skip.astype(F32).reshape(O, 1, LANES)
    return m8.astype(BF16), bs8.astype(BF16), cs8.astype(BF16), a8, d8


def _s5_kernel(u_ref, m_ref, bs_ref, cs_ref, a_ref, d_ref, y_ref, x_ref, sf_ref, sb_ref, sfs_ref, sbs_ref):
    L = S5_CHUNK
    K8 = LANES // S5_GROUP_CH
    sw = 4 * S5_STATE
    nc = u_ref.shape[0] // L
    for s in range(L):
        x_ref[:, s * LANES:(s + 1) * LANES] = u_ref[pl.ds(s, nc, stride=L), :].astype(BF16)
    x = x_ref[...]
    for g in range(K8):
        sg = _dot(x, bs_ref[:, g * sw:(g + 1) * sw])
        sf_ref[g * nc:(g + 1) * nc, :] = sg[:, 0:LANES]
        sb_ref[g * nc:(g + 1) * nc, :] = sg[:, LANES:2 * LANES]
        sfs_ref[g * nc:(g + 1) * nc, :] = pltpu.roll(sg[:, 0:LANES], LANES // 2, axis=1)
        sbs_ref[g * nc:(g + 1) * nc, :] = pltpu.roll(sg[:, LANES:2 * LANES], LANES // 2, axis=1)
    a1f, a2f, a2fs = a_ref[0:K8], a_ref[K8:2 * K8], a_ref[2 * K8:3 * K8]
    a1b, a2b, a2bs = a_ref[3 * K8:4 * K8], a_ref[4 * K8:5 * K8], a_ref[5 * K8:6 * K8]

    def step(c, carry):
        hf, hfs, hb, hbs = carry
        cb = nc - 1 - c
        fwd = pl.ds(c, K8, stride=nc)
        bwd = pl.ds(cb, K8, stride=nc)
        sf, sfs = sf_ref[fwd, :], sfs_ref[fwd, :]
        sb, sbs = sb_ref[bwd, :], sbs_ref[bwd, :]
        sf_ref[fwd, :] = hf
        sb_ref[bwd, :] = hb
        return (a1f * hf + a2f * hfs + sf, a1f * hfs + a2fs * hf + sfs,
                a1b * hb + a2b * hbs + sb, a1b * hbs + a2bs * hb + sbs)

    z = jnp.zeros((K8, LANES), F32)
    lax.fori_loop(0, nc, step, (z, z, z, z), unroll=8)
    y = _dot(x, m_ref[...])
    for g in range(K8):
        hg = jnp.concatenate([sf_ref[g * nc:(g + 1) * nc, :], sb_ref[g * nc:(g + 1) * nc, :]], axis=1)
        y = y + _dot(hg.astype(BF16), cs_ref[g * sw:(g + 1) * sw, :])
    for t in range(L):
        y_ref[pl.ds(t, nc, stride=L), :] = y[:, t * LANES:(t + 1) * LANES]
    y_ref[...] += u_ref[...] * d_ref[...]


def _s5(su, params, B, T):
    m8, bs8, cs8, a8, d8 = params
    L = S5_CHUNK
    nc = T // L
    O = m8.shape[0]
    kw = L * LANES
    sw = 4 * S5_STATE
    K8 = LANES // S5_GROUP_CH
    one = pl.Buffered(1)
    return pl.pallas_call(
        _s5_kernel,
        grid=(O, B),
        in_specs=[
            pl.BlockSpec((T, LANES), lambda o, b: (b, o), pipeline_mode=one),
            pl.BlockSpec((None, kw, kw), lambda o, b: (o, 0, 0), pipeline_mode=one),
            pl.BlockSpec((None, kw, K8 * sw), lambda o, b: (o, 0, 0), pipeline_mode=one),
            pl.BlockSpec((None, K8 * sw, kw), lambda o, b: (o, 0, 0), pipeline_mode=one),
            pl.BlockSpec((None, 6 * K8, LANES), lambda o, b: (o, 0, 0)),
            pl.BlockSpec((None, 1, LANES), lambda o, b: (o, 0, 0)),
        ],
        out_specs=pl.BlockSpec((T, LANES), lambda o, b: (b, o)),
        out_shape=jax.ShapeDtypeStruct((B * T, GROUP_W), F32),
        scratch_shapes=[pltpu.VMEM((nc, kw), BF16)] + [pltpu.VMEM((K8 * nc, LANES), F32)] * 4,
        compiler_params=_cparams("arbitrary", "arbitrary"),
        name="s5",
    )(su, m8, bs8, cs8, a8, d8)


def _outproj_kernel(x_ref, hf_ref, hb_ref, og_ref, mg_ref, yb_ref, yc_ref, ys_ref, gw_ref, gb_ref, w_ref, ng_ref,
                    o_ref, on_ref):
    W = GROUP_W
    h = hf_ref[...] + hb_ref[...]
    parts = []
    for j in range(MLSTM_HEADS):
        hj = h[:, j * MLSTM_DH:(j + 1) * MLSTM_DH]
        parts.append(hj * lax.rsqrt(jnp.mean(hj * hj, axis=-1, keepdims=True) + NORM_EPS))
    hn = jnp.concatenate(parts, axis=1) * mg_ref[...]
    ya = (hn * _sigmoid(og_ref[...].astype(F32))).astype(BF16)
    y = ys_ref[...]
    z = 0.5 * y * (1.0 + jnp.tanh(math.sqrt(2.0 / math.pi) * (y + 0.044715 * (y * y * y))))
    yd = (z * _sigmoid(_dot(z.astype(BF16), gw_ref[...]) + gb_ref[...])).astype(BF16)
    acc = x_ref[...] + _dot(ya, w_ref[0:W, :])
    acc = acc + _dot(yb_ref[...], w_ref[W:2 * W, :])
    acc = acc + _dot(yc_ref[...], w_ref[2 * W:3 * W, :])
    acc = acc + _dot(yd, w_ref[3 * W:4 * W, :])
    o_ref[...] = acc
    on_ref[...] = _rms(acc, ng_ref[...]).astype(BF16)


def _outproj(x, hf, hb, proj, mlstm_norm, yb, yc, ys, glu_w, glu_b, w_out, next_norm, tm=512):
    M, D = x.shape
    W = GROUP_W
    tok = lambda i: (i, 0)
    const = lambda i: (0, 0)
    return pl.pallas_call(
        _outproj_kernel,
        grid=(M // tm,),
        in_specs=[
            pl.BlockSpec((tm, D), tok),
            pl.BlockSpec((tm, W), tok), pl.BlockSpec((tm, W), tok),
            pl.BlockSpec((tm, W), lambda i: (i, 3)),
            pl.BlockSpec((1, W), const),
            pl.BlockSpec((tm, W), tok), pl.BlockSpec((tm, W), tok), pl.BlockSpec((tm, W), tok),
            pl.BlockSpec((W, W), const), pl.BlockSpec((1, W), const),
            pl.BlockSpec((4 * W, D), const),
            pl.BlockSpec((1, D), const),
        ],
        out_specs=[pl.BlockSpec((tm, D), tok), pl.BlockSpec((tm, D), tok)],
        out_shape=[jax.ShapeDtypeStruct((M, D), F32), jax.ShapeDtypeStruct((M, D), BF16)],
        compiler_params=_cparams("parallel"),
        name="outproj",
    )(x, hf, hb, proj, mlstm_norm.reshape(1, W), yb, yc, ys, glu_w, glu_b.reshape(1, W), w_out,
      next_norm.reshape(1, D))


def _prepare_layer(w, l):
    W = GROUP_W
    ng = 4 * MLSTM_HEADS
    w_in = w['w_in'][l]
    w_main = jnp.concatenate([w_in[:, :4 * W], w_in[:, 4 * W + ng:]], axis=1).astype(BF16)
    w_gate = jnp.zeros((w_in.shape[0], LANES), BF16).at[:, :ng].set(w_in[:, 4 * W:4 * W + ng].astype(BF16))
    p = {k: w[k][l] for k in ('ffn1_norm', 'mix_norm', 'ffn2_norm', 'mlstm_conv_w', 'mlstm_conv_b', 'mlstm_gate_bias',
                              'mlstm_norm', 'diff_lambda', 'diff_norm', 's5_glu_b')}
    for k in ('ffn1_w_gate', 'ffn1_w_up', 'ffn1_w_down', 'ffn2_w_gate', 'ffn2_w_up', 'ffn2_w_down', 's5_glu_w', 'w_out'):
        p[k] = w[k][l].astype(BF16)
    p['w_main'], p['w_gate'] = w_main, w_gate
    p['na_bias'] = _na_bias_table(w['na_rpb'][l])
    p['s5'] = _s5_params(w['s5_lambda_re'][l], w['s5_lambda_im'][l], w['s5_log_step'][l], w['s5_b_re'][l],
                         w['s5_b_im'][l], w['s5_c_re'][l], w['s5_c_im'][l], w['s5_d'][l])
    return p


def _mixers(x, p, B, T, layer_idx, rope_tabs):
    proj, gates, su = _inproj(x, p['mix_norm'], p['w_main'], p['w_gate'])
    qc, kc = _mlstm_conv(proj, p['mlstm_conv_w'], p['mlstm_conv_b'], T)
    hf, hb = _mlstm(qc, kc, proj, gates, p['mlstm_gate_bias'], B, T)
    qr, kr = _rope(proj, rope_tabs[0], rope_tabs[1], T)
    yb = _diff_attn(qr, kr, proj, p['diff_lambda'], p['diff_norm'], B, T, layer_idx)
    yc = _na_attn(proj, p['na_bias'], B, T)
    ys = _s5(su, p['s5'], B, T)
    return _outproj(x, hf, hb, proj, p['mlstm_norm'], yb, yc, ys, p['s5_glu_w'], p['s5_glu_b'], p['w_out'],
                    p['ffn2_norm'])


def _trunk(x, layers, final_norm):
    B, T, D = x.shape
    x = x.reshape(B * T, D)
    rope_tabs = _rope_tables(T)
    for l, p in enumerate(layers):
        x = _ffn(x, p['ffn1_norm'], p['ffn1_w_gate'], p['ffn1_w_up'], p['ffn1_w_down'])
        x, h = _mixers(x, p, B, T, l, rope_tabs)
        last = l == len(layers) - 1
        x = _ffn_prenormed(x, h, p['ffn2_w_gate'], p['ffn2_w_up'], p['ffn2_w_down'],
                           final_g=final_norm if last else None)
    return x.reshape(B, T, D)


def kernel(x_prompt, x_sample, ffn1_norm, ffn1_w_gate, ffn1_w_up, ffn1_w_down, mix_norm, w_in, mlstm_conv_w, mlstm_conv_b, mlstm_gate_bias, mlstm_norm, diff_lambda, diff_norm, na_rpb, s5_lambda_re, s5_lambda_im, s5_log_step, s5_b_re, s5_b_im, s5_c_re, s5_c_im, s5_d, s5_glu_w, s5_glu_b, w_out, ffn2_norm, ffn2_w_gate, ffn2_w_up, ffn2_w_down, final_norm):
    w = dict(ffn1_norm=ffn1_norm, ffn1_w_gate=ffn1_w_gate, ffn1_w_up=ffn1_w_up, ffn1_w_down=ffn1_w_down,
             mix_norm=mix_norm, w_in=w_in, mlstm_conv_w=mlstm_conv_w, mlstm_conv_b=mlstm_conv_b,
             mlstm_gate_bias=mlstm_gate_bias, mlstm_norm=mlstm_norm, diff_lambda=diff_lambda, diff_norm=diff_norm,
             na_rpb=na_rpb, s5_lambda_re=s5_lambda_re, s5_lambda_im=s5_lambda_im, s5_log_step=s5_log_step,
             s5_b_re=s5_b_re, s5_b_im=s5_b_im, s5_c_re=s5_c_re, s5_c_im=s5_c_im, s5_d=s5_d,
             s5_glu_w=s5_glu_w, s5_glu_b=s5_glu_b, w_out=w_out, ffn2_norm=ffn2_norm, ffn2_w_gate=ffn2_w_gate,
             ffn2_w_up=ffn2_w_up, ffn2_w_down=ffn2_w_down)
    layers = [_prepare_layer(w, l) for l in range(ffn1_norm.shape[0])]
    return (_trunk(x_prompt, layers, final_norm), _trunk(x_sample, layers, final_norm))
```

```python
import functools
import math

import jax
import jax.numpy as jnp
from jax import lax
from jax.experimental import pallas as pl
from jax.experimental.pallas import tpu as pltpu

F32 = jnp.float32
BF16 = jnp.bfloat16

NORM_EPS = 1e-6
GROUP_W = 512
GRID_W = 64
MLSTM_HEADS = 4
MLSTM_DH = 128
MLSTM_CONV = 5
MLSTM_CHUNK = 128
MLSTM_CHUNKS_PER_STEP = 8
DIFF_HEADS = 4
DIFF_DH = 64
ROPE_THETA = 10000.0
NA_HEADS = 8
NA_DH = 64
NA_WIN_ROWS = 8
NA_WIN_W = 16
NA_ROWS_PER_STEP = 16
S5_GROUP_CH = 16
S5_GROUPS = 32
S5_STATE = 64
S5_CHUNK = 16
LANES = 128
BF16_SUBLANES = 16
MASK_VALUE = -1e30
VMEM_LIMIT = 56 * 1024 * 1024


def _cparams(*sem):
    return pltpu.CompilerParams(dimension_semantics=sem, vmem_limit_bytes=VMEM_LIMIT)


def _rms(x, g):
    return x * lax.rsqrt(jnp.mean(x * x, axis=-1, keepdims=True) + NORM_EPS) * g


def _sigmoid(x):
    return 1.0 / (1.0 + jnp.exp(-x))


def _dot(a, b):
    return jnp.dot(a, b, preferred_element_type=F32)


def _dot_nt(a, b):
    return lax.dot_general(a, b, (((1,), (1,)), ((), ())), preferred_element_type=F32)


def _dot_tn(a, b):
    return lax.dot_general(a, b, (((0,), (0,)), ((), ())), preferred_element_type=F32)


def _split3(x):
    hi = x.astype(BF16)
    r1 = x - hi.astype(F32)
    mid = r1.astype(BF16)
    lo = (r1 - mid.astype(F32)).astype(BF16)
    return hi, mid, lo


def _ffn_kernel(x_ref, g_ref, wg_ref, wu_ref, wd_ref, *rest, final):
    if final:
        fg_ref, o_ref, h_ref = rest
    else:
        o_ref, h_ref = rest
    f = pl.program_id(1)
    nf = pl.num_programs(1)

    @pl.when(f == 0)
    def _():
        h_ref[...] = _rms(x_ref[...], g_ref[...]).astype(BF16)
        o_ref[...] = x_ref[...]

    h = h_ref[...]
    gate = _dot(h, wg_ref[...])
    up = _dot(h, wu_ref[...])
    act = (0.5 * gate * _sigmoid(gate) * up).astype(BF16)
    nchunk = 4
    cw = o_ref.shape[1] // nchunk
    for c in range(nchunk):
        o_ref[:, c * cw:(c + 1) * cw] += _dot(act, wd_ref[:, c * cw:(c + 1) * cw])

    if final:
        @pl.when(f == nf - 1)
        def _():
            o_ref[...] = _rms(o_ref[...], fg_ref[...])


def _ffn_prenormed_kernel(x_hbm, h_ref, wg_ref, wu_ref, wd_ref, *rest, final):
    if final:
        fg_ref, o_ref, sem = rest
    else:
        o_ref, sem = rest
    i = pl.program_id(0)
    f = pl.program_id(1)
    tm = o_ref.shape[0]
    residual = pltpu.make_async_copy(x_hbm.at[pl.ds(i * tm, tm), :], o_ref, sem)

    @pl.when(f == 0)
    def _():
        residual.start()

    h = h_ref[...]
    gate = _dot(h, wg_ref[...])
    up = _dot(h, wu_ref[...])
    act = (0.5 * gate * _sigmoid(gate) * up).astype(BF16)

    @pl.when(f == 0)
    def _():
        residual.wait()

    nchunk = 4
    cw = o_ref.shape[1] // nchunk
    for c in range(nchunk):
        o_ref[:, c * cw:(c + 1) * cw] += _dot(act, wd_ref[:, c * cw:(c + 1) * cw])

    if final:
        @pl.when(f == pl.num_programs(1) - 1)
        def _():
            o_ref[...] = _rms(o_ref[...], fg_ref[...])


def _ffn_prenormed(x, h, wg, wu, wd, final_g=None, tm=1024, tf=512):
    M, D = x.shape
    F = wg.shape[1]
    final = final_g is not None
    in_specs = [
        pl.BlockSpec(memory_space=pl.ANY),
        pl.BlockSpec((tm, D), lambda i, f: (i, 0)),
        pl.BlockSpec((D, tf), lambda i, f: (0, f)),
        pl.BlockSpec((D, tf), lambda i, f: (0, f)),
        pl.BlockSpec((tf, D), lambda i, f: (f, 0)),
    ]
    args = [x, h, wg, wu, wd]
    if final:
        in_specs.append(pl.BlockSpec((1, D), lambda i, f: (0, 0)))
        args.append(final_g.reshape(1, D))
    return pl.pallas_call(
        functools.partial(_ffn_prenormed_kernel, final=final),
        grid=(M // tm, F // tf),
        in_specs=in_specs,
        out_specs=pl.BlockSpec((tm, D), lambda i, f: (i, 0)),
        out_shape=jax.ShapeDtypeStruct((M, D), F32),
        scratch_shapes=[pltpu.SemaphoreType.DMA(())],
        compiler_params=_cparams("parallel", "arbitrary"),
        name="ffn_prenormed",
    )(*args)


def _ffn(x, g, wg, wu, wd, final_g=None, tm=1024, tf=512):
    M, D = x.shape
    F = wg.shape[1]
    final = final_g is not None
    in_specs = [
        pl.BlockSpec((tm, D), lambda i, f: (i, 0), pipeline_mode=pl.Buffered(1)),
        pl.BlockSpec((1, D), lambda i, f: (0, 0)),
        pl.BlockSpec((D, tf), lambda i, f: (0, f)),
        pl.BlockSpec((D, tf), lambda i, f: (0, f)),
        pl.BlockSpec((tf, D), lambda i, f: (f, 0)),
    ]
    args = [x, g.reshape(1, D), wg, wu, wd]
    if final:
        in_specs.append(pl.BlockSpec((1, D), lambda i, f: (0, 0)))
        args.append(final_g.reshape(1, D))
    return pl.pallas_call(
        functools.partial(_ffn_kernel, final=final),
        grid=(M // tm, F // tf),
        in_specs=in_specs,
        out_specs=pl.BlockSpec((tm, D), lambda i, f: (i, 0)),
        out_shape=jax.ShapeDtypeStruct((M, D), F32),
        scratch_shapes=[pltpu.VMEM((tm, D), BF16)],
        compiler_params=_cparams("parallel", "arbitrary"),
        name="ffn",
    )(*args)


def _inproj_kernel(x_ref, g_ref, w_ref, wgate_ref, o_ref, og_ref, os_ref, h_ref):
    n = pl.program_id(1)

    @pl.when(n == 0)
    def _():
        h = _rms(x_ref[...], g_ref[...]).astype(BF16)
        h_ref[...] = h
        og_ref[...] = _dot(h, wgate_ref[...])

    res = _dot(h_ref[...], w_ref[...])
    o_ref[...] = res.astype(BF16)

    @pl.when(n == pl.num_programs(1) - 1)
    def _():
        os_ref[...] = res[:, res.shape[1] - GROUP_W:]


def _inproj(x, g, w_main, w_gate, tm=1024, tn=1408):
    M, D = x.shape
    N = w_main.shape[1]
    assert tn >= GROUP_W and N % tn == 0
    return pl.pallas_call(
        _inproj_kernel,
        grid=(M // tm, N // tn),
        in_specs=[
            pl.BlockSpec((tm, D), lambda i, n: (i, 0)),
            pl.BlockSpec((1, D), lambda i, n: (0, 0)),
            pl.BlockSpec((D, tn), lambda i, n: (0, n)),
            pl.BlockSpec((D, LANES), lambda i, n: (0, 0)),
        ],
        out_specs=[
            pl.BlockSpec((tm, tn), lambda i, n: (i, n)),
            pl.BlockSpec((tm, LANES), lambda i, n: (i, 0)),
            pl.BlockSpec((tm, GROUP_W), lambda i, n: (i, 0)),
        ],
        out_shape=[
            jax.ShapeDtypeStruct((M, N), BF16),
            jax.ShapeDtypeStruct((M, LANES), F32),
            jax.ShapeDtypeStruct((M, GROUP_W), F32),
        ],
        scratch_shapes=[pltpu.VMEM((tm, D), BF16)],
        compiler_params=_cparams("parallel", "arbitrary"),
        name="inproj",
    )(x, g.reshape(1, D), w_main, w_gate)


def _conv_kernel(x_ref, prev_ref, next_ref, w_ref, b_ref, q_ref, k_ref, *, blocks_per_seq):
    tb = x_ref.shape[0]
    halo = prev_ref.shape[0]
    pos = pl.program_id(0) % blocks_per_seq
    prev = jnp.where(pos != 0, prev_ref[...].astype(F32), 0.0)
    nxt = jnp.where(pos != blocks_per_seq - 1, next_ref[...].astype(F32), 0.0)
    ext = jnp.concatenate([prev, x_ref[...].astype(F32), nxt], axis=0)
    rows = tb + 2 * halo
    pad = MLSTM_CONV // 2
    acc = None
    for j in range(MLSTM_CONV):
        shift = (pad - j) % rows
        sh = ext if shift == 0 else pltpu.roll(ext, shift, axis=0)
        term = sh[halo:halo + tb] * w_ref[j:j + 1, :]
        acc = term if acc is None else acc + term
    acc = acc + b_ref[...]
    y = acc * _sigmoid(acc)
    q_ref[...] = y[:, :GROUP_W].astype(BF16)
    k_ref[...] = (y[:, GROUP_W:] * (MLSTM_DH ** -0.5)).astype(BF16)


def _mlstm_conv(proj, conv_w, conv_b, T, tb=512):
    M = proj.shape[0]
    C = 2 * GROUP_W
    halo = BF16_SUBLANES
    r = tb // halo
    nhalo = M // halo
    w = jnp.zeros((8, C), F32).at[:MLSTM_CONV].set(conv_w)
    return pl.pallas_call(
        functools.partial(_conv_kernel, blocks_per_seq=T // tb),
        grid=(M // tb,),
        in_specs=[
            pl.BlockSpec((tb, C), lambda i: (i, 0)),
            pl.BlockSpec((halo, C), lambda i: (jnp.maximum(i * r - 1, 0), 0)),
            pl.BlockSpec((halo, C), lambda i: (jnp.minimum((i + 1) * r, nhalo - 1), 0)),
            pl.BlockSpec((8, C), lambda i: (0, 0)),
            pl.BlockSpec((1, C), lambda i: (0, 0)),
        ],
        out_specs=[
            pl.BlockSpec((tb, GROUP_W), lambda i: (i, 0)),
            pl.BlockSpec((tb, GROUP_W), lambda i: (i, 0)),
        ],
        out_shape=[jax.ShapeDtypeStruct((M, GROUP_W), BF16)] * 2,
        compiler_params=_cparams("parallel"),
        name="mlstm_conv",
    )(proj, proj, proj, w, conv_b.reshape(1, C))


def _mlstm_kernel(qf_ref, kf_ref, vf_ref, gf_ref, qb_ref, kb_ref, vb_ref, gb_ref, bias_ref,
                  of_ref, ob_ref, c_ref, m_ref):
    L = MLSTM_CHUNK
    nsub = qf_ref.shape[0] // L
    H, dh = MLSTM_HEADS, MLSTM_DH

    @pl.when(pl.program_id(1) == 0)
    def _():
        c_ref[...] = jnp.zeros_like(c_ref)
        m_ref[...] = jnp.zeros_like(m_ref)

    row = lax.broadcasted_iota(jnp.int32, (L, L), 0)
    col = lax.broadcasted_iota(jnp.int32, (L, L), 1)
    ones_blk = jnp.ones((L, dh), BF16)

    dirs = ((qf_ref, kf_ref, vf_ref, gf_ref, of_ref), (qb_ref, kb_ref, vb_ref, gb_ref, ob_ref))
    visits = [(d, (j if d == 0 else nsub - 1 - j) * L) for j in range(nsub) for d in range(2)]
    gate_terms = []
    for d, r0 in visits:
        seen = (col <= row) if d == 0 else (col >= row)
        tri = jnp.where(seen, 1.0, 0.0).astype(BF16)
        gt = dirs[d][3][r0:r0 + L, :] + bias_ref[...]
        lf = jnp.minimum(gt, 0.0) - jnp.log(1.0 + jnp.exp(-jnp.abs(gt)))
        hi, mid, lo = _split3(lf)
        bcum = _dot(tri, hi) + _dot(tri, mid) + _dot(tri, lo)
        gate_terms.append((seen, gt, bcum, bcum.T, gt.T))

    for (d, r0), (seen, gt, bcum, bcum_t, gt_t) in zip(visits, gate_terms):
        q_ref, k_ref, v_ref, _, o_ref = dirs[d]
        rows = slice(r0, r0 + L)
        end = L - 1 if d == 0 else 0
        for h in range(H):
            ic = 2 * H * d + h
            fc = 2 * H * d + H + h
            idx = d * H + h
            b_rep = jnp.broadcast_to(bcum[:, fc:fc + 1], (L, dh))
            i_rep = jnp.broadcast_to(gt[:, ic:ic + 1], (L, dh))
            b_row = bcum_t[fc:fc + 1, :]
            i_row = gt_t[ic:ic + 1, :]
            g_tot = b_rep[end:end + 1, :]
            m_prev = m_ref[idx]
            c_prev = c_ref[idx]
            q = q_ref[rows, h * dh:(h + 1) * dh]
            k = k_ref[rows, h * dh:(h + 1) * dh]
            v_aug = jnp.concatenate([v_ref[rows, h * dh:(h + 1) * dh], ones_blk], axis=1)

            dlog = jnp.where(seen, b_rep - b_row + i_row, -jnp.inf)
            inter = b_rep + m_prev
            m_t = jnp.maximum(inter, jnp.max(dlog, axis=1, keepdims=True))
            s = (_dot_nt(q, k) * jnp.exp(dlog - m_t)).astype(BF16)
            s_inter = jnp.exp(inter - m_t)
            num_aug = jnp.concatenate([s_inter, s_inter], axis=1) * _dot(q, c_prev.astype(BF16)) + _dot(s, v_aug)
            den = jnp.maximum(jnp.abs(num_aug[:, dh:]), jnp.exp(-m_t))
            o_ref[rows, h * dh:(h + 1) * dh] = num_aug[:, :dh] / den

            a_rep = g_tot - b_rep + i_rep
            m_new = jnp.maximum(g_tot + m_prev, jnp.max(a_rep, axis=0, keepdims=True))
            decay = jnp.exp(g_tot + m_prev - m_new)
            kw = (k.astype(F32) * jnp.exp(a_rep - m_new)).astype(BF16)
            c_ref[idx] = jnp.concatenate([decay, decay], axis=1) * c_prev + _dot_tn(kw, v_aug)
            m_ref[idx] = m_new


def _mlstm(qc, kc, proj, gates, gate_bias, B, T):
    M = qc.shape[0]
    L = MLSTM_CHUNK
    assert L == LANES == MLSTM_DH
    tb = MLSTM_CHUNKS_PER_STEP * L
    assert T % tb == 0
    nc = T // tb
    W = GROUP_W
    vblk = 2
    fwd = lambda b, i: (b * nc + i, 0)
    bwd = lambda b, i: (b * nc + nc - 1 - i, 0)
    fwd_v = lambda b, i: (b * nc + i, vblk)
    bwd_v = lambda b, i: (b * nc + nc - 1 - i, vblk)
    bias = jnp.zeros((1, LANES), F32).at[0, :4 * MLSTM_HEADS].set(gate_bias)
    return pl.pallas_call(
        _mlstm_kernel,
        grid=(B, nc),
        in_specs=[
            pl.BlockSpec((tb, W), fwd), pl.BlockSpec((tb, W), fwd), pl.BlockSpec((tb, W), fwd_v),
            pl.BlockSpec((tb, LANES), fwd),
            pl.BlockSpec((tb, W), bwd), pl.BlockSpec((tb, W), bwd), pl.BlockSpec((tb, W), bwd_v),
            pl.BlockSpec((tb, LANES), bwd),
            pl.BlockSpec((1, LANES), lambda b, i: (0, 0)),
        ],
        out_specs=[pl.BlockSpec((tb, W), fwd), pl.BlockSpec((tb, W), bwd)],
        out_shape=[jax.ShapeDtypeStruct((M, W), F32)] * 2,
        scratch_shapes=[
            pltpu.VMEM((2 * MLSTM_HEADS, MLSTM_DH, 2 * MLSTM_DH), F32),
            pltpu.VMEM((2 * MLSTM_HEADS, 1, LANES), F32),
        ],
        compiler_params=_cparams("parallel", "arbitrary"),
        name="mlstm",
    )(qc, kc, proj, gates, qc, kc, proj, gates, bias)


def _rope_kernel(x_ref, cos_ref, sin_ref, q_ref, k_ref):
    x = x_ref[...].astype(F32)
    reps = x.shape[1] // LANES
    cos = jnp.concatenate([cos_ref[...]] * reps, axis=1)
    sin = jnp.concatenate([sin_ref[...]] * reps, axis=1)
    lane = lax.broadcasted_iota(jnp.int32, x.shape, 1)
    width = x.shape[1]
    half = DIFF_DH // 2
    rot = jnp.where(lane % DIFF_DH < half, pltpu.roll(x, width - half, axis=1), pltpu.roll(x, half, axis=1))
    y = x * cos + rot * sin
    q_ref[...] = (y[:, :GROUP_W] * (DIFF_DH ** -0.5 * math.log2(math.e))).astype(BF16)
    k_ref[...] = y[:, GROUP_W:].astype(BF16)


def _rope_tables(T):
    half = DIFF_DH // 2
    inv = 1.0 / (ROPE_THETA ** (jnp.arange(0, DIFF_DH, 2, dtype=F32) / DIFF_DH))
    ang = jnp.arange(T, dtype=F32)[:, None] * inv[None, :]
    cos, sin = jnp.cos(ang), jnp.sin(ang)
    cos_t = jnp.concatenate([cos, cos] * (LANES // DIFF_DH), axis=1)
    sin_t = jnp.concatenate([-sin, sin] * (LANES // DIFF_DH), axis=1)
    return cos_t, sin_t


def _rope(proj, cos_t, sin_t, T, tb=512):
    M = proj.shape[0]
    nb = T // tb
    return pl.pallas_call(
        _rope_kernel,
        grid=(M // tb,),
        in_specs=[
            pl.BlockSpec((tb, 2 * GROUP_W), lambda i: (i, 2)),
            pl.BlockSpec((tb, LANES), lambda i: (i % nb, 0)),
            pl.BlockSpec((tb, LANES), lambda i: (i % nb, 0)),
        ],
        out_specs=[pl.BlockSpec((tb, GROUP_W), lambda i: (i, 0))] * 2,
        out_shape=[jax.ShapeDtypeStruct((M, GROUP_W), BF16)] * 2,
        compiler_params=_cparams("parallel"),
        name="rope",
    )(proj, cos_t, sin_t)


def _diff_kernel(q_ref, k_ref, v_ref, lp_ref, g_ref, o_ref, q2_ref, m_ref, acc_ref, *, lam_init, tks):
    ki = pl.program_id(3)
    tq = q_ref.shape[0]
    dv = v_ref.shape[1]
    ones = jnp.ones((tks, dv), BF16)

    @pl.when(ki == 0)
    def _():
        q = q_ref[...]
        lane = lax.broadcasted_iota(jnp.int32, q.shape, 1)
        zero = jnp.zeros_like(q)
        q2_ref[0:tq, :] = jnp.where(lane < DIFF_DH, q, zero)
        q2_ref[tq:2 * tq, :] = jnp.where(lane < DIFF_DH, zero, q)
        m_ref[...] = jnp.full_like(m_ref, -jnp.inf)
        acc_ref[...] = jnp.zeros_like(acc_ref)

    q2 = q2_ref[...]
    for j in range(k_ref.shape[0] // tks):
        s = _dot_nt(q2, k_ref[j * tks:(j + 1) * tks, :])
        tiles = [s[:, c * LANES:(c + 1) * LANES] for c in range(tks // LANES)]
        m_prev = m_ref[...]
        m_new = jnp.maximum(m_prev, jnp.max(functools.reduce(jnp.maximum, tiles), axis=1, keepdims=True))
        alpha = jnp.exp2(m_prev - m_new)
        p = jnp.concatenate([jnp.exp2(t - m_new).astype(BF16) for t in tiles], axis=1)
        v_aug = jnp.concatenate([v_ref[j * tks:(j + 1) * tks, :], ones], axis=1)
        acc_ref[...] = jnp.concatenate([alpha, alpha], axis=1) * acc_ref[...] + _dot(p, v_aug)
        m_ref[...] = m_new

    @pl.when(ki == pl.num_programs(3) - 1)
    def _():
        lp = lp_ref[...]
        lam = (jnp.exp(jnp.sum(lp[0:1] * lp[1:2], axis=1, keepdims=True))
               - jnp.exp(jnp.sum(lp[2:3] * lp[3:4], axis=1, keepdims=True)) + lam_init)
        o = acc_ref[:, 0:dv] / acc_ref[:, dv:2 * dv]
        out = o[0:tq] - lam * o[tq:2 * tq]
        o_ref[...] = (_rms(out, g_ref[...]) * (1.0 - lam_init)).astype(BF16)


def _diff_attn(qr, kr, proj, lam_params, norm_g, B, T, layer_idx, tq=1024, tk=8192, tks=512):
    M = qr.shape[0]
    dv = 2 * DIFF_DH
    tk = min(tk, T)
    nq, nk = T // tq, T // tk
    vblk0 = (6 * GROUP_W) // dv
    lam_init = 0.8 - 0.6 * math.exp(-0.3 * layer_idx)
    return pl.pallas_call(
        functools.partial(_diff_kernel, lam_init=lam_init, tks=tks),
        grid=(B, DIFF_HEADS, nq, nk),
        in_specs=[
            pl.BlockSpec((tq, dv), lambda b, h, qi, ki: (b * nq + qi, h)),
            pl.BlockSpec((tk, dv), lambda b, h, qi, ki: (b * nk + ki, h)),
            pl.BlockSpec((tk, dv), lambda b, h, qi, ki: (b * nk + ki, vblk0 + h)),
            pl.BlockSpec((4, DIFF_DH), lambda b, h, qi, ki: (0, 0)),
            pl.BlockSpec((1, dv), lambda b, h, qi, ki: (0, 0)),
        ],
        out_specs=pl.BlockSpec((tq, dv), lambda b, h, qi, ki: (b * nq + qi, h)),
        out_shape=jax.ShapeDtypeStruct((M, GROUP_W), BF16),
        scratch_shapes=[
            pltpu.VMEM((2 * tq, dv), BF16),
            pltpu.VMEM((2 * tq, LANES), F32),
            pltpu.VMEM((2 * tq, 2 * dv), F32),
        ],
        compiler_params=_cparams("parallel", "parallel", "parallel", "arbitrary"),
        name="diff_attn",
    )(qr, kr, proj, lam_params, norm_g.reshape(1, dv))


def _na_kernel(q_ref, k_ref, v_ref, bias_ref, o_ref, *, rows):
    g = pl.program_id(2)
    win = NA_WIN_ROWS * GRID_W
    lane = lax.broadcasted_iota(jnp.int32, (GRID_W, 2 * NA_DH), 1)
    first = lane < NA_DH
    w = 2 * NA_DH
    ones = jnp.ones((win, w), BF16)
    scores, starts = [], []
    for jr in range(NA_ROWS_PER_STEP):
        r = g * NA_ROWS_PER_STEP + jr
        rs = jnp.clip(r - NA_WIN_ROWS // 2, 0, rows - NA_WIN_ROWS)
        start = pl.multiple_of(rs * GRID_W, GRID_W)
        q = q_ref[jr * GRID_W:(jr + 1) * GRID_W, :]
        zero = jnp.zeros_like(q)
        q2 = jnp.concatenate([jnp.where(first, q, zero), jnp.where(first, zero, q)], axis=0)
        s = _dot_nt(q2, k_ref[pl.ds(start, win), :]) * (NA_DH ** -0.5 * math.log2(math.e)) + bias_ref[r - rs]
        scores.append(s)
        starts.append(start)
    for jr in range(NA_ROWS_PER_STEP):
        s = scores[jr]
        e = jnp.exp2(s - jnp.max(s, axis=1, keepdims=True)).astype(BF16)
        pv = _dot(e, jnp.concatenate([v_ref[pl.ds(starts[jr], win), :], ones], axis=1))
        pv = pv[:, :w] / pv[:, w:]
        o_ref[jr * GRID_W:(jr + 1) * GRID_W, :] = jnp.where(first, pv[:GRID_W], pv[GRID_W:]).astype(BF16)


def _na_bias_table(rpb):
    c = jnp.arange(GRID_W)
    col_start = jnp.clip(c - NA_WIN_W // 2, 0, GRID_W - NA_WIN_W)
    kc = jnp.arange(GRID_W)
    valid = (kc[None, :] >= col_start[:, None]) & (kc[None, :] < col_start[:, None] + NA_WIN_W)
    col_off = kc[None, :] - c[:, None] + NA_WIN_W - 1
    p = jnp.arange(NA_WIN_ROWS)
    k = jnp.arange(NA_WIN_ROWS)
    row_off = k[None, :] - p[:, None] + NA_WIN_ROWS - 1
    pick_row = (row_off[..., None] == jnp.arange(2 * NA_WIN_ROWS - 1)).astype(F32)
    pick_col = (col_off[..., None] == jnp.arange(2 * NA_WIN_W - 1)).astype(F32)
    t = jnp.einsum('hrq,pkr,cjq->hpckj', rpb.astype(F32), pick_row, pick_col,
                   precision=lax.Precision.HIGHEST) * math.log2(math.e)
    t = jnp.where(valid[None, None, :, None, :], t, MASK_VALUE)
    t = t.reshape(NA_HEADS, NA_WIN_ROWS, GRID_W, NA_WIN_ROWS * GRID_W)
    t = t.reshape(NA_HEADS // 2, 2, NA_WIN_ROWS, GRID_W, NA_WIN_ROWS * GRID_W).transpose(0, 2, 1, 3, 4)
    return t.reshape(NA_HEADS // 2, NA_WIN_ROWS, 2 * GRID_W, NA_WIN_ROWS * GRID_W)


def _na_attn(proj, bias_table, B, T):
    M = proj.shape[0]
    rows = T // GRID_W
    assert rows % NA_ROWS_PER_STEP == 0 and rows >= NA_WIN_ROWS
    tq = NA_ROWS_PER_STEP * GRID_W
    ng = rows // NA_ROWS_PER_STEP
    hp = NA_HEADS // 2
    w = 2 * NA_DH
    qblk0, kblk0, vblk0 = (7 * GROUP_W) // w, (8 * GROUP_W) // w, (9 * GROUP_W) // w
    return pl.pallas_call(
        functools.partial(_na_kernel, rows=rows),
        grid=(B, hp, ng),
        in_specs=[
            pl.BlockSpec((tq, w), lambda b, h, g: (b * ng + g, qblk0 + h)),
            pl.BlockSpec((T, w), lambda b, h, g: (b, kblk0 + h)),
            pl.BlockSpec((T, w), lambda b, h, g: (b, vblk0 + h)),
            pl.BlockSpec((None, NA_WIN_ROWS, 2 * GRID_W, NA_WIN_ROWS * GRID_W), lambda b, h, g: (h, 0, 0, 0)),
        ],
        out_specs=pl.BlockSpec((tq, w), lambda b, h, g: (b * ng + g, h)),
        out_shape=jax.ShapeDtypeStruct((M, GROUP_W), BF16),
        compiler_params=_cparams("parallel", "parallel", "arbitrary"),
        name="na_attn",
    )(proj, proj, proj, bias_table)


def _s5_params(lam_re, lam_im, log_step, b_re, b_im, c_re, c_im, d_skip):
    L, c, P, G = S5_CHUNK, S5_GROUP_CH, S5_STATE, S5_GROUPS
    K8 = LANES // c
    O = G // K8
    lam = lax.complex(lam_re.astype(F32), lam_im.astype(F32))
    step = jnp.exp(log_step.astype(F32))[..., None]
    log_lam_bar = lam * step
    lam_bar = jnp.exp(log_lam_bar)
    b_mat = lax.complex(b_re.astype(F32), b_im.astype(F32))
    c_mat = lax.complex(c_re.astype(F32), c_im.astype(F32))
    b_bar = ((lam_bar - 1.0) / lam)[..., None] * b_mat[None]
    pw = jnp.exp(log_lam_bar[:, None] * jnp.arange(L + 1, dtype=F32)[None, :, None, None])
    cb = c_mat[None, :, :, None, :] * b_bar.transpose(0, 1, 3, 2)[:, :, None, :, :]
    s_i = jnp.arange(L)[:, None]
    t_i = jnp.arange(L)[None, :]
    lag = jnp.stack([t_i - s_i, s_i - t_i])
    lag_pw = jnp.where((lag >= 0)[..., None, None],
                       jnp.exp(log_lam_bar[:, None, None] * jnp.maximum(lag, 0).astype(F32)[..., None, None]),
                       0.0)
    m = jnp.einsum('dgoip,dstgp->gsito', jnp.concatenate([cb.real, -cb.imag], axis=-1),
                   jnp.concatenate([lag_pw.real, lag_pw.imag], axis=-1), precision=lax.Precision.HIGH)
    eye = jnp.eye(K8, dtype=F32)
    lane_of = jnp.arange(K8)[:, None] * c + jnp.arange(c)[None, :]
    sel = (jnp.arange(LANES)[None, None, :] == lane_of[:, :, None]).astype(BF16)
    m8 = jnp.einsum('agsitc,gcl->asgitl', m.astype(BF16).reshape(O, K8, L, c, L, c), sel,
                    preferred_element_type=BF16).reshape(O, L * LANES, L * LANES)
    ef = pw[0][L - 1 - jnp.arange(L)][:, :, :, None] * b_bar[0][None]
    eb = pw[1][jnp.arange(L)][:, :, :, None] * b_bar[1][None]
    to_rows = lambda e: e.transpose(1, 0, 3, 2)
    ef, eb = to_rows(ef), to_rows(eb)
    bs = jnp.concatenate([ef.real, ef.imag, eb.real, eb.imag], axis=-1).astype(BF16)
    bs8 = jnp.einsum('agsik,gh->asgihk', bs.reshape(O, K8, L, c, 4 * P), eye.astype(BF16),
                     preferred_element_type=BF16).reshape(O, L * LANES, K8 * 4 * P)
    of = c_mat[:, None] * pw[0][1:L + 1].transpose(1, 0, 2)[:, :, None, :]
    ob = c_mat[:, None] * pw[1][L - jnp.arange(L)].transpose(1, 0, 2)[:, :, None, :]
    to_cols = lambda o: o.transpose(0, 3, 1, 2)
    of, ob = to_cols(of), to_cols(ob)
    cs = jnp.concatenate([of.real, -of.imag, ob.real, -ob.imag], axis=1)
    cs8 = jnp.einsum('agktc,gcl->agktl', cs.astype(BF16).reshape(O, K8, 4 * P, L, c), sel,
                     preferred_element_type=BF16).reshape(O, K8 * 4 * P, L * LANES)
    al = pw[:, L]
    a1 = jnp.concatenate([al.real, al.real], axis=-1)
    a2 = jnp.concatenate([-al.imag, al.imag], axis=-1)
    a8 = jnp.stack([a1, a2, -a2], axis=1)
    a8 = a8.reshape(2, 3, O, K8, LANES).transpose(2, 0, 1, 3, 4).reshape(O, 6 * K8, LANES)
    d8 = d_skip.astype(F32).reshape(O, 1, LANES)
    return m8.astype(BF16), bs8.astype(BF16), cs8.astype(BF16), a8, d8


def _s5_kernel(u_ref, m_ref, bs_ref, cs_ref, a_ref, d_ref, y_ref, x_ref, sf_ref, sb_ref, sfs_ref, sbs_ref):
    L = S5_CHUNK
    K8 = LANES // S5_GROUP_CH
    sw = 4 * S5_STATE
    nc = u_ref.shape[0] // L
    for s in range(L):
        x_ref[:, s * LANES:(s + 1) * LANES] = u_ref[pl.ds(s, nc, stride=L), :].astype(BF16)
    x = x_ref[...]
    for g in range(K8):
        sg = _dot(x, bs_ref[:, g * sw:(g + 1) * sw])
        sf_ref[g * nc:(g + 1) * nc, :] = sg[:, 0:LANES]
        sb_ref[g * nc:(g + 1) * nc, :] = sg[:, LANES:2 * LANES]
        sfs_ref[g * nc:(g + 1) * nc, :] = pltpu.roll(sg[:, 0:LANES], LANES // 2, axis=1)
        sbs_ref[g * nc:(g + 1) * nc, :] = pltpu.roll(sg[:, LANES:2 * LANES], LANES // 2, axis=1)
    a1f, a2f, a2fs = a_ref[0:K8], a_ref[K8:2 * K8], a_ref[2 * K8:3 * K8]
    a1b, a2b, a2bs = a_ref[3 * K8:4 * K8], a_ref[4 * K8:5 * K8], a_ref[5 * K8:6 * K8]

    def step(c, carry):
        hf, hfs, hb, hbs = carry
        cb = nc - 1 - c
        fwd = pl.ds(c, K8, stride=nc)
        bwd = pl.ds(cb, K8, stride=nc)
        sf, sfs = sf_ref[fwd, :], sfs_ref[fwd, :]
        sb, sbs = sb_ref[bwd, :], sbs_ref[bwd, :]
        sf_ref[fwd, :] = hf
        sb_ref[bwd, :] = hb
        return (a1f * hf + a2f * hfs + sf, a1f * hfs + a2fs * hf + sfs,
                a1b * hb + a2b * hbs + sb, a1b * hbs + a2bs * hb + sbs)

    z = jnp.zeros((K8, LANES), F32)
    lax.fori_loop(0, nc, step, (z, z, z, z), unroll=8)
    y = _dot(x, m_ref[...])
    for g in range(K8):
        hg = jnp.concatenate([sf_ref[g * nc:(g + 1) * nc, :], sb_ref[g * nc:(g + 1) * nc, :]], axis=1)
        y = y + _dot(hg.astype(BF16), cs_ref[g * sw:(g + 1) * sw, :])
    for t in range(L):
        y_ref[pl.ds(t, nc, stride=L), :] = y[:, t * LANES:(t + 1) * LANES]
    y_ref[...] += u_ref[...] * d_ref[...]


def _s5(su, params, B, T):
    m8, bs8, cs8, a8, d8 = params
    L = S5_CHUNK
    nc = T // L
    O = m8.shape[0]
    kw = L * LANES
    sw = 4 * S5_STATE
    K8 = LANES // S5_GROUP_CH
    one = pl.Buffered(1)
    return pl.pallas_call(
        _s5_kernel,
        grid=(O, B),
        in_specs=[
            pl.BlockSpec((T, LANES), lambda o, b: (b, o), pipeline_mode=one),
            pl.BlockSpec((None, kw, kw), lambda o, b: (o, 0, 0), pipeline_mode=one),
            pl.BlockSpec((None, kw, K8 * sw), lambda o, b: (o, 0, 0), pipeline_mode=one),
            pl.BlockSpec((None, K8 * sw, kw), lambda o, b: (o, 0, 0), pipeline_mode=one),
            pl.BlockSpec((None, 6 * K8, LANES), lambda o, b: (o, 0, 0)),
            pl.BlockSpec((None, 1, LANES), lambda o, b: (o, 0, 0)),
        ],
        out_specs=pl.BlockSpec((T, LANES), lambda o, b: (b, o)),
        out_shape=jax.ShapeDtypeStruct((B * T, GROUP_W), F32),
        scratch_shapes=[pltpu.VMEM((nc, kw), BF16)] + [pltpu.VMEM((K8 * nc, LANES), F32)] * 4,
        compiler_params=_cparams("arbitrary", "arbitrary"),
        name="s5",
    )(su, m8, bs8, cs8, a8, d8)


def _outproj_kernel(x_ref, hf_ref, hb_ref, og_ref, mg_ref, yb_ref, yc_ref, ys_ref, gw_ref, gb_ref, w_ref, ng_ref,
                    o_ref, on_ref):
    W = GROUP_W
    h = hf_ref[...] + hb_ref[...]
    parts = []
    for j in range(MLSTM_HEADS):
        hj = h[:, j * MLSTM_DH:(j + 1) * MLSTM_DH]
        parts.append(hj * lax.rsqrt(jnp.mean(hj * hj, axis=-1, keepdims=True) + NORM_EPS))
    hn = jnp.concatenate(parts, axis=1) * mg_ref[...]
    ya = (hn * _sigmoid(og_ref[...].astype(F32))).astype(BF16)
    y = ys_ref[...]
    z = 0.5 * y * (1.0 + jnp.tanh(math.sqrt(2.0 / math.pi) * (y + 0.044715 * (y * y * y))))
    yd = (z * _sigmoid(_dot(z.astype(BF16), gw_ref[...]) + gb_ref[...])).astype(BF16)
    acc = x_ref[...] + _dot(ya, w_ref[0:W, :])
    acc = acc + _dot(yb_ref[...], w_ref[W:2 * W, :])
    acc = acc + _dot(yc_ref[...], w_ref[2 * W:3 * W, :])
    acc = acc + _dot(yd, w_ref[3 * W:4 * W, :])
    o_ref[...] = acc
    on_ref[...] = _rms(acc, ng_ref[...]).astype(BF16)


def _outproj(x, hf, hb, proj, mlstm_norm, yb, yc, ys, glu_w, glu_b, w_out, next_norm, tm=512):
    M, D = x.shape
    W = GROUP_W
    tok = lambda i: (i, 0)
    const = lambda i: (0, 0)
    return pl.pallas_call(
        _outproj_kernel,
        grid=(M // tm,),
        in_specs=[
            pl.BlockSpec((tm, D), tok),
            pl.BlockSpec((tm, W), tok), pl.BlockSpec((tm, W), tok),
            pl.BlockSpec((tm, W), lambda i: (i, 3)),
            pl.BlockSpec((1, W), const),
            pl.BlockSpec((tm, W), tok), pl.BlockSpec((tm, W), tok), pl.BlockSpec((tm, W), tok),
            pl.BlockSpec((W, W), const), pl.BlockSpec((1, W), const),
            pl.BlockSpec((4 * W, D), const),
            pl.BlockSpec((1, D), const),
        ],
        out_specs=[pl.BlockSpec((tm, D), tok), pl.BlockSpec((tm, D), tok)],
        out_shape=[jax.ShapeDtypeStruct((M, D), F32), jax.ShapeDtypeStruct((M, D), BF16)],
        compiler_params=_cparams("parallel"),
        name="outproj",
    )(x, hf, hb, proj, mlstm_norm.reshape(1, W), yb, yc, ys, glu_w, glu_b.reshape(1, W), w_out,
      next_norm.reshape(1, D))


def _prepare_layer(w, l):
    W = GROUP_W
    ng = 4 * MLSTM_HEADS
    w_in = w['w_in'][l]
    w_main = jnp.concatenate([w_in[:, :4 * W], w_in[:, 4 * W + ng:]], axis=1).astype(BF16)
    w_gate = jnp.zeros((w_in.shape[0], LANES), BF16).at[:, :ng].set(w_in[:, 4 * W:4 * W + ng].astype(BF16))
    p = {k: w[k][l] for k in ('ffn1_norm', 'mix_norm', 'ffn2_norm', 'mlstm_conv_w', 'mlstm_conv_b', 'mlstm_gate_bias',
                              'mlstm_norm', 'diff_lambda', 'diff_norm', 's5_glu_b')}
    for k in ('ffn1_w_gate', 'ffn1_w_up', 'ffn1_w_down', 'ffn2_w_gate', 'ffn2_w_up', 'ffn2_w_down', 's5_glu_w', 'w_out'):
        p[k] = w[k][l].astype(BF16)
    p['w_main'], p['w_gate'] = w_main, w_gate
    p['na_bias'] = _na_bias_table(w['na_rpb'][l])
    p['s5'] = _s5_params(w['s5_lambda_re'][l], w['s5_lambda_im'][l], w['s5_log_step'][l], w['s5_b_re'][l],
                         w['s5_b_im'][l], w['s5_c_re'][l], w['s5_c_im'][l], w['s5_d'][l])
    return p


def _mixers(x, p, B, T, layer_idx, rope_tabs):
    proj, gates, su = _inproj(x, p['mix_norm'], p['w_main'], p['w_gate'])
    qc, kc = _mlstm_conv(proj, p['mlstm_conv_w'], p['mlstm_conv_b'], T)
    hf, hb = _mlstm(qc, kc, proj, gates, p['mlstm_gate_bias'], B, T)
    qr, kr = _rope(proj, rope_tabs[0], rope_tabs[1], T)
    yb = _diff_attn(qr, kr, proj, p['diff_lambda'], p['diff_norm'], B, T, layer_idx)
    yc = _na_attn(proj, p['na_bias'], B, T)
    ys = _s5(su, p['s5'], B, T)
    return _outproj(x, hf, hb, proj, p['mlstm_norm'], yb, yc, ys, p['s5_glu_w'], p['s5_glu_b'], p['w_out'],
                    p['ffn2_norm'])


def _trunk(x, layers, final_norm):
    B, T, D = x.shape
    x = x.reshape(B * T, D)
    rope_tabs = _rope_tables(T)
    for l, p in enumerate(layers):
        x = _ffn(x, p['ffn1_norm'], p['ffn1_w_gate'], p['ffn1_w_up'], p['ffn1_w_down'])
        x, h = _mixers(x, p, B, T, l, rope_tabs)
        last = l == len(layers) - 1
        x = _ffn_prenormed(x, h, p['ffn2_w_gate'], p['ffn2_w_up'], p['ffn2_w_down'],
                           final_g=final_norm if last else None)
    return x.reshape(B, T, D)


def kernel(x_prompt, x_sample, ffn1_norm, ffn1_w_gate, ffn1_w_up, ffn1_w_down, mix_norm, w_in, mlstm_conv_w, mlstm_conv_b, mlstm_gate_bias, mlstm_norm, diff_lambda, diff_norm, na_rpb, s5_lambda_re, s5_lambda_im, s5_log_step, s5_b_re, s5_b_im, s5_c_re, s5_c_im, s5_d, s5_glu_w, s5_glu_b, w_out, ffn2_norm, ffn2_w_gate, ffn2_w_up, ffn2_w_down, final_norm):
    w = dict(ffn1_norm=ffn1_norm, ffn1_w_gate=ffn1_w_gate, ffn1_w_up=ffn1_w_up, ffn1_w_down=ffn1_w_down,
             mix_norm=mix_norm, w_in=w_in, mlstm_conv_w=mlstm_conv_w, mlstm_conv_b=mlstm_conv_b,
             mlstm_gate_bias=mlstm_gate_bias, mlstm_norm=mlstm_norm, diff_lambda=diff_lambda, diff_norm=diff_norm,
             na_rpb=na_rpb, s5_lambda_re=s5_lambda_re, s5_lambda_im=s5_lambda_im, s5_log_step=s5_log_step,
             s5_b_re=s5_b_re, s5_b_im=s5_b_im, s5_c_re=s5_c_re, s5_c_im=s5_c_im, s5_d=s5_d,
             s5_glu_w=s5_glu_w, s5_glu_b=s5_glu_b, w_out=w_out, ffn2_norm=ffn2_norm, ffn2_w_gate=ffn2_w_gate,
             ffn2_w_up=ffn2_w_up, ffn2_w_down=ffn2_w_down)
    layers = [_prepare_layer(w, l) for l in range(ffn1_norm.shape[0])]
    return (_trunk(x_prompt, layers, final_norm), _trunk(x_sample, layers, final_norm))
```

```python
import functools
import math

import jax
import jax.numpy as jnp
from jax import lax
from jax.experimental import pallas as pl
from jax.experimental.pallas import tpu as pltpu

F32 = jnp.float32
BF16 = jnp.bfloat16

NORM_EPS = 1e-6
GROUP_W = 512
GRID_W = 64
MLSTM_HEADS = 4
MLSTM_DH = 128
MLSTM_CONV = 5
MLSTM_CHUNK = 128
MLSTM_CHUNKS_PER_STEP = 8
DIFF_HEADS = 4
DIFF_DH = 64
ROPE_THETA = 10000.0
NA_HEADS = 8
NA_DH = 64
NA_WIN_ROWS = 8
NA_WIN_W = 16
NA_ROWS_PER_STEP = 16
S5_GROUP_CH = 16
S5_GROUPS = 32
S5_STATE = 64
S5_CHUNK = 16
LANES = 128
BF16_SUBLANES = 16
MASK_VALUE = -1e30
VMEM_LIMIT = 56 * 1024 * 1024


def _cparams(*sem):
    return pltpu.CompilerParams(dimension_semantics=sem, vmem_limit_bytes=VMEM_LIMIT)


def _rms(x, g):
    return x * lax.rsqrt(jnp.mean(x * x, axis=-1, keepdims=True) + NORM_EPS) * g


def _sigmoid(x):
    return 1.0 / (1.0 + jnp.exp(-x))


def _dot(a, b):
    return jnp.dot(a, b, preferred_element_type=F32)


def _dot_nt(a, b):
    return lax.dot_general(a, b, (((1,), (1,)), ((), ())), preferred_element_type=F32)


def _dot_tn(a, b):
    return lax.dot_general(a, b, (((0,), (0,)), ((), ())), preferred_element_type=F32)


def _split3(x):
    hi = x.astype(BF16)
    r1 = x - hi.astype(F32)
    mid = r1.astype(BF16)
    lo = (r1 - mid.astype(F32)).astype(BF16)
    return hi, mid, lo


def _ffn_kernel(x_ref, g_ref, wg_ref, wu_ref, wd_ref, *rest, final):
    if final:
        fg_ref, o_ref, h_ref = rest
    else:
        o_ref, h_ref = rest
    f = pl.program_id(1)
    nf = pl.num_programs(1)

    @pl.when(f == 0)
    def _():
        h_ref[...] = _rms(x_ref[...], g_ref[...]).astype(BF16)
        o_ref[...] = x_ref[...]

    h = h_ref[...]
    gate = _dot(h, wg_ref[...])
    up = _dot(h, wu_ref[...])
    act = (0.5 * gate * _sigmoid(gate) * up).astype(BF16)
    nchunk = 4
    cw = o_ref.shape[1] // nchunk
    for c in range(nchunk):
        o_ref[:, c * cw:(c + 1) * cw] += _dot(act, wd_ref[:, c * cw:(c + 1) * cw])

    if final:
        @pl.when(f == nf - 1)
        def _():
            o_ref[...] = _rms(o_ref[...], fg_ref[...])


def _ffn_prenormed_kernel(x_hbm, h_ref, wg_ref, wu_ref, wd_ref, *rest, final, emit_next):
    if final:
        fg_ref, o_ref, sem = rest
    elif emit_next:
        ng_ref, o_ref, on_ref, sem = rest
    else:
        o_ref, sem = rest
    i = pl.program_id(0)
    f = pl.program_id(1)
    tm = o_ref.shape[0]
    residual = pltpu.make_async_copy(x_hbm.at[pl.ds(i * tm, tm), :], o_ref, sem)

    @pl.when(f == 0)
    def _():
        residual.start()

    h = h_ref[...]
    gate = _dot(h, wg_ref[...])
    up = _dot(h, wu_ref[...])
    act = (0.5 * gate * _sigmoid(gate) * up).astype(BF16)

    @pl.when(f == 0)
    def _():
        residual.wait()

    nchunk = 4
    cw = o_ref.shape[1] // nchunk
    for c in range(nchunk):
        o_ref[:, c * cw:(c + 1) * cw] += _dot(act, wd_ref[:, c * cw:(c + 1) * cw])

    if final or emit_next:
        @pl.when(f == pl.num_programs(1) - 1)
        def _():
            if final:
                o_ref[...] = _rms(o_ref[...], fg_ref[...])
            else:
                on_ref[...] = _rms(o_ref[...], ng_ref[...]).astype(BF16)


def _ffn_prenormed(x, h, wg, wu, wd, final_g=None, next_g=None, tm=1024, tf=512):
    M, D = x.shape
    F = wg.shape[1]
    final = final_g is not None
    emit_next = next_g is not None
    assert not (final and emit_next)
    tok = lambda i, f: (i, 0)
    in_specs = [
        pl.BlockSpec(memory_space=pl.ANY),
        pl.BlockSpec((tm, D), tok),
        pl.BlockSpec((D, tf), lambda i, f: (0, f)),
        pl.BlockSpec((D, tf), lambda i, f: (0, f)),
        pl.BlockSpec((tf, D), lambda i, f: (f, 0)),
    ]
    args = [x, h, wg, wu, wd]
    out_specs = pl.BlockSpec((tm, D), tok)
    out_shape = jax.ShapeDtypeStruct((M, D), F32)
    if final or emit_next:
        in_specs.append(pl.BlockSpec((1, D), lambda i, f: (0, 0)))
        args.append((final_g if final else next_g).reshape(1, D))
    if emit_next:
        out_specs = [out_specs, pl.BlockSpec((tm, D), tok)]
        out_shape = [out_shape, jax.ShapeDtypeStruct((M, D), BF16)]
    return pl.pallas_call(
        functools.partial(_ffn_prenormed_kernel, final=final, emit_next=emit_next),
        grid=(M // tm, F // tf),
        in_specs=in_specs,
        out_specs=out_specs,
        out_shape=out_shape,
        scratch_shapes=[pltpu.SemaphoreType.DMA(())],
        compiler_params=_cparams("parallel", "arbitrary"),
        name="ffn_prenormed",
    )(*args)


def _ffn(x, g, wg, wu, wd, final_g=None, tm=1024, tf=512):
    M, D = x.shape
    F = wg.shape[1]
    final = final_g is not None
    in_specs = [
        pl.BlockSpec((tm, D), lambda i, f: (i, 0), pipeline_mode=pl.Buffered(1)),
        pl.BlockSpec((1, D), lambda i, f: (0, 0)),
        pl.BlockSpec((D, tf), lambda i, f: (0, f)),
        pl.BlockSpec((D, tf), lambda i, f: (0, f)),
        pl.BlockSpec((tf, D), lambda i, f: (f, 0)),
    ]
    args = [x, g.reshape(1, D), wg, wu, wd]
    if final:
        in_specs.append(pl.BlockSpec((1, D), lambda i, f: (0, 0)))
        args.append(final_g.reshape(1, D))
    return pl.pallas_call(
        functools.partial(_ffn_kernel, final=final),
        grid=(M // tm, F // tf),
        in_specs=in_specs,
        out_specs=pl.BlockSpec((tm, D), lambda i, f: (i, 0)),
        out_shape=jax.ShapeDtypeStruct((M, D), F32),
        scratch_shapes=[pltpu.VMEM((tm, D), BF16)],
        compiler_params=_cparams("parallel", "arbitrary"),
        name="ffn",
    )(*args)


def _inproj_kernel(x_ref, g_ref, w_ref, wgate_ref, o_ref, og_ref, os_ref, h_ref):
    n = pl.program_id(1)

    @pl.when(n == 0)
    def _():
        h = _rms(x_ref[...], g_ref[...]).astype(BF16)
        h_ref[...] = h
        og_ref[...] = _dot(h, wgate_ref[...])

    res = _dot(h_ref[...], w_ref[...])
    o_ref[...] = res.astype(BF16)

    @pl.when(n == pl.num_programs(1) - 1)
    def _():
        os_ref[...] = res[:, res.shape[1] - GROUP_W:]


def _inproj(x, g, w_main, w_gate, tm=1024, tn=1408):
    M, D = x.shape
    N = w_main.shape[1]
    assert tn >= GROUP_W and N % tn == 0
    return pl.pallas_call(
        _inproj_kernel,
        grid=(M // tm, N // tn),
        in_specs=[
            pl.BlockSpec((tm, D), lambda i, n: (i, 0)),
            pl.BlockSpec((1, D), lambda i, n: (0, 0)),
            pl.BlockSpec((D, tn), lambda i, n: (0, n)),
            pl.BlockSpec((D, LANES), lambda i, n: (0, 0)),
        ],
        out_specs=[
            pl.BlockSpec((tm, tn), lambda i, n: (i, n)),
            pl.BlockSpec((tm, LANES), lambda i, n: (i, 0)),
            pl.BlockSpec((tm, GROUP_W), lambda i, n: (i, 0)),
        ],
        out_shape=[
            jax.ShapeDtypeStruct((M, N), BF16),
            jax.ShapeDtypeStruct((M, LANES), F32),
            jax.ShapeDtypeStruct((M, GROUP_W), F32),
        ],
        scratch_shapes=[pltpu.VMEM((tm, D), BF16)],
        compiler_params=_cparams("parallel", "arbitrary"),
        name="inproj",
    )(x, g.reshape(1, D), w_main, w_gate)


def _conv_kernel(x_ref, prev_ref, next_ref, w_ref, b_ref, q_ref, k_ref, *, blocks_per_seq):
    tb = x_ref.shape[0]
    halo = prev_ref.shape[0]
    pos = pl.program_id(0) % blocks_per_seq
    prev = jnp.where(pos != 0, prev_ref[...].astype(F32), 0.0)
    nxt = jnp.where(pos != blocks_per_seq - 1, next_ref[...].astype(F32), 0.0)
    ext = jnp.concatenate([prev, x_ref[...].astype(F32), nxt], axis=0)
    rows = tb + 2 * halo
    pad = MLSTM_CONV // 2
    acc = None
    for j in range(MLSTM_CONV):
        shift = (pad - j) % rows
        sh = ext if shift == 0 else pltpu.roll(ext, shift, axis=0)
        term = sh[halo:halo + tb] * w_ref[j:j + 1, :]
        acc = term if acc is None else acc + term
    acc = acc + b_ref[...]
    y = acc * _sigmoid(acc)
    q_ref[...] = y[:, :GROUP_W].astype(BF16)
    k_ref[...] = (y[:, GROUP_W:] * (MLSTM_DH ** -0.5)).astype(BF16)


def _mlstm_conv(proj, conv_w, conv_b, T, tb=512):
    M = proj.shape[0]
    C = 2 * GROUP_W
    halo = BF16_SUBLANES
    r = tb // halo
    nhalo = M // halo
    w = jnp.zeros((8, C), F32).at[:MLSTM_CONV].set(conv_w)
    return pl.pallas_call(
        functools.partial(_conv_kernel, blocks_per_seq=T // tb),
        grid=(M // tb,),
        in_specs=[
            pl.BlockSpec((tb, C), lambda i: (i, 0)),
            pl.BlockSpec((halo, C), lambda i: (jnp.maximum(i * r - 1, 0), 0)),
            pl.BlockSpec((halo, C), lambda i: (jnp.minimum((i + 1) * r, nhalo - 1), 0)),
            pl.BlockSpec((8, C), lambda i: (0, 0)),
            pl.BlockSpec((1, C), lambda i: (0, 0)),
        ],
        out_specs=[
            pl.BlockSpec((tb, GROUP_W), lambda i: (i, 0)),
            pl.BlockSpec((tb, GROUP_W), lambda i: (i, 0)),
        ],
        out_shape=[jax.ShapeDtypeStruct((M, GROUP_W), BF16)] * 2,
        compiler_params=_cparams("parallel"),
        name="mlstm_conv",
    )(proj, proj, proj, w, conv_b.reshape(1, C))


def _mlstm_kernel(qf_ref, kf_ref, vf_ref, gf_ref, qb_ref, kb_ref, vb_ref, gb_ref, bias_ref,
                  of_ref, ob_ref, c_ref, m_ref):
    L = MLSTM_CHUNK
    nsub = qf_ref.shape[0] // L
    H, dh = MLSTM_HEADS, MLSTM_DH

    @pl.when(pl.program_id(1) == 0)
    def _():
        c_ref[...] = jnp.zeros_like(c_ref)
        m_ref[...] = jnp.zeros_like(m_ref)

    row = lax.broadcasted_iota(jnp.int32, (L, L), 0)
    col = lax.broadcasted_iota(jnp.int32, (L, L), 1)
    ones_blk = jnp.ones((L, dh), BF16)

    dirs = ((qf_ref, kf_ref, vf_ref, gf_ref, of_ref), (qb_ref, kb_ref, vb_ref, gb_ref, ob_ref))
    visits = [(d, (j if d == 0 else nsub - 1 - j) * L) for j in range(nsub) for d in range(2)]
    gate_terms = []
    for d, r0 in visits:
        seen = (col <= row) if d == 0 else (col >= row)
        tri = jnp.where(seen, 1.0, 0.0).astype(BF16)
        gt = dirs[d][3][r0:r0 + L, :] + bias_ref[...]
        lf = jnp.minimum(gt, 0.0) - jnp.log(1.0 + jnp.exp(-jnp.abs(gt)))
        hi, mid, lo = _split3(lf)
        bcum = _dot(tri, hi) + _dot(tri, mid) + _dot(tri, lo)
        gate_terms.append((seen, gt, bcum, bcum.T, gt.T))

    for (d, r0), (seen, gt, bcum, bcum_t, gt_t) in zip(visits, gate_terms):
        q_ref, k_ref, v_ref, _, o_ref = dirs[d]
        rows = slice(r0, r0 + L)
        end = L - 1 if d == 0 else 0
        for h in range(H):
            ic = 2 * H * d + h
            fc = 2 * H * d + H + h
            idx = d * H + h
            b_rep = jnp.broadcast_to(bcum[:, fc:fc + 1], (L, dh))
            i_rep = jnp.broadcast_to(gt[:, ic:ic + 1], (L, dh))
            b_row = bcum_t[fc:fc + 1, :]
            i_row = gt_t[ic:ic + 1, :]
            g_tot = b_rep[end:end + 1, :]
            m_prev = m_ref[idx]
            c_prev = c_ref[idx]
            q = q_ref[rows, h * dh:(h + 1) * dh]
            k = k_ref[rows, h * dh:(h + 1) * dh]
            v_aug = jnp.concatenate([v_ref[rows, h * dh:(h + 1) * dh], ones_blk], axis=1)

            dlog = jnp.where(seen, b_rep - b_row + i_row, -jnp.inf)
            inter = b_rep + m_prev
            m_t = jnp.maximum(inter, jnp.max(dlog, axis=1, keepdims=True))
            s = (_dot_nt(q, k) * jnp.exp(dlog - m_t)).astype(BF16)
            s_inter = jnp.exp(inter - m_t)
            num_aug = jnp.concatenate([s_inter, s_inter], axis=1) * _dot(q, c_prev.astype(BF16)) + _dot(s, v_aug)
            den = jnp.maximum(jnp.abs(num_aug[:, dh:]), jnp.exp(-m_t))
            o_ref[rows, h * dh:(h + 1) * dh] = num_aug[:, :dh] / den

            a_rep = g_tot - b_rep + i_rep
            m_new = jnp.maximum(g_tot + m_prev, jnp.max(a_rep, axis=0, keepdims=True))
            decay = jnp.exp(g_tot + m_prev - m_new)
            kw = (k.astype(F32) * jnp.exp(a_rep - m_new)).astype(BF16)
            c_ref[idx] = jnp.concatenate([decay, decay], axis=1) * c_prev + _dot_tn(kw, v_aug)
            m_ref[idx] = m_new


def _mlstm(qc, kc, proj, gates, gate_bias, B, T):
    M = qc.shape[0]
    L = MLSTM_CHUNK
    assert L == LANES == MLSTM_DH
    tb = MLSTM_CHUNKS_PER_STEP * L
    assert T % tb == 0
    nc = T // tb
    W = GROUP_W
    vblk = 2
    fwd = lambda b, i: (b * nc + i, 0)
    bwd = lambda b, i: (b * nc + nc - 1 - i, 0)
    fwd_v = lambda b, i: (b * nc + i, vblk)
    bwd_v = lambda b, i: (b * nc + nc - 1 - i, vblk)
    bias = jnp.zeros((1, LANES), F32).at[0, :4 * MLSTM_HEADS].set(gate_bias)
    return pl.pallas_call(
        _mlstm_kernel,
        grid=(B, nc),
        in_specs=[
            pl.BlockSpec((tb, W), fwd), pl.BlockSpec((tb, W), fwd), pl.BlockSpec((tb, W), fwd_v),
            pl.BlockSpec((tb, LANES), fwd),
            pl.BlockSpec((tb, W), bwd), pl.BlockSpec((tb, W), bwd), pl.BlockSpec((tb, W), bwd_v),
            pl.BlockSpec((tb, LANES), bwd),
            pl.BlockSpec((1, LANES), lambda b, i: (0, 0)),
        ],
        out_specs=[pl.BlockSpec((tb, W), fwd), pl.BlockSpec((tb, W), bwd)],
        out_shape=[jax.ShapeDtypeStruct((M, W), F32)] * 2,
        scratch_shapes=[
            pltpu.VMEM((2 * MLSTM_HEADS, MLSTM_DH, 2 * MLSTM_DH), F32),
            pltpu.VMEM((2 * MLSTM_HEADS, 1, LANES), F32),
        ],
        compiler_params=_cparams("parallel", "arbitrary"),
        name="mlstm",
    )(qc, kc, proj, gates, qc, kc, proj, gates, bias)


def _rope_kernel(x_ref, cos_ref, sin_ref, q_ref, k_ref):
    x = x_ref[...].astype(F32)
    reps = x.shape[1] // LANES
    cos = jnp.concatenate([cos_ref[...]] * reps, axis=1)
    sin = jnp.concatenate([sin_ref[...]] * reps, axis=1)
    lane = lax.broadcasted_iota(jnp.int32, x.shape, 1)
    width = x.shape[1]
    half = DIFF_DH // 2
    rot = jnp.where(lane % DIFF_DH < half, pltpu.roll(x, width - half, axis=1), pltpu.roll(x, half, axis=1))
    y = x * cos + rot * sin
    q_ref[...] = (y[:, :GROUP_W] * (DIFF_DH ** -0.5 * math.log2(math.e))).astype(BF16)
    k_ref[...] = y[:, GROUP_W:].astype(BF16)


def _rope_tables(T):
    half = DIFF_DH // 2
    inv = 1.0 / (ROPE_THETA ** (jnp.arange(0, DIFF_DH, 2, dtype=F32) / DIFF_DH))
    ang = jnp.arange(T, dtype=F32)[:, None] * inv[None, :]
    cos, sin = jnp.cos(ang), jnp.sin(ang)
    cos_t = jnp.concatenate([cos, cos] * (LANES // DIFF_DH), axis=1)
    sin_t = jnp.concatenate([-sin, sin] * (LANES // DIFF_DH), axis=1)
    return cos_t, sin_t


def _rope(proj, cos_t, sin_t, T, tb=512):
    M = proj.shape[0]
    nb = T // tb
    return pl.pallas_call(
        _rope_kernel,
        grid=(M // tb,),
        in_specs=[
            pl.BlockSpec((tb, 2 * GROUP_W), lambda i: (i, 2)),
            pl.BlockSpec((tb, LANES), lambda i: (i % nb, 0)),
            pl.BlockSpec((tb, LANES), lambda i: (i % nb, 0)),
        ],
        out_specs=[pl.BlockSpec((tb, GROUP_W), lambda i: (i, 0))] * 2,
        out_shape=[jax.ShapeDtypeStruct((M, GROUP_W), BF16)] * 2,
        compiler_params=_cparams("parallel"),
        name="rope",
    )(proj, cos_t, sin_t)


def _diff_kernel(q_ref, k_ref, v_ref, lp_ref, g_ref, o_ref, q2_ref, m_ref, acc_ref, *, lam_init, tks):
    ki = pl.program_id(3)
    tq = q_ref.shape[0]
    dv = v_ref.shape[1]
    ones = jnp.ones((tks, dv), BF16)

    @pl.when(ki == 0)
    def _():
        q = q_ref[...]
        lane = lax.broadcasted_iota(jnp.int32, q.shape, 1)
        zero = jnp.zeros_like(q)
        q2_ref[0:tq, :] = jnp.where(lane < DIFF_DH, q, zero)
        q2_ref[tq:2 * tq, :] = jnp.where(lane < DIFF_DH, zero, q)
        m_ref[...] = jnp.full_like(m_ref, -jnp.inf)
        acc_ref[...] = jnp.zeros_like(acc_ref)

    q2 = q2_ref[...]
    for j in range(k_ref.shape[0] // tks):
        s = _dot_nt(q2, k_ref[j * tks:(j + 1) * tks, :])
        tiles = [s[:, c * LANES:(c + 1) * LANES] for c in range(tks // LANES)]
        m_prev = m_ref[...]
        m_new = jnp.maximum(m_prev, jnp.max(functools.reduce(jnp.maximum, tiles), axis=1, keepdims=True))
        alpha = jnp.exp2(m_prev - m_new)
        p = jnp.concatenate([jnp.exp2(t - m_new).astype(BF16) for t in tiles], axis=1)
        v_aug = jnp.concatenate([v_ref[j * tks:(j + 1) * tks, :], ones], axis=1)
        acc_ref[...] = jnp.concatenate([alpha, alpha], axis=1) * acc_ref[...] + _dot(p, v_aug)
        m_ref[...] = m_new

    @pl.when(ki == pl.num_programs(3) - 1)
    def _():
        lp = lp_ref[...]
        lam = (jnp.exp(jnp.sum(lp[0:1] * lp[1:2], axis=1, keepdims=True))
               - jnp.exp(jnp.sum(lp[2:3] * lp[3:4], axis=1, keepdims=True)) + lam_init)
        o = acc_ref[:, 0:dv] / acc_ref[:, dv:2 * dv]
        out = o[0:tq] - lam * o[tq:2 * tq]
        o_ref[...] = (_rms(out, g_ref[...]) * (1.0 - lam_init)).astype(BF16)


def _diff_attn(qr, kr, proj, lam_params, norm_g, B, T, layer_idx, tq=1024, tk=8192, tks=512):
    M = qr.shape[0]
    dv = 2 * DIFF_DH
    tk = min(tk, T)
    nq, nk = T // tq, T // tk
    vblk0 = (6 * GROUP_W) // dv
    lam_init = 0.8 - 0.6 * math.exp(-0.3 * layer_idx)
    return pl.pallas_call(
        functools.partial(_diff_kernel, lam_init=lam_init, tks=tks),
        grid=(B, DIFF_HEADS, nq, nk),
        in_specs=[
            pl.BlockSpec((tq, dv), lambda b, h, qi, ki: (b * nq + qi, h)),
            pl.BlockSpec((tk, dv), lambda b, h, qi, ki: (b * nk + ki, h)),
            pl.BlockSpec((tk, dv), lambda b, h, qi, ki: (b * nk + ki, vblk0 + h)),
            pl.BlockSpec((4, DIFF_DH), lambda b, h, qi, ki: (0, 0)),
            pl.BlockSpec((1, dv), lambda b, h, qi, ki: (0, 0)),
        ],
        out_specs=pl.BlockSpec((tq, dv), lambda b, h, qi, ki: (b * nq + qi, h)),
        out_shape=jax.ShapeDtypeStruct((M, GROUP_W), BF16),
        scratch_shapes=[
            pltpu.VMEM((2 * tq, dv), BF16),
            pltpu.VMEM((2 * tq, LANES), F32),
            pltpu.VMEM((2 * tq, 2 * dv), F32),
        ],
        compiler_params=_cparams("parallel", "parallel", "parallel", "arbitrary"),
        name="diff_attn",
    )(qr, kr, proj, lam_params, norm_g.reshape(1, dv))


def _na_kernel(q_ref, k_ref, v_ref, bias_ref, o_ref, *, rows):
    g = pl.program_id(2)
    win = NA_WIN_ROWS * GRID_W
    lane = lax.broadcasted_iota(jnp.int32, (GRID_W, 2 * NA_DH), 1)
    first = lane < NA_DH
    w = 2 * NA_DH
    ones = jnp.ones((win, w), BF16)
    scores, starts = [], []
    for jr in range(NA_ROWS_PER_STEP):
        r = g * NA_ROWS_PER_STEP + jr
        rs = jnp.clip(r - NA_WIN_ROWS // 2, 0, rows - NA_WIN_ROWS)
        start = pl.multiple_of(rs * GRID_W, GRID_W)
        q = q_ref[jr * GRID_W:(jr + 1) * GRID_W, :]
        zero = jnp.zeros_like(q)
        q2 = jnp.concatenate([jnp.where(first, q, zero), jnp.where(first, zero, q)], axis=0)
        s = _dot_nt(q2, k_ref[pl.ds(start, win), :]) * (NA_DH ** -0.5 * math.log2(math.e)) + bias_ref[r - rs]
        scores.append(s)
        starts.append(start)
    for jr in range(NA_ROWS_PER_STEP):
        s = scores[jr]
        e = jnp.exp2(s - jnp.max(s, axis=1, keepdims=True)).astype(BF16)
        pv = _dot(e, jnp.concatenate([v_ref[pl.ds(starts[jr], win), :], ones], axis=1))
        pv = pv[:, :w] / pv[:, w:]
        o_ref[jr * GRID_W:(jr + 1) * GRID_W, :] = jnp.where(first, pv[:GRID_W], pv[GRID_W:]).astype(BF16)


def _na_bias_table(rpb):
    c = jnp.arange(GRID_W)
    col_start = jnp.clip(c - NA_WIN_W // 2, 0, GRID_W - NA_WIN_W)
    kc = jnp.arange(GRID_W)
    valid = (kc[None, :] >= col_start[:, None]) & (kc[None, :] < col_start[:, None] + NA_WIN_W)
    col_off = kc[None, :] - c[:, None] + NA_WIN_W - 1
    p = jnp.arange(NA_WIN_ROWS)
    k = jnp.arange(NA_WIN_ROWS)
    row_off = k[None, :] - p[:, None] + NA_WIN_ROWS - 1
    pick_row = (row_off[..., None] == jnp.arange(2 * NA_WIN_ROWS - 1)).astype(F32)
    pick_col = (col_off[..., None] == jnp.arange(2 * NA_WIN_W - 1)).astype(F32)
    t = jnp.einsum('hrq,pkr,cjq->hpckj', rpb.astype(F32), pick_row, pick_col,
                   precision=lax.Precision.HIGHEST) * math.log2(math.e)
    t = jnp.where(valid[None, None, :, None, :], t, MASK_VALUE)
    t = t.reshape(NA_HEADS, NA_WIN_ROWS, GRID_W, NA_WIN_ROWS * GRID_W)
    t = t.reshape(NA_HEADS // 2, 2, NA_WIN_ROWS, GRID_W, NA_WIN_ROWS * GRID_W).transpose(0, 2, 1, 3, 4)
    return t.reshape(NA_HEADS // 2, NA_WIN_ROWS, 2 * GRID_W, NA_WIN_ROWS * GRID_W)


def _na_attn(proj, bias_table, B, T):
    M = proj.shape[0]
    rows = T // GRID_W
    assert rows % NA_ROWS_PER_STEP == 0 and rows >= NA_WIN_ROWS
    tq = NA_ROWS_PER_STEP * GRID_W
    ng = rows // NA_ROWS_PER_STEP
    hp = NA_HEADS // 2
    w = 2 * NA_DH
    qblk0, kblk0, vblk0 = (7 * GROUP_W) // w, (8 * GROUP_W) // w, (9 * GROUP_W) // w
    return pl.pallas_call(
        functools.partial(_na_kernel, rows=rows),
        grid=(B, hp, ng),
        in_specs=[
            pl.BlockSpec((tq, w), lambda b, h, g: (b * ng + g, qblk0 + h)),
            pl.BlockSpec((T, w), lambda b, h, g: (b, kblk0 + h)),
            pl.BlockSpec((T, w), lambda b, h, g: (b, vblk0 + h)),
            pl.BlockSpec((None, NA_WIN_ROWS, 2 * GRID_W, NA_WIN_ROWS * GRID_W), lambda b, h, g: (h, 0, 0, 0)),
        ],
        out_specs=pl.BlockSpec((tq, w), lambda b, h, g: (b * ng + g, h)),
        out_shape=jax.ShapeDtypeStruct((M, GROUP_W), BF16),
        compiler_params=_cparams("parallel", "parallel", "arbitrary"),
        name="na_attn",
    )(proj, proj, proj, bias_table)


def _s5_params(lam_re, lam_im, log_step, b_re, b_im, c_re, c_im, d_skip):
    L, c, P, G = S5_CHUNK, S5_GROUP_CH, S5_STATE, S5_GROUPS
    K8 = LANES // c
    O = G // K8
    lam = lax.complex(lam_re.astype(F32), lam_im.astype(F32))
    step = jnp.exp(log_step.astype(F32))[..., None]
    log_lam_bar = lam * step
    lam_bar = jnp.exp(log_lam_bar)
    b_mat = lax.complex(b_re.astype(F32), b_im.astype(F32))
    c_mat = lax.complex(c_re.astype(F32), c_im.astype(F32))
    b_bar = ((lam_bar - 1.0) / lam)[..., None] * b_mat[None]
    pw = jnp.exp(log_lam_bar[:, None] * jnp.arange(L + 1, dtype=F32)[None, :, None, None])
    cb = c_mat[None, :, :, None, :] * b_bar.transpose(0, 1, 3, 2)[:, :, None, :, :]
    s_i = jnp.arange(L)[:, None]
    t_i = jnp.arange(L)[None, :]
    lag = jnp.stack([t_i - s_i, s_i - t_i])
    lag_pw = jnp.where((lag >= 0)[..., None, None],
                       jnp.exp(log_lam_bar[:, None, None] * jnp.maximum(lag, 0).astype(F32)[..., None, None]),
                       0.0)
    m = jnp.einsum('dgoip,dstgp->gsito', jnp.concatenate([cb.real, -cb.imag], axis=-1),
                   jnp.concatenate([lag_pw.real, lag_pw.imag], axis=-1), precision=lax.Precision.HIGH)
    eye = jnp.eye(K8, dtype=F32)
    lane_of = jnp.arange(K8)[:, None] * c + jnp.arange(c)[None, :]
    sel = (jnp.arange(LANES)[None, None, :] == lane_of[:, :, None]).astype(BF16)
    m8 = jnp.einsum('agsitc,gcl->asgitl', m.astype(BF16).reshape(O, K8, L, c, L, c), sel,
                    preferred_element_type=BF16).reshape(O, L * LANES, L * LANES)
    ef = pw[0][L - 1 - jnp.arange(L)][:, :, :, None] * b_bar[0][None]
    eb = pw[1][jnp.arange(L)][:, :, :, None] * b_bar[1][None]
    to_rows = lambda e: e.transpose(1, 0, 3, 2)
    ef, eb = to_rows(ef), to_rows(eb)
    bs = jnp.concatenate([ef.real, ef.imag, eb.real, eb.imag], axis=-1).astype(BF16)
    bs8 = jnp.einsum('agsik,gh->asgihk', bs.reshape(O, K8, L, c, 4 * P), eye.astype(BF16),
                     preferred_element_type=BF16).reshape(O, L * LANES, K8 * 4 * P)
    of = c_mat[:, None] * pw[0][1:L + 1].transpose(1, 0, 2)[:, :, None, :]
    ob = c_mat[:, None] * pw[1][L - jnp.arange(L)].transpose(1, 0, 2)[:, :, None, :]
    to_cols = lambda o: o.transpose(0, 3, 1, 2)
    of, ob = to_cols(of), to_cols(ob)
    cs = jnp.concatenate([of.real, -of.imag, ob.real, -ob.imag], axis=1)
    cs8 = jnp.einsum('agktc,gcl->agktl', cs.astype(BF16).reshape(O, K8, 4 * P, L, c), sel,
                     preferred_element_type=BF16).reshape(O, K8 * 4 * P, L * LANES)
    al = pw[:, L]
    a1 = jnp.concatenate([al.real, al.real], axis=-1)
    a2 = jnp.concatenate([-al.imag, al.imag], axis=-1)
    a8 = jnp.stack([a1, a2, -a2], axis=1)
    a8 = a8.reshape(2, 3, O, K8, LANES).transpose(2, 0, 1, 3, 4).reshape(O, 6 * K8, LANES)
    d8 = d_skip.astype(F32).reshape(O, 1, LANES)
    return m8.astype(BF16), bs8.astype(BF16), cs8.astype(BF16), a8, d8


def _s5_kernel(u_ref, m_ref, bs_ref, cs_ref, a_ref, d_ref, y_ref, x_ref, sf_ref, sb_ref, sfs_ref, sbs_ref):
    L = S5_CHUNK
    K8 = LANES // S5_GROUP_CH
    sw = 4 * S5_STATE
    nc = u_ref.shape[0] // L
    for s in range(L):
        x_ref[:, s * LANES:(s + 1) * LANES] = u_ref[pl.ds(s, nc, stride=L), :].astype(BF16)
    x = x_ref[...]
    for g in range(K8):
        sg = _dot(x, bs_ref[:, g * sw:(g + 1) * sw])
        sf_ref[g * nc:(g + 1) * nc, :] = sg[:, 0:LANES]
        sb_ref[g * nc:(g + 1) * nc, :] = sg[:, LANES:2 * LANES]
        sfs_ref[g * nc:(g + 1) * nc, :] = pltpu.roll(sg[:, 0:LANES], LANES // 2, axis=1)
        sbs_ref[g * nc:(g + 1) * nc, :] = pltpu.roll(sg[:, LANES:2 * LANES], LANES // 2, axis=1)
    a1f, a2f, a2fs = a_ref[0:K8], a_ref[K8:2 * K8], a_ref[2 * K8:3 * K8]
    a1b, a2b, a2bs = a_ref[3 * K8:4 * K8], a_ref[4 * K8:5 * K8], a_ref[5 * K8:6 * K8]

    def step(c, carry):
        hf, hfs, hb, hbs = carry
        cb = nc - 1 - c
        fwd = pl.ds(c, K8, stride=nc)
        bwd = pl.ds(cb, K8, stride=nc)
        sf, sfs = sf_ref[fwd, :], sfs_ref[fwd, :]
        sb, sbs = sb_ref[bwd, :], sbs_ref[bwd, :]
        sf_ref[fwd, :] = hf
        sb_ref[bwd, :] = hb
        return (a1f * hf + a2f * hfs + sf, a1f * hfs + a2fs * hf + sfs,
                a1b * hb + a2b * hbs + sb, a1b * hbs + a2bs * hb + sbs)

    z = jnp.zeros((K8, LANES), F32)
    lax.fori_loop(0, nc, step, (z, z, z, z), unroll=8)
    y = _dot(x, m_ref[...])
    for g in range(K8):
        hg = jnp.concatenate([sf_ref[g * nc:(g + 1) * nc, :], sb_ref[g * nc:(g + 1) * nc, :]], axis=1)
        y = y + _dot(hg.astype(BF16), cs_ref[g * sw:(g + 1) * sw, :])
    for t in range(L):
        y_ref[pl.ds(t, nc, stride=L), :] = y[:, t * LANES:(t + 1) * LANES]
    y_ref[...] += u_ref[...] * d_ref[...]


def _s5(su, params, B, T):
    m8, bs8, cs8, a8, d8 = params
    L = S5_CHUNK
    nc = T // L
    O = m8.shape[0]
    kw = L * LANES
    sw = 4 * S5_STATE
    K8 = LANES // S5_GROUP_CH
    one = pl.Buffered(1)
    return pl.pallas_call(
        _s5_kernel,
        grid=(O, B),
        in_specs=[
            pl.BlockSpec((T, LANES), lambda o, b: (b, o), pipeline_mode=one),
            pl.BlockSpec((None, kw, kw), lambda o, b: (o, 0, 0), pipeline_mode=one),
            pl.BlockSpec((None, kw, K8 * sw), lambda o, b: (o, 0, 0), pipeline_mode=one),
            pl.BlockSpec((None, K8 * sw, kw), lambda o, b: (o, 0, 0), pipeline_mode=one),
            pl.BlockSpec((None, 6 * K8, LANES), lambda o, b: (o, 0, 0)),
            pl.BlockSpec((None, 1, LANES), lambda o, b: (o, 0, 0)),
        ],
        out_specs=pl.BlockSpec((T, LANES), lambda o, b: (b, o)),
        out_shape=jax.ShapeDtypeStruct((B * T, GROUP_W), F32),
        scratch_shapes=[pltpu.VMEM((nc, kw), BF16)] + [pltpu.VMEM((K8 * nc, LANES), F32)] * 4,
        compiler_params=_cparams("arbitrary", "arbitrary"),
        name="s5",
    )(su, m8, bs8, cs8, a8, d8)


def _outproj_kernel(x_ref, hf_ref, hb_ref, og_ref, mg_ref, yb_ref, yc_ref, ys_ref, gw_ref, gb_ref, w_ref, ng_ref,
                    o_ref, on_ref):
    W = GROUP_W
    h = hf_ref[...] + hb_ref[...]
    parts = []
    for j in range(MLSTM_HEADS):
        hj = h[:, j * MLSTM_DH:(j + 1) * MLSTM_DH]
        parts.append(hj * lax.rsqrt(jnp.mean(hj * hj, axis=-1, keepdims=True) + NORM_EPS))
    hn = jnp.concatenate(parts, axis=1) * mg_ref[...]
    ya = (hn * _sigmoid(og_ref[...].astype(F32))).astype(BF16)
    y = ys_ref[...]
    z = 0.5 * y * (1.0 + jnp.tanh(math.sqrt(2.0 / math.pi) * (y + 0.044715 * (y * y * y))))
    yd = (z * _sigmoid(_dot(z.astype(BF16), gw_ref[...]) + gb_ref[...])).astype(BF16)
    acc = x_ref[...] + _dot(ya, w_ref[0:W, :])
    acc = acc + _dot(yb_ref[...], w_ref[W:2 * W, :])
    acc = acc + _dot(yc_ref[...], w_ref[2 * W:3 * W, :])
    acc = acc + _dot(yd, w_ref[3 * W:4 * W, :])
    o_ref[...] = acc
    on_ref[...] = _rms(acc, ng_ref[...]).astype(BF16)


def _outproj(x, hf, hb, proj, mlstm_norm, yb, yc, ys, glu_w, glu_b, w_out, next_norm, tm=512):
    M, D = x.shape
    W = GROUP_W
    tok = lambda i: (i, 0)
    const = lambda i: (0, 0)
    return pl.pallas_call(
        _outproj_kernel,
        grid=(M // tm,),
        in_specs=[
            pl.BlockSpec((tm, D), tok),
            pl.BlockSpec((tm, W), tok), pl.BlockSpec((tm, W), tok),
            pl.BlockSpec((tm, W), lambda i: (i, 3)),
            pl.BlockSpec((1, W), const),
            pl.BlockSpec((tm, W), tok), pl.BlockSpec((tm, W), tok), pl.BlockSpec((tm, W), tok),
            pl.BlockSpec((W, W), const), pl.BlockSpec((1, W), const),
            pl.BlockSpec((4 * W, D), const),
            pl.BlockSpec((1, D), const),
        ],
        out_specs=[pl.BlockSpec((tm, D), tok), pl.BlockSpec((tm, D), tok)],
        out_shape=[jax.ShapeDtypeStruct((M, D), F32), jax.ShapeDtypeStruct((M, D), BF16)],
        compiler_params=_cparams("parallel"),
        name="outproj",
    )(x, hf, hb, proj, mlstm_norm.reshape(1, W), yb, yc, ys, glu_w, glu_b.reshape(1, W), w_out,
      next_norm.reshape(1, D))


def _prepare_layer(w, l):
    W = GROUP_W
    ng = 4 * MLSTM_HEADS
    w_in = w['w_in'][l]
    w_main = jnp.concatenate([w_in[:, :4 * W], w_in[:, 4 * W + ng:]], axis=1).astype(BF16)
    w_gate = jnp.zeros((w_in.shape[0], LANES), BF16).at[:, :ng].set(w_in[:, 4 * W:4 * W + ng].astype(BF16))
    p = {k: w[k][l] for k in ('ffn1_norm', 'mix_norm', 'ffn2_norm', 'mlstm_conv_w', 'mlstm_conv_b', 'mlstm_gate_bias',
                              'mlstm_norm', 'diff_lambda', 'diff_norm', 's5_glu_b')}
    for k in ('ffn1_w_gate', 'ffn1_w_up', 'ffn1_w_down', 'ffn2_w_gate', 'ffn2_w_up', 'ffn2_w_down', 's5_glu_w', 'w_out'):
        p[k] = w[k][l].astype(BF16)
    p['w_main'], p['w_gate'] = w_main, w_gate
    p['na_bias'] = _na_bias_table(w['na_rpb'][l])
    p['s5'] = _s5_params(w['s5_lambda_re'][l], w['s5_lambda_im'][l], w['s5_log_step'][l], w['s5_b_re'][l],
                         w['s5_b_im'][l], w['s5_c_re'][l], w['s5_c_im'][l], w['s5_d'][l])
    return p


def _mixers(x, p, B, T, layer_idx, rope_tabs):
    proj, gates, su = _inproj(x, p['mix_norm'], p['w_main'], p['w_gate'])
    qc, kc = _mlstm_conv(proj, p['mlstm_conv_w'], p['mlstm_conv_b'], T)
    hf, hb = _mlstm(qc, kc, proj, gates, p['mlstm_gate_bias'], B, T)
    qr, kr = _rope(proj, rope_tabs[0], rope_tabs[1], T)
    yb = _diff_attn(qr, kr, proj, p['diff_lambda'], p['diff_norm'], B, T, layer_idx)
    yc = _na_attn(proj, p['na_bias'], B, T)
    ys = _s5(su, p['s5'], B, T)
    return _outproj(x, hf, hb, proj, p['mlstm_norm'], yb, yc, ys, p['s5_glu_w'], p['s5_glu_b'], p['w_out'],
                    p['ffn2_norm'])


def _trunk(x, layers, final_norm):
    B, T, D = x.shape
    x = x.reshape(B * T, D)
    rope_tabs = _rope_tables(T)
    h = None
    for l, p in enumerate(layers):
        if h is None:
            x = _ffn(x, p['ffn1_norm'], p['ffn1_w_gate'], p['ffn1_w_up'], p['ffn1_w_down'])
        else:
            x = _ffn_prenormed(x, h, p['ffn1_w_gate'], p['ffn1_w_up'], p['ffn1_w_down'])
        x, h = _mixers(x, p, B, T, l, rope_tabs)
        if l == len(layers) - 1:
            x = _ffn_prenormed(x, h, p['ffn2_w_gate'], p['ffn2_w_up'], p['ffn2_w_down'], final_g=final_norm)
        else:
            x, h = _ffn_prenormed(x, h, p['ffn2_w_gate'], p['ffn2_w_up'], p['ffn2_w_down'],
                                  next_g=layers[l + 1]['ffn1_norm'])
    return x.reshape(B, T, D)


def kernel(x_prompt, x_sample, ffn1_norm, ffn1_w_gate, ffn1_w_up, ffn1_w_down, mix_norm, w_in, mlstm_conv_w, mlstm_conv_b, mlstm_gate_bias, mlstm_norm, diff_lambda, diff_norm, na_rpb, s5_lambda_re, s5_lambda_im, s5_log_step, s5_b_re, s5_b_im, s5_c_re, s5_c_im, s5_d, s5_glu_w, s5_glu_b, w_out, ffn2_norm, ffn2_w_gate, ffn2_w_up, ffn2_w_down, final_norm):
    w = dict(ffn1_norm=ffn1_norm, ffn1_w_gate=ffn1_w_gate, ffn1_w_up=ffn1_w_up, ffn1_w_down=ffn1_w_down,
             mix_norm=mix_norm, w_in=w_in, mlstm_conv_w=mlstm_conv_w, mlstm_conv_b=mlstm_conv_b,
             mlstm_gate_bias=mlstm_gate_bias, mlstm_norm=mlstm_norm, diff_lambda=diff_lambda, diff_norm=diff_norm,
             na_rpb=na_rpb, s5_lambda_re=s5_lambda_re, s5_lambda_im=s5_lambda_im, s5_log_step=s5_log_step,
             s5_b_re=s5_b_re, s5_b_im=s5_b_im, s5_c_re=s5_c_re, s5_c_im=s5_c_im, s5_d=s5_d,
             s5_glu_w=s5_glu_w, s5_glu_b=s5_glu_b, w_out=w_out, ffn2_norm=ffn2_norm, ffn2_w_gate=ffn2_w_gate,
             ffn2_w_up=ffn2_w_up, ffn2_w_down=ffn2_w_down)
    layers = [_prepare_layer(w, l) for l in range(ffn1_norm.shape[0])]
    return (_trunk(x_prompt, layers, final_norm), _trunk(x_sample, layers, final_norm))
```

```python
import functools
import math

import jax
import jax.numpy as jnp
from jax import lax
from jax.experimental import pallas as pl
from jax.experimental.pallas import tpu as pltpu

F32 = jnp.float32
BF16 = jnp.bfloat16

NORM_EPS = 1e-6
GROUP_W = 512
GRID_W = 64
MLSTM_HEADS = 4
MLSTM_DH = 128
MLSTM_CONV = 5
MLSTM_CHUNK = 128
MLSTM_CHUNKS_PER_STEP = 8
DIFF_HEADS = 4
DIFF_DH = 64
ROPE_THETA = 10000.0
NA_HEADS = 8
NA_DH = 64
NA_WIN_ROWS = 8
NA_WIN_W = 16
NA_ROWS_PER_STEP = 32
S5_GROUP_CH = 16
S5_GROUPS = 32
S5_STATE = 64
S5_CHUNK = 16
LANES = 128
BF16_SUBLANES = 16
MASK_VALUE = -1e30
VMEM_LIMIT = 56 * 1024 * 1024


def _cparams(*sem):
    return pltpu.CompilerParams(dimension_semantics=sem, vmem_limit_bytes=VMEM_LIMIT)


def _rms(x, g):
    return x * lax.rsqrt(jnp.mean(x * x, axis=-1, keepdims=True) + NORM_EPS) * g


def _sigmoid(x):
    return 1.0 / (1.0 + jnp.exp(-x))


def _dot(a, b):
    return jnp.dot(a, b, preferred_element_type=F32)


def _dot_nt(a, b):
    return lax.dot_general(a, b, (((1,), (1,)), ((), ())), preferred_element_type=F32)


def _dot_tn(a, b):
    return lax.dot_general(a, b, (((0,), (0,)), ((), ())), preferred_element_type=F32)


def _split3(x):
    hi = x.astype(BF16)
    r1 = x - hi.astype(F32)
    mid = r1.astype(BF16)
    lo = (r1 - mid.astype(F32)).astype(BF16)
    return hi, mid, lo


def _ffn_kernel(x_ref, g_ref, wg_ref, wu_ref, wd_ref, *rest, final):
    if final:
        fg_ref, o_ref, h_ref = rest
    else:
        o_ref, h_ref = rest
    f = pl.program_id(1)
    nf = pl.num_programs(1)

    @pl.when(f == 0)
    def _():
        h_ref[...] = _rms(x_ref[...], g_ref[...]).astype(BF16)
        o_ref[...] = x_ref[...]

    h = h_ref[...]
    gate = _dot(h, wg_ref[...])
    up = _dot(h, wu_ref[...])
    act = (0.5 * gate * _sigmoid(gate) * up).astype(BF16)
    nchunk = 4
    cw = o_ref.shape[1] // nchunk
    for c in range(nchunk):
        o_ref[:, c * cw:(c + 1) * cw] += _dot(act, wd_ref[:, c * cw:(c + 1) * cw])

    if final:
        @pl.when(f == nf - 1)
        def _():
            o_ref[...] = _rms(o_ref[...], fg_ref[...])


def _ffn_prenormed_kernel(x_hbm, h_ref, wg_ref, wu_ref, wd_ref, *rest, final):
    if final:
        fg_ref, o_ref, sem = rest
    else:
        o_ref, sem = rest
    i = pl.program_id(0)
    f = pl.program_id(1)
    tm = o_ref.shape[0]
    residual = pltpu.make_async_copy(x_hbm.at[pl.ds(i * tm, tm), :], o_ref, sem)

    @pl.when(f == 0)
    def _():
        residual.start()

    h = h_ref[...]
    gate = _dot(h, wg_ref[...])
    up = _dot(h, wu_ref[...])
    act = (0.5 * gate * _sigmoid(gate) * up).astype(BF16)

    @pl.when(f == 0)
    def _():
        residual.wait()

    nchunk = 4
    cw = o_ref.shape[1] // nchunk
    for c in range(nchunk):
        o_ref[:, c * cw:(c + 1) * cw] += _dot(act, wd_ref[:, c * cw:(c + 1) * cw])

    if final:
        @pl.when(f == pl.num_programs(1) - 1)
        def _():
            o_ref[...] = _rms(o_ref[...], fg_ref[...])


def _ffn_prenormed(x, h, wg, wu, wd, final_g=None, tm=1024, tf=512):
    M, D = x.shape
    F = wg.shape[1]
    final = final_g is not None
    in_specs = [
        pl.BlockSpec(memory_space=pl.ANY),
        pl.BlockSpec((tm, D), lambda i, f: (i, 0)),
        pl.BlockSpec((D, tf), lambda i, f: (0, f)),
        pl.BlockSpec((D, tf), lambda i, f: (0, f)),
        pl.BlockSpec((tf, D), lambda i, f: (f, 0)),
    ]
    args = [x, h, wg, wu, wd]
    if final:
        in_specs.append(pl.BlockSpec((1, D), lambda i, f: (0, 0)))
        args.append(final_g.reshape(1, D))
    return pl.pallas_call(
        functools.partial(_ffn_prenormed_kernel, final=final),
        grid=(M // tm, F // tf),
        in_specs=in_specs,
        out_specs=pl.BlockSpec((tm, D), lambda i, f: (i, 0)),
        out_shape=jax.ShapeDtypeStruct((M, D), F32),
        scratch_shapes=[pltpu.SemaphoreType.DMA(())],
        compiler_params=_cparams("parallel", "arbitrary"),
        name="ffn_prenormed",
    )(*args)


def _ffn(x, g, wg, wu, wd, final_g=None, tm=1024, tf=512):
    M, D = x.shape
    F = wg.shape[1]
    final = final_g is not None
    in_specs = [
        pl.BlockSpec((tm, D), lambda i, f: (i, 0), pipeline_mode=pl.Buffered(1)),
        pl.BlockSpec((1, D), lambda i, f: (0, 0)),
        pl.BlockSpec((D, tf), lambda i, f: (0, f)),
        pl.BlockSpec((D, tf), lambda i, f: (0, f)),
        pl.BlockSpec((tf, D), lambda i, f: (f, 0)),
    ]
    args = [x, g.reshape(1, D), wg, wu, wd]
    if final:
        in_specs.append(pl.BlockSpec((1, D), lambda i, f: (0, 0)))
        args.append(final_g.reshape(1, D))
    return pl.pallas_call(
        functools.partial(_ffn_kernel, final=final),
        grid=(M // tm, F // tf),
        in_specs=in_specs,
        out_specs=pl.BlockSpec((tm, D), lambda i, f: (i, 0)),
        out_shape=jax.ShapeDtypeStruct((M, D), F32),
        scratch_shapes=[pltpu.VMEM((tm, D), BF16)],
        compiler_params=_cparams("parallel", "arbitrary"),
        name="ffn",
    )(*args)


def _inproj_kernel(x_ref, g_ref, w_ref, wgate_ref, o_ref, og_ref, os_ref, h_ref):
    n = pl.program_id(1)

    @pl.when(n == 0)
    def _():
        h = _rms(x_ref[...], g_ref[...]).astype(BF16)
        h_ref[...] = h
        og_ref[...] = _dot(h, wgate_ref[...])

    res = _dot(h_ref[...], w_ref[...])
    o_ref[...] = res.astype(BF16)

    @pl.when(n == pl.num_programs(1) - 1)
    def _():
        os_ref[...] = res[:, res.shape[1] - GROUP_W:]


def _inproj(x, g, w_main, w_gate, tm=1024, tn=1408):
    M, D = x.shape
    N = w_main.shape[1]
    assert tn >= GROUP_W and N % tn == 0
    return pl.pallas_call(
        _inproj_kernel,
        grid=(M // tm, N // tn),
        in_specs=[
            pl.BlockSpec((tm, D), lambda i, n: (i, 0)),
            pl.BlockSpec((1, D), lambda i, n: (0, 0)),
            pl.BlockSpec((D, tn), lambda i, n: (0, n)),
            pl.BlockSpec((D, LANES), lambda i, n: (0, 0)),
        ],
        out_specs=[
            pl.BlockSpec((tm, tn), lambda i, n: (i, n)),
            pl.BlockSpec((tm, LANES), lambda i, n: (i, 0)),
            pl.BlockSpec((tm, GROUP_W), lambda i, n: (i, 0)),
        ],
        out_shape=[
            jax.ShapeDtypeStruct((M, N), BF16),
            jax.ShapeDtypeStruct((M, LANES), F32),
            jax.ShapeDtypeStruct((M, GROUP_W), F32),
        ],
        scratch_shapes=[pltpu.VMEM((tm, D), BF16)],
        compiler_params=_cparams("parallel", "arbitrary"),
        name="inproj",
    )(x, g.reshape(1, D), w_main, w_gate)


def _conv_kernel(x_ref, prev_ref, next_ref, w_ref, b_ref, q_ref, k_ref, *, blocks_per_seq):
    tb = x_ref.shape[0]
    halo = prev_ref.shape[0]
    pos = pl.program_id(0) % blocks_per_seq
    prev = jnp.where(pos != 0, prev_ref[...].astype(F32), 0.0)
    nxt = jnp.where(pos != blocks_per_seq - 1, next_ref[...].astype(F32), 0.0)
    ext = jnp.concatenate([prev, x_ref[...].astype(F32), nxt], axis=0)
    rows = tb + 2 * halo
    pad = MLSTM_CONV // 2
    acc = None
    for j in range(MLSTM_CONV):
        shift = (pad - j) % rows
        sh = ext if shift == 0 else pltpu.roll(ext, shift, axis=0)
        term = sh[halo:halo + tb] * w_ref[j:j + 1, :]
        acc = term if acc is None else acc + term
    acc = acc + b_ref[...]
    y = acc * _sigmoid(acc)
    q_ref[...] = y[:, :GROUP_W].astype(BF16)
    k_ref[...] = (y[:, GROUP_W:] * (MLSTM_DH ** -0.5)).astype(BF16)


def _mlstm_conv(proj, conv_w, conv_b, T, tb=512):
    M = proj.shape[0]
    C = 2 * GROUP_W
    halo = BF16_SUBLANES
    r = tb // halo
    nhalo = M // halo
    w = jnp.zeros((8, C), F32).at[:MLSTM_CONV].set(conv_w)
    return pl.pallas_call(
        functools.partial(_conv_kernel, blocks_per_seq=T // tb),
        grid=(M // tb,),
        in_specs=[
            pl.BlockSpec((tb, C), lambda i: (i, 0)),
            pl.BlockSpec((halo, C), lambda i: (jnp.maximum(i * r - 1, 0), 0)),
            pl.BlockSpec((halo, C), lambda i: (jnp.minimum((i + 1) * r, nhalo - 1), 0)),
            pl.BlockSpec((8, C), lambda i: (0, 0)),
            pl.BlockSpec((1, C), lambda i: (0, 0)),
        ],
        out_specs=[
            pl.BlockSpec((tb, GROUP_W), lambda i: (i, 0)),
            pl.BlockSpec((tb, GROUP_W), lambda i: (i, 0)),
        ],
        out_shape=[jax.ShapeDtypeStruct((M, GROUP_W), BF16)] * 2,
        compiler_params=_cparams("parallel"),
        name="mlstm_conv",
    )(proj, proj, proj, w, conv_b.reshape(1, C))


def _mlstm_kernel(qf_ref, kf_ref, vf_ref, gf_ref, qb_ref, kb_ref, vb_ref, gb_ref, bias_ref,
                  of_ref, ob_ref, c_ref, m_ref):
    L = MLSTM_CHUNK
    nsub = qf_ref.shape[0] // L
    H, dh = MLSTM_HEADS, MLSTM_DH

    @pl.when(pl.program_id(1) == 0)
    def _():
        c_ref[...] = jnp.zeros_like(c_ref)
        m_ref[...] = jnp.zeros_like(m_ref)

    row = lax.broadcasted_iota(jnp.int32, (L, L), 0)
    col = lax.broadcasted_iota(jnp.int32, (L, L), 1)
    ones_blk = jnp.ones((L, dh), BF16)

    dirs = ((qf_ref, kf_ref, vf_ref, gf_ref, of_ref), (qb_ref, kb_ref, vb_ref, gb_ref, ob_ref))
    visits = [(d, (j if d == 0 else nsub - 1 - j) * L) for j in range(nsub) for d in range(2)]
    gate_terms = []
    for d, r0 in visits:
        seen = (col <= row) if d == 0 else (col >= row)
        tri = jnp.where(seen, 1.0, 0.0).astype(BF16)
        gt = dirs[d][3][r0:r0 + L, :] + bias_ref[...]
        lf = jnp.minimum(gt, 0.0) - jnp.log(1.0 + jnp.exp(-jnp.abs(gt)))
        hi, mid, lo = _split3(lf)
        bcum = _dot(tri, hi) + _dot(tri, mid) + _dot(tri, lo)
        gate_terms.append((seen, gt, bcum, bcum.T, gt.T))

    for (d, r0), (seen, gt, bcum, bcum_t, gt_t) in zip(visits, gate_terms):
        q_ref, k_ref, v_ref, _, o_ref = dirs[d]
        rows = slice(r0, r0 + L)
        end = L - 1 if d == 0 else 0
        for h in range(H):
            ic = 2 * H * d + h
            fc = 2 * H * d + H + h
            idx = d * H + h
            b_rep = jnp.broadcast_to(bcum[:, fc:fc + 1], (L, dh))
            i_rep = jnp.broadcast_to(gt[:, ic:ic + 1], (L, dh))
            b_row = bcum_t[fc:fc + 1, :]
            i_row = gt_t[ic:ic + 1, :]
            g_tot = b_rep[end:end + 1, :]
            m_prev = m_ref[idx]
            c_prev = c_ref[idx]
            q = q_ref[rows, h * dh:(h + 1) * dh]
            k = k_ref[rows, h * dh:(h + 1) * dh]
            v_aug = jnp.concatenate([v_ref[rows, h * dh:(h + 1) * dh], ones_blk], axis=1)

            dlog = jnp.where(seen, b_rep - b_row + i_row, -jnp.inf)
            inter = b_rep + m_prev
            m_t = jnp.maximum(inter, jnp.max(dlog, axis=1, keepdims=True))
            s = (_dot_nt(q, k) * jnp.exp(dlog - m_t)).astype(BF16)
            s_inter = jnp.exp(inter - m_t)
            num_aug = jnp.concatenate([s_inter, s_inter], axis=1) * _dot(q, c_prev.astype(BF16)) + _dot(s, v_aug)
            den = jnp.maximum(jnp.abs(num_aug[:, dh:]), jnp.exp(-m_t))
            o_ref[rows, h * dh:(h + 1) * dh] = num_aug[:, :dh] / den

            a_rep = g_tot - b_rep + i_rep
            m_new = jnp.maximum(g_tot + m_prev, jnp.max(a_rep, axis=0, keepdims=True))
            decay = jnp.exp(g_tot + m_prev - m_new)
            kw = (k.astype(F32) * jnp.exp(a_rep - m_new)).astype(BF16)
            c_ref[idx] = jnp.concatenate([decay, decay], axis=1) * c_prev + _dot_tn(kw, v_aug)
            m_ref[idx] = m_new


def _mlstm(qc, kc, proj, gates, gate_bias, B, T):
    M = qc.shape[0]
    L = MLSTM_CHUNK
    assert L == LANES == MLSTM_DH
    tb = MLSTM_CHUNKS_PER_STEP * L
    assert T % tb == 0
    nc = T // tb
    W = GROUP_W
    vblk = 2
    fwd = lambda b, i: (b * nc + i, 0)
    bwd = lambda b, i: (b * nc + nc - 1 - i, 0)
    fwd_v = lambda b, i: (b * nc + i, vblk)
    bwd_v = lambda b, i: (b * nc + nc - 1 - i, vblk)
    bias = jnp.zeros((1, LANES), F32).at[0, :4 * MLSTM_HEADS].set(gate_bias)
    return pl.pallas_call(
        _mlstm_kernel,
        grid=(B, nc),
        in_specs=[
            pl.BlockSpec((tb, W), fwd), pl.BlockSpec((tb, W), fwd), pl.BlockSpec((tb, W), fwd_v),
            pl.BlockSpec((tb, LANES), fwd),
            pl.BlockSpec((tb, W), bwd), pl.BlockSpec((tb, W), bwd), pl.BlockSpec((tb, W), bwd_v),
            pl.BlockSpec((tb, LANES), bwd),
            pl.BlockSpec((1, LANES), lambda b, i: (0, 0)),
        ],
        out_specs=[pl.BlockSpec((tb, W), fwd), pl.BlockSpec((tb, W), bwd)],
        out_shape=[jax.ShapeDtypeStruct((M, W), F32)] * 2,
        scratch_shapes=[
            pltpu.VMEM((2 * MLSTM_HEADS, MLSTM_DH, 2 * MLSTM_DH), F32),
            pltpu.VMEM((2 * MLSTM_HEADS, 1, LANES), F32),
        ],
        compiler_params=_cparams("parallel", "arbitrary"),
        name="mlstm",
    )(qc, kc, proj, gates, qc, kc, proj, gates, bias)


def _rope_kernel(x_ref, cos_ref, sin_ref, q_ref, k_ref):
    x = x_ref[...].astype(F32)
    reps = x.shape[1] // LANES
    cos = jnp.concatenate([cos_ref[...]] * reps, axis=1)
    sin = jnp.concatenate([sin_ref[...]] * reps, axis=1)
    lane = lax.broadcasted_iota(jnp.int32, x.shape, 1)
    width = x.shape[1]
    half = DIFF_DH // 2
    rot = jnp.where(lane % DIFF_DH < half, pltpu.roll(x, width - half, axis=1), pltpu.roll(x, half, axis=1))
    y = x * cos + rot * sin
    q_ref[...] = (y[:, :GROUP_W] * (DIFF_DH ** -0.5 * math.log2(math.e))).astype(BF16)
    k_ref[...] = y[:, GROUP_W:].astype(BF16)


def _rope_tables(T):
    half = DIFF_DH // 2
    inv = 1.0 / (ROPE_THETA ** (jnp.arange(0, DIFF_DH, 2, dtype=F32) / DIFF_DH))
    ang = jnp.arange(T, dtype=F32)[:, None] * inv[None, :]
    cos, sin = jnp.cos(ang), jnp.sin(ang)
    cos_t = jnp.concatenate([cos, cos] * (LANES // DIFF_DH), axis=1)
    sin_t = jnp.concatenate([-sin, sin] * (LANES // DIFF_DH), axis=1)
    return cos_t, sin_t


def _rope(proj, cos_t, sin_t, T, tb=512):
    M = proj.shape[0]
    nb = T // tb
    return pl.pallas_call(
        _rope_kernel,
        grid=(M // tb,),
        in_specs=[
            pl.BlockSpec((tb, 2 * GROUP_W), lambda i: (i, 2)),
            pl.BlockSpec((tb, LANES), lambda i: (i % nb, 0)),
            pl.BlockSpec((tb, LANES), lambda i: (i % nb, 0)),
        ],
        out_specs=[pl.BlockSpec((tb, GROUP_W), lambda i: (i, 0))] * 2,
        out_shape=[jax.ShapeDtypeStruct((M, GROUP_W), BF16)] * 2,
        compiler_params=_cparams("parallel"),
        name="rope",
    )(proj, cos_t, sin_t)


def _diff_kernel(q_ref, k_ref, v_ref, lp_ref, g_ref, o_ref, q2_ref, m_ref, acc_ref, *, lam_init, tks):
    ki = pl.program_id(3)
    tq = q_ref.shape[0]
    dv = v_ref.shape[1]
    ones = jnp.ones((tks, dv), BF16)

    @pl.when(ki == 0)
    def _():
        q = q_ref[...]
        lane = lax.broadcasted_iota(jnp.int32, q.shape, 1)
        zero = jnp.zeros_like(q)
        q2_ref[0:tq, :] = jnp.where(lane < DIFF_DH, q, zero)
        q2_ref[tq:2 * tq, :] = jnp.where(lane < DIFF_DH, zero, q)
        m_ref[...] = jnp.full_like(m_ref, -jnp.inf)
        acc_ref[...] = jnp.zeros_like(acc_ref)

    q2 = q2_ref[...]
    for j in range(k_ref.shape[0] // tks):
        s = _dot_nt(q2, k_ref[j * tks:(j + 1) * tks, :])
        tiles = [s[:, c * LANES:(c + 1) * LANES] for c in range(tks // LANES)]
        m_prev = m_ref[...]
        m_new = jnp.maximum(m_prev, jnp.max(functools.reduce(jnp.maximum, tiles), axis=1, keepdims=True))
        alpha = jnp.exp2(m_prev - m_new)
        p = jnp.concatenate([jnp.exp2(t - m_new).astype(BF16) for t in tiles], axis=1)
        v_aug = jnp.concatenate([v_ref[j * tks:(j + 1) * tks, :], ones], axis=1)
        acc_ref[...] = jnp.concatenate([alpha, alpha], axis=1) * acc_ref[...] + _dot(p, v_aug)
        m_ref[...] = m_new

    @pl.when(ki == pl.num_programs(3) - 1)
    def _():
        lp = lp_ref[...]
        lam = (jnp.exp(jnp.sum(lp[0:1] * lp[1:2], axis=1, keepdims=True))
               - jnp.exp(jnp.sum(lp[2:3] * lp[3:4], axis=1, keepdims=True)) + lam_init)
        o = acc_ref[:, 0:dv] / acc_ref[:, dv:2 * dv]
        out = o[0:tq] - lam * o[tq:2 * tq]
        o_ref[...] = (_rms(out, g_ref[...]) * (1.0 - lam_init)).astype(BF16)


def _diff_attn(qr, kr, proj, lam_params, norm_g, B, T, layer_idx, tq=1024, tk=8192, tks=512):
    M = qr.shape[0]
    dv = 2 * DIFF_DH
    tk = min(tk, T)
    nq, nk = T // tq, T // tk
    vblk0 = (6 * GROUP_W) // dv
    lam_init = 0.8 - 0.6 * math.exp(-0.3 * layer_idx)
    return pl.pallas_call(
        functools.partial(_diff_kernel, lam_init=lam_init, tks=tks),
        grid=(B, DIFF_HEADS, nq, nk),
        in_specs=[
            pl.BlockSpec((tq, dv), lambda b, h, qi, ki: (b * nq + qi, h)),
            pl.BlockSpec((tk, dv), lambda b, h, qi, ki: (b * nk + ki, h)),
            pl.BlockSpec((tk, dv), lambda b, h, qi, ki: (b * nk + ki, vblk0 + h)),
            pl.BlockSpec((4, DIFF_DH), lambda b, h, qi, ki: (0, 0)),
            pl.BlockSpec((1, dv), lambda b, h, qi, ki: (0, 0)),
        ],
        out_specs=pl.BlockSpec((tq, dv), lambda b, h, qi, ki: (b * nq + qi, h)),
        out_shape=jax.ShapeDtypeStruct((M, GROUP_W), BF16),
        scratch_shapes=[
            pltpu.VMEM((2 * tq, dv), BF16),
            pltpu.VMEM((2 * tq, LANES), F32),
            pltpu.VMEM((2 * tq, 2 * dv), F32),
        ],
        compiler_params=_cparams("parallel", "parallel", "parallel", "arbitrary"),
        name="diff_attn",
    )(qr, kr, proj, lam_params, norm_g.reshape(1, dv))


def _na_kernel(q_ref, k_ref, v_ref, bias_ref, o_ref, *, rows):
    g = pl.program_id(2)
    win = NA_WIN_ROWS * GRID_W
    lane = lax.broadcasted_iota(jnp.int32, (GRID_W, 2 * NA_DH), 1)
    first = lane < NA_DH
    w = 2 * NA_DH
    ones = jnp.ones((win, w), BF16)
    scores, starts = [], []
    for jr in range(NA_ROWS_PER_STEP):
        r = g * NA_ROWS_PER_STEP + jr
        rs = jnp.clip(r - NA_WIN_ROWS // 2, 0, rows - NA_WIN_ROWS)
        start = pl.multiple_of(rs * GRID_W, GRID_W)
        q = q_ref[jr * GRID_W:(jr + 1) * GRID_W, :]
        zero = jnp.zeros_like(q)
        q2 = jnp.concatenate([jnp.where(first, q, zero), jnp.where(first, zero, q)], axis=0)
        s = _dot_nt(q2, k_ref[pl.ds(start, win), :]) * (NA_DH ** -0.5 * math.log2(math.e)) + bias_ref[r - rs]
        scores.append(s)
        starts.append(start)
    for jr in range(NA_ROWS_PER_STEP):
        s = scores[jr]
        e = jnp.exp2(s - jnp.max(s, axis=1, keepdims=True)).astype(BF16)
        pv = _dot(e, jnp.concatenate([v_ref[pl.ds(starts[jr], win), :], ones], axis=1))
        pv = pv[:, :w] / pv[:, w:]
        o_ref[jr * GRID_W:(jr + 1) * GRID_W, :] = jnp.where(first, pv[:GRID_W], pv[GRID_W:]).astype(BF16)


def _na_bias_table(rpb):
    c = jnp.arange(GRID_W)
    col_start = jnp.clip(c - NA_WIN_W // 2, 0, GRID_W - NA_WIN_W)
    kc = jnp.arange(GRID_W)
    valid = (kc[None, :] >= col_start[:, None]) & (kc[None, :] < col_start[:, None] + NA_WIN_W)
    col_off = kc[None, :] - c[:, None] + NA_WIN_W - 1
    p = jnp.arange(NA_WIN_ROWS)
    k = jnp.arange(NA_WIN_ROWS)
    row_off = k[None, :] - p[:, None] + NA_WIN_ROWS - 1
    pick_row = (row_off[..., None] == jnp.arange(2 * NA_WIN_ROWS - 1)).astype(F32)
    pick_col = (col_off[..., None] == jnp.arange(2 * NA_WIN_W - 1)).astype(F32)
    t = jnp.einsum('hrq,pkr,cjq->hpckj', rpb.astype(F32), pick_row, pick_col,
                   precision=lax.Precision.HIGHEST) * math.log2(math.e)
    t = jnp.where(valid[None, None, :, None, :], t, MASK_VALUE)
    t = t.reshape(NA_HEADS, NA_WIN_ROWS, GRID_W, NA_WIN_ROWS * GRID_W)
    t = t.reshape(NA_HEADS // 2, 2, NA_WIN_ROWS, GRID_W, NA_WIN_ROWS * GRID_W).transpose(0, 2, 1, 3, 4)
    return t.reshape(NA_HEADS // 2, NA_WIN_ROWS, 2 * GRID_W, NA_WIN_ROWS * GRID_W)


def _na_attn(proj, bias_table, B, T):
    M = proj.shape[0]
    rows = T // GRID_W
    assert rows % NA_ROWS_PER_STEP == 0 and rows >= NA_WIN_ROWS
    tq = NA_ROWS_PER_STEP * GRID_W
    ng = rows // NA_ROWS_PER_STEP
    hp = NA_HEADS // 2
    w = 2 * NA_DH
    qblk0, kblk0, vblk0 = (7 * GROUP_W) // w, (8 * GROUP_W) // w, (9 * GROUP_W) // w
    return pl.pallas_call(
        functools.partial(_na_kernel, rows=rows),
        grid=(B, hp, ng),
        in_specs=[
            pl.BlockSpec((tq, w), lambda b, h, g: (b * ng + g, qblk0 + h)),
            pl.BlockSpec((T, w), lambda b, h, g: (b, kblk0 + h)),
            pl.BlockSpec((T, w), lambda b, h, g: (b, vblk0 + h)),
            pl.BlockSpec((None, NA_WIN_ROWS, 2 * GRID_W, NA_WIN_ROWS * GRID_W), lambda b, h, g: (h, 0, 0, 0)),
        ],
        out_specs=pl.BlockSpec((tq, w), lambda b, h, g: (b * ng + g, h)),
        out_shape=jax.ShapeDtypeStruct((M, GROUP_W), BF16),
        compiler_params=_cparams("parallel", "parallel", "arbitrary"),
        name="na_attn",
    )(proj, proj, proj, bias_table)


def _s5_params(lam_re, lam_im, log_step, b_re, b_im, c_re, c_im, d_skip):
    L, c, P, G = S5_CHUNK, S5_GROUP_CH, S5_STATE, S5_GROUPS
    K8 = LANES // c
    O = G // K8
    lam = lax.complex(lam_re.astype(F32), lam_im.astype(F32))
    step = jnp.exp(log_step.astype(F32))[..., None]
    log_lam_bar = lam * step
    lam_bar = jnp.exp(log_lam_bar)
    b_mat = lax.complex(b_re.astype(F32), b_im.astype(F32))
    c_mat = lax.complex(c_re.astype(F32), c_im.astype(F32))
    b_bar = ((lam_bar - 1.0) / lam)[..., None] * b_mat[None]
    pw = jnp.exp(log_lam_bar[:, None] * jnp.arange(L + 1, dtype=F32)[None, :, None, None])
    cb = c_mat[None, :, :, None, :] * b_bar.transpose(0, 1, 3, 2)[:, :, None, :, :]
    s_i = jnp.arange(L)[:, None]
    t_i = jnp.arange(L)[None, :]
    lag = jnp.stack([t_i - s_i, s_i - t_i])
    lag_pw = jnp.where((lag >= 0)[..., None, None],
                       jnp.exp(log_lam_bar[:, None, None] * jnp.maximum(lag, 0).astype(F32)[..., None, None]),
                       0.0)
    m = jnp.einsum('dgoip,dstgp->gsito', jnp.concatenate([cb.real, -cb.imag], axis=-1),
                   jnp.concatenate([lag_pw.real, lag_pw.imag], axis=-1), precision=lax.Precision.HIGH)
    eye = jnp.eye(K8, dtype=F32)
    lane_of = jnp.arange(K8)[:, None] * c + jnp.arange(c)[None, :]
    sel = (jnp.arange(LANES)[None, None, :] == lane_of[:, :, None]).astype(BF16)
    m8 = jnp.einsum('agsitc,gcl->asgitl', m.astype(BF16).reshape(O, K8, L, c, L, c), sel,
                    preferred_element_type=BF16).reshape(O, L * LANES, L * LANES)
    ef = pw[0][L - 1 - jnp.arange(L)][:, :, :, None] * b_bar[0][None]
    eb = pw[1][jnp.arange(L)][:, :, :, None] * b_bar[1][None]
    to_rows = lambda e: e.transpose(1, 0, 3, 2)
    ef, eb = to_rows(ef), to_rows(eb)
    bs = jnp.concatenate([ef.real, ef.imag, eb.real, eb.imag], axis=-1).astype(BF16)
    bs8 = jnp.einsum('agsik,gh->asgihk', bs.reshape(O, K8, L, c, 4 * P), eye.astype(BF16),
                     preferred_element_type=BF16).reshape(O, L * LANES, K8 * 4 * P)
    of = c_mat[:, None] * pw[0][1:L + 1].transpose(1, 0, 2)[:, :, None, :]
    ob = c_mat[:, None] * pw[1][L - jnp.arange(L)].transpose(1, 0, 2)[:, :, None, :]
    to_cols = lambda o: o.transpose(0, 3, 1, 2)
    of, ob = to_cols(of), to_cols(ob)
    cs = jnp.concatenate([of.real, -of.imag, ob.real, -ob.imag], axis=1)
    cs8 = jnp.einsum('agktc,gcl->agktl', cs.astype(BF16).reshape(O, K8, 4 * P, L, c), sel,
                     preferred_element_type=BF16).reshape(O, K8 * 4 * P, L * LANES)
    al = pw[:, L]
    a1 = jnp.concatenate([al.real, al.real], axis=-1)
    a2 = jnp.concatenate([-al.imag, al.imag], axis=-1)
    a8 = jnp.stack([a1, a2, -a2], axis=1)
    a8 = a8.reshape(2, 3, O, K8, LANES).transpose(2, 0, 1, 3, 4).reshape(O, 6 * K8, LANES)
    d8 = d_skip.astype(F32).reshape(O, 1, LANES)
    return m8.astype(BF16), bs8.astype(BF16), cs8.astype(BF16), a8, d8


def _s5_kernel(u_ref, m_ref, bs_ref, cs_ref, a_ref, d_ref, y_ref, x_ref, sf_ref, sb_ref, sfs_ref, sbs_ref):
    L = S5_CHUNK
    K8 = LANES // S5_GROUP_CH
    sw = 4 * S5_STATE
    nc = u_ref.shape[0] // L
    for s in range(L):
        x_ref[:, s * LANES:(s + 1) * LANES] = u_ref[pl.ds(s, nc, stride=L), :].astype(BF16)
    x = x_ref[...]
    for g in range(K8):
        sg = _dot(x, bs_ref[:, g * sw:(g + 1) * sw])
        sf_ref[g * nc:(g + 1) * nc, :] = sg[:, 0:LANES]
        sb_ref[g * nc:(g + 1) * nc, :] = sg[:, LANES:2 * LANES]
        sfs_ref[g * nc:(g + 1) * nc, :] = pltpu.roll(sg[:, 0:LANES], LANES // 2, axis=1)
        sbs_ref[g * nc:(g + 1) * nc, :] = pltpu.roll(sg[:, LANES:2 * LANES], LANES // 2, axis=1)
    a1f, a2f, a2fs = a_ref[0:K8], a_ref[K8:2 * K8], a_ref[2 * K8:3 * K8]
    a1b, a2b, a2bs = a_ref[3 * K8:4 * K8], a_ref[4 * K8:5 * K8], a_ref[5 * K8:6 * K8]

    def step(c, carry):
        hf, hfs, hb, hbs = carry
        cb = nc - 1 - c
        fwd = pl.ds(c, K8, stride=nc)
        bwd = pl.ds(cb, K8, stride=nc)
        sf, sfs = sf_ref[fwd, :], sfs_ref[fwd, :]
        sb, sbs = sb_ref[bwd, :], sbs_ref[bwd, :]
        sf_ref[fwd, :] = hf
        sb_ref[bwd, :] = hb
        return (a1f * hf + a2f * hfs + sf, a1f * hfs + a2fs * hf + sfs,
                a1b * hb + a2b * hbs + sb, a1b * hbs + a2bs * hb + sbs)

    z = jnp.zeros((K8, LANES), F32)
    lax.fori_loop(0, nc, step, (z, z, z, z), unroll=8)
    y = _dot(x, m_ref[...])
    for g in range(K8):
        hg = jnp.concatenate([sf_ref[g * nc:(g + 1) * nc, :], sb_ref[g * nc:(g + 1) * nc, :]], axis=1)
        y = y + _dot(hg.astype(BF16), cs_ref[g * sw:(g + 1) * sw, :])
    for t in range(L):
        y_ref[pl.ds(t, nc, stride=L), :] = y[:, t * LANES:(t + 1) * LANES]
    y_ref[...] += u_ref[...] * d_ref[...]


def _s5(su, params, B, T):
    m8, bs8, cs8, a8, d8 = params
    L = S5_CHUNK
    nc = T // L
    O = m8.shape[0]
    kw = L * LANES
    sw = 4 * S5_STATE
    K8 = LANES // S5_GROUP_CH
    one = pl.Buffered(1)
    return pl.pallas_call(
        _s5_kernel,
        grid=(O, B),
        in_specs=[
            pl.BlockSpec((T, LANES), lambda o, b: (b, o), pipeline_mode=one),
            pl.BlockSpec((None, kw, kw), lambda o, b: (o, 0, 0), pipeline_mode=one),
            pl.BlockSpec((None, kw, K8 * sw), lambda o, b: (o, 0, 0), pipeline_mode=one),
            pl.BlockSpec((None, K8 * sw, kw), lambda o, b: (o, 0, 0), pipeline_mode=one),
            pl.BlockSpec((None, 6 * K8, LANES), lambda o, b: (o, 0, 0)),
            pl.BlockSpec((None, 1, LANES), lambda o, b: (o, 0, 0)),
        ],
        out_specs=pl.BlockSpec((T, LANES), lambda o, b: (b, o)),
        out_shape=jax.ShapeDtypeStruct((B * T, GROUP_W), F32),
        scratch_shapes=[pltpu.VMEM((nc, kw), BF16)] + [pltpu.VMEM((K8 * nc, LANES), F32)] * 4,
        compiler_params=_cparams("arbitrary", "arbitrary"),
        name="s5",
    )(su, m8, bs8, cs8, a8, d8)


def _outproj_kernel(x_ref, hf_ref, hb_ref, og_ref, mg_ref, yb_ref, yc_ref, ys_ref, gw_ref, gb_ref, w_ref, ng_ref,
                    o_ref, on_ref):
    W = GROUP_W
    h = hf_ref[...] + hb_ref[...]
    parts = []
    for j in range(MLSTM_HEADS):
        hj = h[:, j * MLSTM_DH:(j + 1) * MLSTM_DH]
        parts.append(hj * lax.rsqrt(jnp.mean(hj * hj, axis=-1, keepdims=True) + NORM_EPS))
    hn = jnp.concatenate(parts, axis=1) * mg_ref[...]
    ya = (hn * _sigmoid(og_ref[...].astype(F32))).astype(BF16)
    y = ys_ref[...]
    z = 0.5 * y * (1.0 + jnp.tanh(math.sqrt(2.0 / math.pi) * (y + 0.044715 * (y * y * y))))
    yd = (z * _sigmoid(_dot(z.astype(BF16), gw_ref[...]) + gb_ref[...])).astype(BF16)
    acc = x_ref[...] + _dot(ya, w_ref[0:W, :])
    acc = acc + _dot(yb_ref[...], w_ref[W:2 * W, :])
    acc = acc + _dot(yc_ref[...], w_ref[2 * W:3 * W, :])
    acc = acc + _dot(yd, w_ref[3 * W:4 * W, :])
    o_ref[...] = acc
    on_ref[...] = _rms(acc, ng_ref[...]).astype(BF16)


def _outproj(x, hf, hb, proj, mlstm_norm, yb, yc, ys, glu_w, glu_b, w_out, next_norm, tm=512):
    M, D = x.shape
    W = GROUP_W
    tok = lambda i: (i, 0)
    const = lambda i: (0, 0)
    return pl.pallas_call(
        _outproj_kernel,
        grid=(M // tm,),
        in_specs=[
            pl.BlockSpec((tm, D), tok),
            pl.BlockSpec((tm, W), tok), pl.BlockSpec((tm, W), tok),
            pl.BlockSpec((tm, W), lambda i: (i, 3)),
            pl.BlockSpec((1, W), const),
            pl.BlockSpec((tm, W), tok), pl.BlockSpec((tm, W), tok), pl.BlockSpec((tm, W), tok),
            pl.BlockSpec((W, W), const), pl.BlockSpec((1, W), const),
            pl.BlockSpec((4 * W, D), const),
            pl.BlockSpec((1, D), const),
        ],
        out_specs=[pl.BlockSpec((tm, D), tok), pl.BlockSpec((tm, D), tok)],
        out_shape=[jax.ShapeDtypeStruct((M, D), F32), jax.ShapeDtypeStruct((M, D), BF16)],
        compiler_params=_cparams("parallel"),
        name="outproj",
    )(x, hf, hb, proj, mlstm_norm.reshape(1, W), yb, yc, ys, glu_w, glu_b.reshape(1, W), w_out,
      next_norm.reshape(1, D))


def _prepare_layer(w, l):
    W = GROUP_W
    ng = 4 * MLSTM_HEADS
    w_in = w['w_in'][l]
    w_main = jnp.concatenate([w_in[:, :4 * W], w_in[:, 4 * W + ng:]], axis=1).astype(BF16)
    w_gate = jnp.zeros((w_in.shape[0], LANES), BF16).at[:, :ng].set(w_in[:, 4 * W:4 * W + ng].astype(BF16))
    p = {k: w[k][l] for k in ('ffn1_norm', 'mix_norm', 'ffn2_norm', 'mlstm_conv_w', 'mlstm_conv_b', 'mlstm_gate_bias',
                              'mlstm_norm', 'diff_lambda', 'diff_norm', 's5_glu_b')}
    for k in ('ffn1_w_gate', 'ffn1_w_up', 'ffn1_w_down', 'ffn2_w_gate', 'ffn2_w_up', 'ffn2_w_down', 's5_glu_w', 'w_out'):
        p[k] = w[k][l].astype(BF16)
    p['w_main'], p['w_gate'] = w_main, w_gate
    p['na_bias'] = _na_bias_table(w['na_rpb'][l])
    p['s5'] = _s5_params(w['s5_lambda_re'][l], w['s5_lambda_im'][l], w['s5_log_step'][l], w['s5_b_re'][l],
                         w['s5_b_im'][l], w['s5_c_re'][l], w['s5_c_im'][l], w['s5_d'][l])
    return p


def _mixers(x, p, B, T, layer_idx, rope_tabs):
    proj, gates, su = _inproj(x, p['mix_norm'], p['w_main'], p['w_gate'])
    qc, kc = _mlstm_conv(proj, p['mlstm_conv_w'], p['mlstm_conv_b'], T)
    hf, hb = _mlstm(qc, kc, proj, gates, p['mlstm_gate_bias'], B, T)
    qr, kr = _rope(proj, rope_tabs[0], rope_tabs[1], T)
    yb = _diff_attn(qr, kr, proj, p['diff_lambda'], p['diff_norm'], B, T, layer_idx)
    yc = _na_attn(proj, p['na_bias'], B, T)
    ys = _s5(su, p['s5'], B, T)
    return _outproj(x, hf, hb, proj, p['mlstm_norm'], yb, yc, ys, p['s5_glu_w'], p['s5_glu_b'], p['w_out'],
                    p['ffn2_norm'])


def _trunk(x, layers, final_norm):
    B, T, D = x.shape
    x = x.reshape(B * T, D)
    rope_tabs = _rope_tables(T)
    for l, p in enumerate(layers):
        x = _ffn(x, p['ffn1_norm'], p['ffn1_w_gate'], p['ffn1_w_up'], p['ffn1_w_down'])
        x, h = _mixers(x, p, B, T, l, rope_tabs)
        last = l == len(layers) - 1
        x = _ffn_prenormed(x, h, p['ffn2_w_gate'], p['ffn2_w_up'], p['ffn2_w_down'],
                           final_g=final_norm if last else None)
    return x.reshape(B, T, D)


def kernel(x_prompt, x_sample, ffn1_norm, ffn1_w_gate, ffn1_w_up, ffn1_w_down, mix_norm, w_in, mlstm_conv_w, mlstm_conv_b, mlstm_gate_bias, mlstm_norm, diff_lambda, diff_norm, na_rpb, s5_lambda_re, s5_lambda_im, s5_log_step, s5_b_re, s5_b_im, s5_c_re, s5_c_im, s5_d, s5_glu_w, s5_glu_b, w_out, ffn2_norm, ffn2_w_gate, ffn2_w_up, ffn2_w_down, final_norm):
    w = dict(ffn1_norm=ffn1_norm, ffn1_w_gate=ffn1_w_gate, ffn1_w_up=ffn1_w_up, ffn1_w_down=ffn1_w_down,
             mix_norm=mix_norm, w_in=w_in, mlstm_conv_w=mlstm_conv_w, mlstm_conv_b=mlstm_conv_b,
             mlstm_gate_bias=mlstm_gate_bias, mlstm_norm=mlstm_norm, diff_lambda=diff_lambda, diff_norm=diff_norm,
             na_rpb=na_rpb, s5_lambda_re=s5_lambda_re, s5_lambda_im=s5_lambda_im, s5_log_step=s5_log_step,
             s5_b_re=s5_b_re, s5_b_im=s5_b_im, s5_c_re=s5_c_re, s5_c_im=s5_c_im, s5_d=s5_d,
             s5_glu_w=s5_glu_w, s5_glu_b=s5_glu_b, w_out=w_out, ffn2_norm=ffn2_norm, ffn2_w_gate=ffn2_w_gate,
             ffn2_w_up=ffn2_w_up, ffn2_w_down=ffn2_w_down)
    layers = [_prepare_layer(w, l) for l in range(ffn1_norm.shape[0])]
    return (_trunk(x_prompt, layers, final_norm), _trunk(x_sample, layers, final_norm))
```

```python
import functools
import math

import jax
import jax.numpy as jnp
from jax import lax
from jax.experimental import pallas as pl
from jax.experimental.pallas import tpu as pltpu

F32 = jnp.float32
BF16 = jnp.bfloat16

NORM_EPS = 1e-6
GROUP_W = 512
GRID_W = 64
MLSTM_HEADS = 4
MLSTM_DH = 128
MLSTM_CONV = 5
MLSTM_CHUNK = 128
MLSTM_CHUNKS_PER_STEP = 8
DIFF_HEADS = 4
DIFF_DH = 64
ROPE_THETA = 10000.0
NA_HEADS = 8
NA_DH = 64
NA_WIN_ROWS = 8
NA_WIN_W = 16
NA_ROWS_PER_STEP = 32
S5_GROUP_CH = 16
S5_GROUPS = 32
S5_STATE = 64
S5_CHUNK = 16
LANES = 128
BF16_SUBLANES = 16
MASK_VALUE = -1e30
VMEM_LIMIT = 56 * 1024 * 1024


def _cparams(*sem):
    return pltpu.CompilerParams(dimension_semantics=sem, vmem_limit_bytes=VMEM_LIMIT)


def _rms(x, g):
    return x * lax.rsqrt(jnp.mean(x * x, axis=-1, keepdims=True) + NORM_EPS) * g


def _sigmoid(x):
    return 1.0 / (1.0 + jnp.exp(-x))


def _dot(a, b):
    return jnp.dot(a, b, preferred_element_type=F32)


def _dot_nt(a, b):
    return lax.dot_general(a, b, (((1,), (1,)), ((), ())), preferred_element_type=F32)


def _dot_tn(a, b):
    return lax.dot_general(a, b, (((0,), (0,)), ((), ())), preferred_element_type=F32)


def _split3(x):
    hi = x.astype(BF16)
    r1 = x - hi.astype(F32)
    mid = r1.astype(BF16)
    lo = (r1 - mid.astype(F32)).astype(BF16)
    return hi, mid, lo


def _ffn_kernel(x_ref, g_ref, wg_ref, wu_ref, wd_ref, *rest, final):
    if final:
        fg_ref, o_ref, h_ref = rest
    else:
        o_ref, h_ref = rest
    f = pl.program_id(1)
    nf = pl.num_programs(1)

    @pl.when(f == 0)
    def _():
        h_ref[...] = _rms(x_ref[...], g_ref[...]).astype(BF16)
        o_ref[...] = x_ref[...]

    h = h_ref[...]
    gate = _dot(h, wg_ref[...])
    up = _dot(h, wu_ref[...])
    act = (0.5 * gate * _sigmoid(gate) * up).astype(BF16)
    nchunk = 4
    cw = o_ref.shape[1] // nchunk
    for c in range(nchunk):
        o_ref[:, c * cw:(c + 1) * cw] += _dot(act, wd_ref[:, c * cw:(c + 1) * cw])

    if final:
        @pl.when(f == nf - 1)
        def _():
            o_ref[...] = _rms(o_ref[...], fg_ref[...])


def _ffn_prenormed_kernel(x_hbm, h_ref, wg_ref, wu_ref, wd_ref, *rest, final):
    if final:
        fg_ref, o_ref, sem = rest
    else:
        o_ref, sem = rest
    i = pl.program_id(0)
    f = pl.program_id(1)
    tm = o_ref.shape[0]
    residual = pltpu.make_async_copy(x_hbm.at[pl.ds(i * tm, tm), :], o_ref, sem)

    @pl.when(f == 0)
    def _():
        residual.start()

    h = h_ref[...]
    gate = _dot(h, wg_ref[...])
    up = _dot(h, wu_ref[...])
    act = (0.5 * gate * _sigmoid(gate) * up).astype(BF16)

    @pl.when(f == 0)
    def _():
        residual.wait()

    nchunk = 4
    cw = o_ref.shape[1] // nchunk
    for c in range(nchunk):
        o_ref[:, c * cw:(c + 1) * cw] += _dot(act, wd_ref[:, c * cw:(c + 1) * cw])

    if final:
        @pl.when(f == pl.num_programs(1) - 1)
        def _():
            o_ref[...] = _rms(o_ref[...], fg_ref[...])


def _ffn_prenormed(x, h, wg, wu, wd, final_g=None, tm=1024, tf=512):
    M, D = x.shape
    F = wg.shape[1]
    final = final_g is not None
    in_specs = [
        pl.BlockSpec(memory_space=pl.ANY),
        pl.BlockSpec((tm, D), lambda i, f: (i, 0)),
        pl.BlockSpec((D, tf), lambda i, f: (0, f)),
        pl.BlockSpec((D, tf), lambda i, f: (0, f)),
        pl.BlockSpec((tf, D), lambda i, f: (f, 0)),
    ]
    args = [x, h, wg, wu, wd]
    if final:
        in_specs.append(pl.BlockSpec((1, D), lambda i, f: (0, 0)))
        args.append(final_g.reshape(1, D))
    return pl.pallas_call(
        functools.partial(_ffn_prenormed_kernel, final=final),
        grid=(M // tm, F // tf),
        in_specs=in_specs,
        out_specs=pl.BlockSpec((tm, D), lambda i, f: (i, 0)),
        out_shape=jax.ShapeDtypeStruct((M, D), F32),
        scratch_shapes=[pltpu.SemaphoreType.DMA(())],
        compiler_params=_cparams("parallel", "arbitrary"),
        name="ffn_prenormed",
    )(*args)


def _ffn(x, g, wg, wu, wd, final_g=None, tm=1024, tf=512):
    M, D = x.shape
    F = wg.shape[1]
    final = final_g is not None
    in_specs = [
        pl.BlockSpec((tm, D), lambda i, f: (i, 0), pipeline_mode=pl.Buffered(1)),
        pl.BlockSpec((1, D), lambda i, f: (0, 0)),
        pl.BlockSpec((D, tf), lambda i, f: (0, f)),
        pl.BlockSpec((D, tf), lambda i, f: (0, f)),
        pl.BlockSpec((tf, D), lambda i, f: (f, 0)),
    ]
    args = [x, g.reshape(1, D), wg, wu, wd]
    if final:
        in_specs.append(pl.BlockSpec((1, D), lambda i, f: (0, 0)))
        args.append(final_g.reshape(1, D))
    return pl.pallas_call(
        functools.partial(_ffn_kernel, final=final),
        grid=(M // tm, F // tf),
        in_specs=in_specs,
        out_specs=pl.BlockSpec((tm, D), lambda i, f: (i, 0)),
        out_shape=jax.ShapeDtypeStruct((M, D), F32),
        scratch_shapes=[pltpu.VMEM((tm, D), BF16)],
        compiler_params=_cparams("parallel", "arbitrary"),
        name="ffn",
    )(*args)


def _inproj_kernel(x_ref, g_ref, w_ref, wgate_ref, o_ref, og_ref, os_ref, h_ref):
    n = pl.program_id(1)

    @pl.when(n == 0)
    def _():
        h = _rms(x_ref[...], g_ref[...]).astype(BF16)
        h_ref[...] = h
        og_ref[...] = _dot(h, wgate_ref[...])

    res = _dot(h_ref[...], w_ref[...])
    o_ref[...] = res.astype(BF16)

    @pl.when(n == pl.num_programs(1) - 1)
    def _():
        os_ref[...] = res[:, res.shape[1] - GROUP_W:]


def _inproj(x, g, w_main, w_gate, tm=1024, tn=1408):
    M, D = x.shape
    N = w_main.shape[1]
    assert tn >= GROUP_W and N % tn == 0
    return pl.pallas_call(
        _inproj_kernel,
        grid=(M // tm, N // tn),
        in_specs=[
            pl.BlockSpec((tm, D), lambda i, n: (i, 0)),
            pl.BlockSpec((1, D), lambda i, n: (0, 0)),
            pl.BlockSpec((D, tn), lambda i, n: (0, n)),
            pl.BlockSpec((D, LANES), lambda i, n: (0, 0)),
        ],
        out_specs=[
            pl.BlockSpec((tm, tn), lambda i, n: (i, n)),
            pl.BlockSpec((tm, LANES), lambda i, n: (i, 0)),
            pl.BlockSpec((tm, GROUP_W), lambda i, n: (i, 0)),
        ],
        out_shape=[
            jax.ShapeDtypeStruct((M, N), BF16),
            jax.ShapeDtypeStruct((M, LANES), F32),
            jax.ShapeDtypeStruct((M, GROUP_W), F32),
        ],
        scratch_shapes=[pltpu.VMEM((tm, D), BF16)],
        compiler_params=_cparams("parallel", "arbitrary"),
        name="inproj",
    )(x, g.reshape(1, D), w_main, w_gate)


def _conv_kernel(x_ref, prev_ref, next_ref, w_ref, b_ref, q_ref, k_ref, *, blocks_per_seq):
    tb = x_ref.shape[0]
    halo = prev_ref.shape[0]
    pos = pl.program_id(0) % blocks_per_seq
    prev = jnp.where(pos != 0, prev_ref[...], jnp.zeros_like(prev_ref[...]))
    nxt = jnp.where(pos != blocks_per_seq - 1, next_ref[...], jnp.zeros_like(next_ref[...]))
    tile = LANES
    tail = jnp.zeros((2 * tile - tile - 2 * halo, x_ref.shape[1]), BF16)
    ext = jnp.concatenate([prev, x_ref[...], nxt, tail], axis=0)
    pad = MLSTM_CONV // 2
    r = lax.broadcasted_iota(jnp.int32, (tile, 2 * tile), 0)
    c = lax.broadcasted_iota(jnp.int32, (tile, 2 * tile), 1)
    picks = [jnp.where(c == r + halo - pad + j, 1.0, 0.0).astype(BF16) for j in range(MLSTM_CONV)]
    parts = []
    for t0 in range(0, tb, tile):
        win = ext[t0:t0 + 2 * tile]
        acc_t = None
        for j in range(MLSTM_CONV):
            term = _dot(picks[j], win) * w_ref[j:j + 1, :]
            acc_t = term if acc_t is None else acc_t + term
        parts.append(acc_t)
    acc = jnp.concatenate(parts, axis=0) + b_ref[...]
    y = acc * _sigmoid(acc)
    q_ref[...] = y[:, :GROUP_W].astype(BF16)
    k_ref[...] = (y[:, GROUP_W:] * (MLSTM_DH ** -0.5)).astype(BF16)


def _mlstm_conv(proj, conv_w, conv_b, T, tb=512):
    M = proj.shape[0]
    C = 2 * GROUP_W
    halo = BF16_SUBLANES
    r = tb // halo
    nhalo = M // halo
    w = jnp.zeros((8, C), F32).at[:MLSTM_CONV].set(conv_w)
    return pl.pallas_call(
        functools.partial(_conv_kernel, blocks_per_seq=T // tb),
        grid=(M // tb,),
        in_specs=[
            pl.BlockSpec((tb, C), lambda i: (i, 0)),
            pl.BlockSpec((halo, C), lambda i: (jnp.maximum(i * r - 1, 0), 0)),
            pl.BlockSpec((halo, C), lambda i: (jnp.minimum((i + 1) * r, nhalo - 1), 0)),
            pl.BlockSpec((8, C), lambda i: (0, 0)),
            pl.BlockSpec((1, C), lambda i: (0, 0)),
        ],
        out_specs=[
            pl.BlockSpec((tb, GROUP_W), lambda i: (i, 0)),
            pl.BlockSpec((tb, GROUP_W), lambda i: (i, 0)),
        ],
        out_shape=[jax.ShapeDtypeStruct((M, GROUP_W), BF16)] * 2,
        compiler_params=_cparams("parallel"),
        name="mlstm_conv",
    )(proj, proj, proj, w, conv_b.reshape(1, C))


def _mlstm_kernel(qf_ref, kf_ref, vf_ref, gf_ref, qb_ref, kb_ref, vb_ref, gb_ref, bias_ref,
                  of_ref, ob_ref, c_ref, m_ref):
    L = MLSTM_CHUNK
    nsub = qf_ref.shape[0] // L
    H, dh = MLSTM_HEADS, MLSTM_DH

    @pl.when(pl.program_id(1) == 0)
    def _():
        c_ref[...] = jnp.zeros_like(c_ref)
        m_ref[...] = jnp.zeros_like(m_ref)

    row = lax.broadcasted_iota(jnp.int32, (L, L), 0)
    col = lax.broadcasted_iota(jnp.int32, (L, L), 1)
    ones_blk = jnp.ones((L, dh), BF16)

    dirs = ((qf_ref, kf_ref, vf_ref, gf_ref, of_ref), (qb_ref, kb_ref, vb_ref, gb_ref, ob_ref))
    visits = [(d, (j if d == 0 else nsub - 1 - j) * L) for j in range(nsub) for d in range(2)]
    gate_terms = []
    for d, r0 in visits:
        seen = (col <= row) if d == 0 else (col >= row)
        tri = jnp.where(seen, 1.0, 0.0).astype(BF16)
        gt = dirs[d][3][r0:r0 + L, :] + bias_ref[...]
        lf = jnp.minimum(gt, 0.0) - jnp.log(1.0 + jnp.exp(-jnp.abs(gt)))
        hi, mid, lo = _split3(lf)
        bcum = _dot(tri, hi) + _dot(tri, mid) + _dot(tri, lo)
        gate_terms.append((seen, gt, bcum, bcum.T, gt.T))

    for (d, r0), (seen, gt, bcum, bcum_t, gt_t) in zip(visits, gate_terms):
        q_ref, k_ref, v_ref, _, o_ref = dirs[d]
        rows = slice(r0, r0 + L)
        end = L - 1 if d == 0 else 0
        for h in range(H):
            ic = 2 * H * d + h
            fc = 2 * H * d + H + h
            idx = d * H + h
            b_rep = jnp.broadcast_to(bcum[:, fc:fc + 1], (L, dh))
            i_rep = jnp.broadcast_to(gt[:, ic:ic + 1], (L, dh))
            b_row = bcum_t[fc:fc + 1, :]
            i_row = gt_t[ic:ic + 1, :]
            g_tot = b_rep[end:end + 1, :]
            m_prev = m_ref[idx]
            c_prev = c_ref[idx]
            q = q_ref[rows, h * dh:(h + 1) * dh]
            k = k_ref[rows, h * dh:(h + 1) * dh]
            v_aug = jnp.concatenate([v_ref[rows, h * dh:(h + 1) * dh], ones_blk], axis=1)

            dlog = jnp.where(seen, b_rep - b_row + i_row, -jnp.inf)
            inter = b_rep + m_prev
            m_t = jnp.maximum(inter, jnp.max(dlog, axis=1, keepdims=True))
            s = (_dot_nt(q, k) * jnp.exp(dlog - m_t)).astype(BF16)
            s_inter = jnp.exp(inter - m_t)
            num_aug = jnp.concatenate([s_inter, s_inter], axis=1) * _dot(q, c_prev.astype(BF16)) + _dot(s, v_aug)
            den = jnp.maximum(jnp.abs(num_aug[:, dh:]), jnp.exp(-m_t))
            o_ref[rows, h * dh:(h + 1) * dh] = num_aug[:, :dh] / den

            a_rep = g_tot - b_rep + i_rep
            m_new = jnp.maximum(g_tot + m_prev, jnp.max(a_rep, axis=0, keepdims=True))
            decay = jnp.exp(g_tot + m_prev - m_new)
            kw = (k.astype(F32) * jnp.exp(a_rep - m_new)).astype(BF16)
            c_ref[idx] = jnp.concatenate([decay, decay], axis=1) * c_prev + _dot_tn(kw, v_aug)
            m_ref[idx] = m_new


def _mlstm(qc, kc, proj, gates, gate_bias, B, T):
    M = qc.shape[0]
    L = MLSTM_CHUNK
    assert L == LANES == MLSTM_DH
    tb = MLSTM_CHUNKS_PER_STEP * L
    assert T % tb == 0
    nc = T // tb
    W = GROUP_W
    vblk = 2
    fwd = lambda b, i: (b * nc + i, 0)
    bwd = lambda b, i: (b * nc + nc - 1 - i, 0)
    fwd_v = lambda b, i: (b * nc + i, vblk)
    bwd_v = lambda b, i: (b * nc + nc - 1 - i, vblk)
    bias = jnp.zeros((1, LANES), F32).at[0, :4 * MLSTM_HEADS].set(gate_bias)
    return pl.pallas_call(
        _mlstm_kernel,
        grid=(B, nc),
        in_specs=[
            pl.BlockSpec((tb, W), fwd), pl.BlockSpec((tb, W), fwd), pl.BlockSpec((tb, W), fwd_v),
            pl.BlockSpec((tb, LANES), fwd),
            pl.BlockSpec((tb, W), bwd), pl.BlockSpec((tb, W), bwd), pl.BlockSpec((tb, W), bwd_v),
            pl.BlockSpec((tb, LANES), bwd),
            pl.BlockSpec((1, LANES), lambda b, i: (0, 0)),
        ],
        out_specs=[pl.BlockSpec((tb, W), fwd), pl.BlockSpec((tb, W), bwd)],
        out_shape=[jax.ShapeDtypeStruct((M, W), F32)] * 2,
        scratch_shapes=[
            pltpu.VMEM((2 * MLSTM_HEADS, MLSTM_DH, 2 * MLSTM_DH), F32),
            pltpu.VMEM((2 * MLSTM_HEADS, 1, LANES), F32),
        ],
        compiler_params=_cparams("parallel", "arbitrary"),
        name="mlstm",
    )(qc, kc, proj, gates, qc, kc, proj, gates, bias)


def _rope_kernel(x_ref, cos_ref, sin_ref, q_ref, k_ref):
    x = x_ref[...].astype(F32)
    reps = x.shape[1] // LANES
    cos = jnp.concatenate([cos_ref[...]] * reps, axis=1)
    sin = jnp.concatenate([sin_ref[...]] * reps, axis=1)
    lane = lax.broadcasted_iota(jnp.int32, x.shape, 1)
    width = x.shape[1]
    half = DIFF_DH // 2
    rot = jnp.where(lane % DIFF_DH < half, pltpu.roll(x, width - half, axis=1), pltpu.roll(x, half, axis=1))
    y = x * cos + rot * sin
    q_ref[...] = (y[:, :GROUP_W] * (DIFF_DH ** -0.5 * math.log2(math.e))).astype(BF16)
    k_ref[...] = y[:, GROUP_W:].astype(BF16)


def _rope_tables(T):
    half = DIFF_DH // 2
    inv = 1.0 / (ROPE_THETA ** (jnp.arange(0, DIFF_DH, 2, dtype=F32) / DIFF_DH))
    ang = jnp.arange(T, dtype=F32)[:, None] * inv[None, :]
    cos, sin = jnp.cos(ang), jnp.sin(ang)
    cos_t = jnp.concatenate([cos, cos] * (LANES // DIFF_DH), axis=1)
    sin_t = jnp.concatenate([-sin, sin] * (LANES // DIFF_DH), axis=1)
    return cos_t, sin_t


def _rope(proj, cos_t, sin_t, T, tb=512):
    M = proj.shape[0]
    nb = T // tb
    return pl.pallas_call(
        _rope_kernel,
        grid=(M // tb,),
        in_specs=[
            pl.BlockSpec((tb, 2 * GROUP_W), lambda i: (i, 2)),
            pl.BlockSpec((tb, LANES), lambda i: (i % nb, 0)),
            pl.BlockSpec((tb, LANES), lambda i: (i % nb, 0)),
        ],
        out_specs=[pl.BlockSpec((tb, GROUP_W), lambda i: (i, 0))] * 2,
        out_shape=[jax.ShapeDtypeStruct((M, GROUP_W), BF16)] * 2,
        compiler_params=_cparams("parallel"),
        name="rope",
    )(proj, cos_t, sin_t)


def _diff_kernel(q_ref, k_ref, v_ref, lp_ref, g_ref, o_ref, q2_ref, m_ref, acc_ref, *, lam_init, tks):
    ki = pl.program_id(3)
    tq = q_ref.shape[0]
    dv = v_ref.shape[1]
    ones = jnp.ones((tks, dv), BF16)

    @pl.when(ki == 0)
    def _():
        q = q_ref[...]
        lane = lax.broadcasted_iota(jnp.int32, q.shape, 1)
        zero = jnp.zeros_like(q)
        q2_ref[0:tq, :] = jnp.where(lane < DIFF_DH, q, zero)
        q2_ref[tq:2 * tq, :] = jnp.where(lane < DIFF_DH, zero, q)
        m_ref[...] = jnp.full_like(m_ref, -jnp.inf)
        acc_ref[...] = jnp.zeros_like(acc_ref)

    q2 = q2_ref[...]
    for j in range(k_ref.shape[0] // tks):
        s = _dot_nt(q2, k_ref[j * tks:(j + 1) * tks, :])
        tiles = [s[:, c * LANES:(c + 1) * LANES] for c in range(tks // LANES)]
        m_prev = m_ref[...]
        m_new = jnp.maximum(m_prev, jnp.max(functools.reduce(jnp.maximum, tiles), axis=1, keepdims=True))
        alpha = jnp.exp2(m_prev - m_new)
        p = jnp.concatenate([jnp.exp2(t - m_new).astype(BF16) for t in tiles], axis=1)
        v_aug = jnp.concatenate([v_ref[j * tks:(j + 1) * tks, :], ones], axis=1)
        acc_ref[...] = jnp.concatenate([alpha, alpha], axis=1) * acc_ref[...] + _dot(p, v_aug)
        m_ref[...] = m_new

    @pl.when(ki == pl.num_programs(3) - 1)
    def _():
        lp = lp_ref[...]
        lam = (jnp.exp(jnp.sum(lp[0:1] * lp[1:2], axis=1, keepdims=True))
               - jnp.exp(jnp.sum(lp[2:3] * lp[3:4], axis=1, keepdims=True)) + lam_init)
        o = acc_ref[:, 0:dv] / acc_ref[:, dv:2 * dv]
        out = o[0:tq] - lam * o[tq:2 * tq]
        o_ref[...] = (_rms(out, g_ref[...]) * (1.0 - lam_init)).astype(BF16)


def _diff_attn(qr, kr, proj, lam_params, norm_g, B, T, layer_idx, tq=1024, tk=8192, tks=512):
    M = qr.shape[0]
    dv = 2 * DIFF_DH
    tk = min(tk, T)
    nq, nk = T // tq, T // tk
    vblk0 = (6 * GROUP_W) // dv
    lam_init = 0.8 - 0.6 * math.exp(-0.3 * layer_idx)
    return pl.pallas_call(
        functools.partial(_diff_kernel, lam_init=lam_init, tks=tks),
        grid=(B, DIFF_HEADS, nq, nk),
        in_specs=[
            pl.BlockSpec((tq, dv), lambda b, h, qi, ki: (b * nq + qi, h)),
            pl.BlockSpec((tk, dv), lambda b, h, qi, ki: (b * nk + ki, h)),
            pl.BlockSpec((tk, dv), lambda b, h, qi, ki: (b * nk + ki, vblk0 + h)),
            pl.BlockSpec((4, DIFF_DH), lambda b, h, qi, ki: (0, 0)),
            pl.BlockSpec((1, dv), lambda b, h, qi, ki: (0, 0)),
        ],
        out_specs=pl.BlockSpec((tq, dv), lambda b, h, qi, ki: (b * nq + qi, h)),
        out_shape=jax.ShapeDtypeStruct((M, GROUP_W), BF16),
        scratch_shapes=[
            pltpu.VMEM((2 * tq, dv), BF16),
            pltpu.VMEM((2 * tq, LANES), F32),
            pltpu.VMEM((2 * tq, 2 * dv), F32),
        ],
        compiler_params=_cparams("parallel", "parallel", "parallel", "arbitrary"),
        name="diff_attn",
    )(qr, kr, proj, lam_params, norm_g.reshape(1, dv))


def _na_kernel(q_ref, k_ref, v_ref, bias_ref, o_ref, *, rows):
    g = pl.program_id(2)
    win = NA_WIN_ROWS * GRID_W
    lane = lax.broadcasted_iota(jnp.int32, (GRID_W, 2 * NA_DH), 1)
    first = lane < NA_DH
    w = 2 * NA_DH
    ones = jnp.ones((win, w), BF16)
    scores, starts = [], []
    for jr in range(NA_ROWS_PER_STEP):
        r = g * NA_ROWS_PER_STEP + jr
        rs = jnp.clip(r - NA_WIN_ROWS // 2, 0, rows - NA_WIN_ROWS)
        start = pl.multiple_of(rs * GRID_W, GRID_W)
        q = q_ref[jr * GRID_W:(jr + 1) * GRID_W, :]
        zero = jnp.zeros_like(q)
        q2 = jnp.concatenate([jnp.where(first, q, zero), jnp.where(first, zero, q)], axis=0)
        s = _dot_nt(q2, k_ref[pl.ds(start, win), :]) * (NA_DH ** -0.5 * math.log2(math.e)) + bias_ref[r - rs]
        scores.append(s)
        starts.append(start)
    for jr in range(NA_ROWS_PER_STEP):
        s = scores[jr]
        e = jnp.exp2(s - jnp.max(s, axis=1, keepdims=True)).astype(BF16)
        pv = _dot(e, jnp.concatenate([v_ref[pl.ds(starts[jr], win), :], ones], axis=1))
        pv = pv[:, :w] / pv[:, w:]
        o_ref[jr * GRID_W:(jr + 1) * GRID_W, :] = jnp.where(first, pv[:GRID_W], pv[GRID_W:]).astype(BF16)


def _na_bias_table(rpb):
    c = jnp.arange(GRID_W)
    col_start = jnp.clip(c - NA_WIN_W // 2, 0, GRID_W - NA_WIN_W)
    kc = jnp.arange(GRID_W)
    valid = (kc[None, :] >= col_start[:, None]) & (kc[None, :] < col_start[:, None] + NA_WIN_W)
    col_off = kc[None, :] - c[:, None] + NA_WIN_W - 1
    p = jnp.arange(NA_WIN_ROWS)
    k = jnp.arange(NA_WIN_ROWS)
    row_off = k[None, :] - p[:, None] + NA_WIN_ROWS - 1
    pick_row = (row_off[..., None] == jnp.arange(2 * NA_WIN_ROWS - 1)).astype(F32)
    pick_col = (col_off[..., None] == jnp.arange(2 * NA_WIN_W - 1)).astype(F32)
    t = jnp.einsum('hrq,pkr,cjq->hpckj', rpb.astype(F32), pick_row, pick_col,
                   precision=lax.Precision.HIGHEST) * math.log2(math.e)
    t = jnp.where(valid[None, None, :, None, :], t, MASK_VALUE)
    t = t.reshape(NA_HEADS, NA_WIN_ROWS, GRID_W, NA_WIN_ROWS * GRID_W)
    t = t.reshape(NA_HEADS // 2, 2, NA_WIN_ROWS, GRID_W, NA_WIN_ROWS * GRID_W).transpose(0, 2, 1, 3, 4)
    return t.reshape(NA_HEADS // 2, NA_WIN_ROWS, 2 * GRID_W, NA_WIN_ROWS * GRID_W)


def _na_attn(proj, bias_table, B, T):
    M = proj.shape[0]
    rows = T // GRID_W
    assert rows % NA_ROWS_PER_STEP == 0 and rows >= NA_WIN_ROWS
    tq = NA_ROWS_PER_STEP * GRID_W
    ng = rows // NA_ROWS_PER_STEP
    hp = NA_HEADS // 2
    w = 2 * NA_DH
    qblk0, kblk0, vblk0 = (7 * GROUP_W) // w, (8 * GROUP_W) // w, (9 * GROUP_W) // w
    return pl.pallas_call(
        functools.partial(_na_kernel, rows=rows),
        grid=(B, hp, ng),
        in_specs=[
            pl.BlockSpec((tq, w), lambda b, h, g: (b * ng + g, qblk0 + h)),
            pl.BlockSpec((T, w), lambda b, h, g: (b, kblk0 + h)),
            pl.BlockSpec((T, w), lambda b, h, g: (b, vblk0 + h)),
            pl.BlockSpec((None, NA_WIN_ROWS, 2 * GRID_W, NA_WIN_ROWS * GRID_W), lambda b, h, g: (h, 0, 0, 0)),
        ],
        out_specs=pl.BlockSpec((tq, w), lambda b, h, g: (b * ng + g, h)),
        out_shape=jax.ShapeDtypeStruct((M, GROUP_W), BF16),
        compiler_params=_cparams("parallel", "parallel", "arbitrary"),
        name="na_attn",
    )(proj, proj, proj, bias_table)


def _s5_params(lam_re, lam_im, log_step, b_re, b_im, c_re, c_im, d_skip):
    L, c, P, G = S5_CHUNK, S5_GROUP_CH, S5_STATE, S5_GROUPS
    K8 = LANES // c
    O = G // K8
    lam = lax.complex(lam_re.astype(F32), lam_im.astype(F32))
    step = jnp.exp(log_step.astype(F32))[..., None]
    log_lam_bar = lam * step
    lam_bar = jnp.exp(log_lam_bar)
    b_mat = lax.complex(b_re.astype(F32), b_im.astype(F32))
    c_mat = lax.complex(c_re.astype(F32), c_im.astype(F32))
    b_bar = ((lam_bar - 1.0) / lam)[..., None] * b_mat[None]
    pw = jnp.exp(log_lam_bar[:, None] * jnp.arange(L + 1, dtype=F32)[None, :, None, None])
    cb = c_mat[None, :, :, None, :] * b_bar.transpose(0, 1, 3, 2)[:, :, None, :, :]
    s_i = jnp.arange(L)[:, None]
    t_i = jnp.arange(L)[None, :]
    lag = jnp.stack([t_i - s_i, s_i - t_i])
    lag_pw = jnp.where((lag >= 0)[..., None, None],
                       jnp.exp(log_lam_bar[:, None, None] * jnp.maximum(lag, 0).astype(F32)[..., None, None]),
                       0.0)
    m = jnp.einsum('dgoip,dstgp->gsito', jnp.concatenate([cb.real, -cb.imag], axis=-1),
                   jnp.concatenate([lag_pw.real, lag_pw.imag], axis=-1), precision=lax.Precision.HIGH)
    eye = jnp.eye(K8, dtype=F32)
    lane_of = jnp.arange(K8)[:, None] * c + jnp.arange(c)[None, :]
    sel = (jnp.arange(LANES)[None, None, :] == lane_of[:, :, None]).astype(BF16)
    m8 = jnp.einsum('agsitc,gcl->asgitl', m.astype(BF16).reshape(O, K8, L, c, L, c), sel,
                    preferred_element_type=BF16).reshape(O, L * LANES, L * LANES)
    ef = pw[0][L - 1 - jnp.arange(L)][:, :, :, None] * b_bar[0][None]
    eb = pw[1][jnp.arange(L)][:, :, :, None] * b_bar[1][None]
    to_rows = lambda e: e.transpose(1, 0, 3, 2)
    ef, eb = to_rows(ef), to_rows(eb)
    bs = jnp.concatenate([ef.real, ef.imag, eb.real, eb.imag], axis=-1).astype(BF16)
    bs8 = jnp.einsum('agsik,gh->asgihk', bs.reshape(O, K8, L, c, 4 * P), eye.astype(BF16),
                     preferred_element_type=BF16).reshape(O, L * LANES, K8 * 4 * P)
    of = c_mat[:, None] * pw[0][1:L + 1].transpose(1, 0, 2)[:, :, None, :]
    ob = c_mat[:, None] * pw[1][L - jnp.arange(L)].transpose(1, 0, 2)[:, :, None, :]
    to_cols = lambda o: o.transpose(0, 3, 1, 2)
    of, ob = to_cols(of), to_cols(ob)
    cs = jnp.concatenate([of.real, -of.imag, ob.real, -ob.imag], axis=1)
    cs8 = jnp.einsum('agktc,gcl->agktl', cs.astype(BF16).reshape(O, K8, 4 * P, L, c), sel,
                     preferred_element_type=BF16).reshape(O, K8 * 4 * P, L * LANES)
    al = pw[:, L]
    a1 = jnp.concatenate([al.real, al.real], axis=-1)
    a2 = jnp.concatenate([-al.imag, al.imag], axis=-1)
    a8 = jnp.stack([a1, a2, -a2], axis=1)
    a8 = a8.reshape(2, 3, O, K8, LANES).transpose(2, 0, 1, 3, 4).reshape(O, 6 * K8, LANES)
    d8 = d_skip.astype(F32).reshape(O, 1, LANES)
    return m8.astype(BF16), bs8.astype(BF16), cs8.astype(BF16), a8, d8


def _s5_kernel(u_ref, m_ref, bs_ref, cs_ref, a_ref, d_ref, y_ref, x_ref, sf_ref, sb_ref, sfs_ref, sbs_ref):
    L = S5_CHUNK
    K8 = LANES // S5_GROUP_CH
    sw = 4 * S5_STATE
    nc = u_ref.shape[0] // L
    for s in range(L):
        x_ref[:, s * LANES:(s + 1) * LANES] = u_ref[pl.ds(s, nc, stride=L), :].astype(BF16)
    x = x_ref[...]
    for g in range(K8):
        sg = _dot(x, bs_ref[:, g * sw:(g + 1) * sw])
        sf_ref[g * nc:(g + 1) * nc, :] = sg[:, 0:LANES]
        sb_ref[g * nc:(g + 1) * nc, :] = sg[:, LANES:2 * LANES]
        sfs_ref[g * nc:(g + 1) * nc, :] = pltpu.roll(sg[:, 0:LANES], LANES // 2, axis=1)
        sbs_ref[g * nc:(g + 1) * nc, :] = pltpu.roll(sg[:, LANES:2 * LANES], LANES // 2, axis=1)
    a1f, a2f, a2fs = a_ref[0:K8], a_ref[K8:2 * K8], a_ref[2 * K8:3 * K8]
    a1b, a2b, a2bs = a_ref[3 * K8:4 * K8], a_ref[4 * K8:5 * K8], a_ref[5 * K8:6 * K8]

    def step(c, carry):
        hf, hfs, hb, hbs = carry
        cb = nc - 1 - c
        fwd = pl.ds(c, K8, stride=nc)
        bwd = pl.ds(cb, K8, stride=nc)
        sf, sfs = sf_ref[fwd, :], sfs_ref[fwd, :]
        sb, sbs = sb_ref[bwd, :], sbs_ref[bwd, :]
        sf_ref[fwd, :] = hf
        sb_ref[bwd, :] = hb
        return (a1f * hf + a2f * hfs + sf, a1f * hfs + a2fs * hf + sfs,
                a1b * hb + a2b * hbs + sb, a1b * hbs + a2bs * hb + sbs)

    z = jnp.zeros((K8, LANES), F32)
    lax.fori_loop(0, nc, step, (z, z, z, z), unroll=8)
    y = _dot(x, m_ref[...])
    for g in range(K8):
        hg = jnp.concatenate([sf_ref[g * nc:(g + 1) * nc, :], sb_ref[g * nc:(g + 1) * nc, :]], axis=1)
        y = y + _dot(hg.astype(BF16), cs_ref[g * sw:(g + 1) * sw, :])
    for t in range(L):
        y_ref[pl.ds(t, nc, stride=L), :] = y[:, t * LANES:(t + 1) * LANES]
    y_ref[...] += u_ref[...] * d_ref[...]


def _s5(su, params, B, T):
    m8, bs8, cs8, a8, d8 = params
    L = S5_CHUNK
    nc = T // L
    O = m8.shape[0]
    kw = L * LANES
    sw = 4 * S5_STATE
    K8 = LANES // S5_GROUP_CH
    one = pl.Buffered(1)
    return pl.pallas_call(
        _s5_kernel,
        grid=(O, B),
        in_specs=[
            pl.BlockSpec((T, LANES), lambda o, b: (b, o), pipeline_mode=one),
            pl.BlockSpec((None, kw, kw), lambda o, b: (o, 0, 0), pipeline_mode=one),
            pl.BlockSpec((None, kw, K8 * sw), lambda o, b: (o, 0, 0), pipeline_mode=one),
            pl.BlockSpec((None, K8 * sw, kw), lambda o, b: (o, 0, 0), pipeline_mode=one),
            pl.BlockSpec((None, 6 * K8, LANES), lambda o, b: (o, 0, 0)),
            pl.BlockSpec((None, 1, LANES), lambda o, b: (o, 0, 0)),
        ],
        out_specs=pl.BlockSpec((T, LANES), lambda o, b: (b, o)),
        out_shape=jax.ShapeDtypeStruct((B * T, GROUP_W), F32),
        scratch_shapes=[pltpu.VMEM((nc, kw), BF16)] + [pltpu.VMEM((K8 * nc, LANES), F32)] * 4,
        compiler_params=_cparams("arbitrary", "arbitrary"),
        name="s5",
    )(su, m8, bs8, cs8, a8, d8)


def _outproj_kernel(x_ref, hf_ref, hb_ref, og_ref, mg_ref, yb_ref, yc_ref, ys_ref, gw_ref, gb_ref, w_ref, ng_ref,
                    o_ref, on_ref):
    W = GROUP_W
    h = hf_ref[...] + hb_ref[...]
    parts = []
    for j in range(MLSTM_HEADS):
        hj = h[:, j * MLSTM_DH:(j + 1) * MLSTM_DH]
        parts.append(hj * lax.rsqrt(jnp.mean(hj * hj, axis=-1, keepdims=True) + NORM_EPS))
    hn = jnp.concatenate(parts, axis=1) * mg_ref[...]
    ya = (hn * _sigmoid(og_ref[...].astype(F32))).astype(BF16)
    y = ys_ref[...]
    z = 0.5 * y * (1.0 + jnp.tanh(math.sqrt(2.0 / math.pi) * (y + 0.044715 * (y * y * y))))
    yd = (z * _sigmoid(_dot(z.astype(BF16), gw_ref[...]) + gb_ref[...])).astype(BF16)
    acc = x_ref[...] + _dot(ya, w_ref[0:W, :])
    acc = acc + _dot(yb_ref[...], w_ref[W:2 * W, :])
    acc = acc + _dot(yc_ref[...], w_ref[2 * W:3 * W, :])
    acc = acc + _dot(yd, w_ref[3 * W:4 * W, :])
    o_ref[...] = acc
    on_ref[...] = _rms(acc, ng_ref[...]).astype(BF16)


def _outproj(x, hf, hb, proj, mlstm_norm, yb, yc, ys, glu_w, glu_b, w_out, next_norm, tm=512):
    M, D = x.shape
    W = GROUP_W
    tok = lambda i: (i, 0)
    const = lambda i: (0, 0)
    return pl.pallas_call(
        _outproj_kernel,
        grid=(M // tm,),
        in_specs=[
            pl.BlockSpec((tm, D), tok),
            pl.BlockSpec((tm, W), tok), pl.BlockSpec((tm, W), tok),
            pl.BlockSpec((tm, W), lambda i: (i, 3)),
            pl.BlockSpec((1, W), const),
            pl.BlockSpec((tm, W), tok), pl.BlockSpec((tm, W), tok), pl.BlockSpec((tm, W), tok),
            pl.BlockSpec((W, W), const), pl.BlockSpec((1, W), const),
            pl.BlockSpec((4 * W, D), const),
            pl.BlockSpec((1, D), const),
        ],
        out_specs=[pl.BlockSpec((tm, D), tok), pl.BlockSpec((tm, D), tok)],
        out_shape=[jax.ShapeDtypeStruct((M, D), F32), jax.ShapeDtypeStruct((M, D), BF16)],
        compiler_params=_cparams("parallel"),
        name="outproj",
    )(x, hf, hb, proj, mlstm_norm.reshape(1, W), yb, yc, ys, glu_w, glu_b.reshape(1, W), w_out,
      next_norm.reshape(1, D))


def _prepare_layer(w, l):
    W = GROUP_W
    ng = 4 * MLSTM_HEADS
    w_in = w['w_in'][l]
    w_main = jnp.concatenate([w_in[:, :4 * W], w_in[:, 4 * W + ng:]], axis=1).astype(BF16)
    w_gate = jnp.zeros((w_in.shape[0], LANES), BF16).at[:, :ng].set(w_in[:, 4 * W:4 * W + ng].astype(BF16))
    p = {k: w[k][l] for k in ('ffn1_norm', 'mix_norm', 'ffn2_norm', 'mlstm_conv_w', 'mlstm_conv_b', 'mlstm_gate_bias',
                              'mlstm_norm', 'diff_lambda', 'diff_norm', 's5_glu_b')}
    for k in ('ffn1_w_gate', 'ffn1_w_up', 'ffn1_w_down', 'ffn2_w_gate', 'ffn2_w_up', 'ffn2_w_down', 's5_glu_w', 'w_out'):
        p[k] = w[k][l].astype(BF16)
    p['w_main'], p['w_gate'] = w_main, w_gate
    p['na_bias'] = _na_bias_table(w['na_rpb'][l])
    p['s5'] = _s5_params(w['s5_lambda_re'][l], w['s5_lambda_im'][l], w['s5_log_step'][l], w['s5_b_re'][l],
                         w['s5_b_im'][l], w['s5_c_re'][l], w['s5_c_im'][l], w['s5_d'][l])
    return p


def _mixers(x, p, B, T, layer_idx, rope_tabs):
    proj, gates, su = _inproj(x, p['mix_norm'], p['w_main'], p['w_gate'])
    qc, kc = _mlstm_conv(proj, p['mlstm_conv_w'], p['mlstm_conv_b'], T)
    hf, hb = _mlstm(qc, kc, proj, gates, p['mlstm_gate_bias'], B, T)
    qr, kr = _rope(proj, rope_tabs[0], rope_tabs[1], T)
    yb = _diff_attn(qr, kr, proj, p['diff_lambda'], p['diff_norm'], B, T, layer_idx)
    yc = _na_attn(proj, p['na_bias'], B, T)
    ys = _s5(su, p['s5'], B, T)
    return _outproj(x, hf, hb, proj, p['mlstm_norm'], yb, yc, ys, p['s5_glu_w'], p['s5_glu_b'], p['w_out'],
                    p['ffn2_norm'])


def _trunk(x, layers, final_norm):
    B, T, D = x.shape
    x = x.reshape(B * T, D)
    rope_tabs = _rope_tables(T)
    for l, p in enumerate(layers):
        x = _ffn(x, p['ffn1_norm'], p['ffn1_w_gate'], p['ffn1_w_up'], p['ffn1_w_down'])
        x, h = _mixers(x, p, B, T, l, rope_tabs)
        last = l == len(layers) - 1
        x = _ffn_prenormed(x, h, p['ffn2_w_gate'], p['ffn2_w_up'], p['ffn2_w_down'],
                           final_g=final_norm if last else None)
    return x.reshape(B, T, D)


def kernel(x_prompt, x_sample, ffn1_norm, ffn1_w_gate, ffn1_w_up, ffn1_w_down, mix_norm, w_in, mlstm_conv_w, mlstm_conv_b, mlstm_gate_bias, mlstm_norm, diff_lambda, diff_norm, na_rpb, s5_lambda_re, s5_lambda_im, s5_log_step, s5_b_re, s5_b_im, s5_c_re, s5_c_im, s5_d, s5_glu_w, s5_glu_b, w_out, ffn2_norm, ffn2_w_gate, ffn2_w_up, ffn2_w_down, final_norm):
    w = dict(ffn1_norm=ffn1_norm, ffn1_w_gate=ffn1_w_gate, ffn1_w_up=ffn1_w_up, ffn1_w_down=ffn1_w_down,
             mix_norm=mix_norm, w_in=w_in, mlstm_conv_w=mlstm_conv_w, mlstm_conv_b=mlstm_conv_b,
             mlstm_gate_bias=mlstm_gate_bias, mlstm_norm=mlstm_norm, diff_lambda=diff_lambda, diff_norm=diff_norm,
             na_rpb=na_rpb, s5_lambda_re=s5_lambda_re, s5_lambda_im=s5_lambda_im, s5_log_step=s5_log_step,
             s5_b_re=s5_b_re, s5_b_im=s5_b_im, s5_c_re=s5_c_re, s5_c_im=s5_c_im, s5_d=s5_d,
             s5_glu_w=s5_glu_w, s5_glu_b=s5_glu_b, w_out=w_out, ffn2_norm=ffn2_norm, ffn2_w_gate=ffn2_w_gate,
             ffn2_w_up=ffn2_w_up, ffn2_w_down=ffn2_w_down)
    layers = [_prepare_layer(w, l) for l in range(ffn1_norm.shape[0])]
    return (_trunk(x_prompt, layers, final_norm), _trunk(x_sample, layers, final_norm))
```
